```python
import jax, jax.numpy as jnp
from jax import lax
import numpy as np

D_MODEL = 1024
BATCH = 8
SEQ = 2048
DEPTH = 1
DEC_BATCH = 32
DEC_SEQ = 1
PAST_LEN = 8192
PAGE_SIZE = 128

D_MIX = D_MODEL
HEAD_DIM = 64
POOL_WIDTH = D_MIX // 4
POOL_WINDOWS = (2, 4, 8, 16)
POOL_GROUP = POOL_WIDTH // len(POOL_WINDOWS)
POOL_BUF = max(POOL_WINDOWS) - 1
MOBA_WIDTH = D_MIX // 2
MOBA_HEADS = MOBA_WIDTH // HEAD_DIM
MOBA_BLOCK = 256
MOBA_TOPK = 3
MOBA_Q_CHUNK = 32
MEM_WIDTH = D_MIX - POOL_WIDTH - MOBA_WIDTH
MEM_HEADS = MEM_WIDTH // HEAD_DIM
MEM_LEN = 256
ROT_DIM = HEAD_DIM // 4
ROPE_THETA = 500000.0
D_FF = 2816
RMS_EPS = 1e-6
IN_COLS = POOL_WIDTH + 3 * MOBA_WIDTH + MEM_WIDTH

kernel_name = "hymba_pool_moba_memxattn_macaron_step"


def rmsnorm(x, g):
    xf = x.astype(jnp.float32)
    y = xf * lax.rsqrt(jnp.mean(xf * xf, axis=-1, keepdims=True) + RMS_EPS)
    return (y * g.astype(jnp.float32)).astype(x.dtype)


def rotary(x, pos):
    half = ROT_DIM // 2
    inv = jnp.power(jnp.float32(ROPE_THETA), -jnp.arange(half, dtype=jnp.float32) * (2.0 / ROT_DIM))
    ang = pos.astype(jnp.float32)[:, None] * inv[None, :]
    cos = jnp.cos(ang)[None, :, None, :]
    sin = jnp.sin(ang)[None, :, None, :]
    xr = x[..., :ROT_DIM].astype(jnp.float32)
    x1, x2 = xr[..., :half], xr[..., half:]
    rot = jnp.concatenate([x1 * cos - x2 * sin, x2 * cos + x1 * sin], axis=-1).astype(x.dtype)
    return jnp.concatenate([rot, x[..., ROT_DIM:]], axis=-1)


def swiglu_ffn(x, g, w_in, w_out):
    gate, up = jnp.split(rmsnorm(x, g) @ w_in, 2, axis=-1)
    return (jax.nn.silu(gate) * up) @ w_out


def pool_mixer(u, buf, pos, pool_w, pool_scale):
    B, T, C = u.shape
    ext = jnp.concatenate([buf, u], axis=1).astype(jnp.float32)
    cs = jnp.concatenate([jnp.zeros((B, 1, C), jnp.float32), jnp.cumsum(ext, axis=1)], axis=1)
    end = cs[:, POOL_BUF + 1:]
    means = []
    for gi, w in enumerate(POOL_WINDOWS):
        sl = slice(gi * POOL_GROUP, (gi + 1) * POOL_GROUP)
        start = cs[:, POOL_BUF + 1 - w: POOL_BUF + 1 - w + T, sl]
        cnt = jnp.minimum(w, pos + 1).astype(jnp.float32)[None, :, None]
        means.append((end[..., sl] - start) / cnt)
    pooled = (jnp.concatenate(means, axis=-1) - ext[:, POOL_BUF:]).astype(u.dtype)
    pg = pooled.reshape(B, T, len(POOL_WINDOWS), POOL_GROUP)
    y = jnp.einsum("btgc,gcd->btgd", pg, pool_w).reshape(B, T, C) * pool_scale
    new_buf = ext[:, -POOL_BUF:].astype(u.dtype)
    return y, new_buf


def moba_attention(q, k, v, q_pos):
    B, Q, H, Dh = q.shape
    L = k.shape[1]
    nb = max(-(-L // MOBA_BLOCK), MOBA_TOPK)
    pad = nb * MOBA_BLOCK - L
    kb = jnp.pad(k, ((0, 0), (0, pad), (0, 0), (0, 0))).reshape(B, nb, MOBA_BLOCK, H, Dh)
    vb = jnp.pad(v, ((0, 0), (0, pad), (0, 0), (0, 0))).reshape(B, nb, MOBA_BLOCK, H, Dh)
    kmean = jnp.mean(kb.astype(jnp.float32), axis=2)
    qc = min(MOBA_Q_CHUNK, Q)
    n_ch = -(-Q // qc)
    qpad = n_ch * qc - Q
    qs = jnp.pad(q, ((0, 0), (0, qpad), (0, 0), (0, 0))).reshape(B, n_ch, qc, H, Dh).transpose(1, 0, 2, 3, 4)
    ps = jnp.pad(q_pos, (0, qpad), mode="edge").reshape(n_ch, qc)
    bi = jnp.arange(B)[:, None, None]
    hi = jnp.arange(H)[None, :, None]
    blk_pos = jnp.arange(MOBA_BLOCK, dtype=jnp.int32)
    scale = Dh ** -0.5

    def one_chunk(args):
        qch, pch = args
        qf = qch.astype(jnp.float32)
        own = pch // MOBA_BLOCK
        s = jnp.einsum("bqhd,bnhd->bhqn", qf, kmean)
        fully_past = jnp.arange(nb)[None, :] < own[:, None]
        s = jnp.where(fully_past[None, None], s, -jnp.inf)
        top_s, top_i = lax.top_k(s, MOBA_TOPK)
        idx = jnp.concatenate([top_i, jnp.broadcast_to(own[None, None, :, None], (B, H, qc, 1))], axis=-1)
        masks = [jnp.isfinite(top_s[..., i])[..., None] for i in range(MOBA_TOPK)]
        own_pos = own[:, None] * MOBA_BLOCK + blk_pos[None, :]
        masks.append((own_pos <= pch[:, None])[None, None])
        logits = []
        for i in range(MOBA_TOPK + 1):
            ks = kb[bi, idx[..., i], :, hi].astype(jnp.float32)
            lg = jnp.einsum("bqhd,bhqkd->bhqk", qf, ks) * scale
            logits.append(jnp.where(masks[i], lg, -jnp.inf))
        p = jax.nn.softmax(jnp.concatenate(logits, axis=-1), axis=-1)
        p = p.reshape(B, H, qc, MOBA_TOPK + 1, MOBA_BLOCK)
        out = sum(jnp.einsum("bhqk,bhqkd->bqhd", p[:, :, :, i], vb[bi, idx[..., i], :, hi].astype(jnp.float32))
                  for i in range(MOBA_TOPK + 1))
        return out.astype(q.dtype)

    out = lax.map(one_chunk, (qs, ps))
    return out.transpose(1, 0, 2, 3, 4).reshape(B, n_ch * qc, H, Dh)[:, :Q]


def mem_attention(q, mk, mv):
    lg = jnp.einsum("bthd,bmhd->bhtm", q.astype(jnp.float32), mk.astype(jnp.float32)) * (HEAD_DIM ** -0.5)
    p = jax.nn.softmax(lg, axis=-1)
    return jnp.einsum("bhtm,bmhd->bthd", p, mv.astype(jnp.float32)).astype(q.dtype)


def mem_kv(mem, mem_norm, mem_w_kv, mem_k_norm):
    B, M, _ = mem.shape
    mk, mv = jnp.split(rmsnorm(mem, mem_norm) @ mem_w_kv, 2, axis=-1)
    mk = rmsnorm(mk.reshape(B, M, MEM_HEADS, HEAD_DIM), mem_k_norm)
    return mk, mv.reshape(B, M, MEM_HEADS, HEAD_DIM)


def trunk_layer(x, pos, k_past, v_past, pool_buf, mk, mv,
                ffn1_norm, ffn1_w_in, ffn1_w_out, mix_norm, w_in, pool_w, pool_scale,
                q_norm, k_norm, mem_q_norm, w_out, ffn2_norm, ffn2_w_in, ffn2_w_out):
    B, T, _ = x.shape
    x = x + 0.5 * swiglu_ffn(x, ffn1_norm, ffn1_w_in, ffn1_w_out)
    h = rmsnorm(x, mix_norm)
    proj = h @ w_in
    c1 = POOL_WIDTH
    c2 = c1 + MOBA_WIDTH
    c3 = c2 + MOBA_WIDTH
    c4 = c3 + MOBA_WIDTH
    u, q, k, v, qm = proj[..., :c1], proj[..., c1:c2], proj[..., c2:c3], proj[..., c3:c4], proj[..., c4:]
    y_pool, new_buf = pool_mixer(u, pool_buf, pos, pool_w, pool_scale)
    q = rotary(rmsnorm(q.reshape(B, T, MOBA_HEADS, HEAD_DIM), q_norm), pos)
    k = rotary(rmsnorm(k.reshape(B, T, MOBA_HEADS, HEAD_DIM), k_norm), pos)
    v = v.reshape(B, T, MOBA_HEADS, HEAD_DIM)
    k_all = jnp.concatenate([k_past, k], axis=1)
    v_all = jnp.concatenate([v_past, v], axis=1)
    y_moba = moba_attention(q, k_all, v_all, pos).reshape(B, T, MOBA_WIDTH)
    qm = rmsnorm(qm.reshape(B, T, MEM_HEADS, HEAD_DIM), mem_q_norm)
    y_mem = mem_attention(qm, mk, mv).reshape(B, T, MEM_WIDTH)
    x = x + jnp.concatenate([y_pool, y_moba, y_mem], axis=-1) @ w_out
    x = x + 0.5 * swiglu_ffn(x, ffn2_norm, ffn2_w_in, ffn2_w_out)
    return x, k, v, new_buf


def setup_inputs(seed: int = 0) -> dict:
    key = jax.random.key(seed)
    ks = jax.random.split(key, 32)
    n_pages = PAST_LEN // PAGE_SIZE
    n_used = DEC_BATCH * n_pages
    n_phys = n_used + n_used // 4
    f32 = jnp.float32

    def nrm(k, shape, scale=1.0):
        return jax.random.normal(k, shape, f32) * scale

    def gain(k, shape):
        return 1.0 + 0.05 * jax.random.normal(k, shape, f32)

    page_table = jax.random.permutation(ks[8], n_phys)[:n_used].reshape(DEC_BATCH, n_pages).astype(jnp.int32)
    return {
        "x_prompt": nrm(ks[0], (BATCH, SEQ, D_MODEL)),
        "x_sample": nrm(ks[1], (DEC_BATCH, DEC_SEQ, D_MODEL)),
        "mem_prompt": nrm(ks[2], (BATCH, MEM_LEN, D_MODEL)),
        "cache_k": nrm(ks[3], (DEPTH, n_phys, PAGE_SIZE, MOBA_HEADS, HEAD_DIM)),
        "cache_v": nrm(ks[4], (DEPTH, n_phys, PAGE_SIZE, MOBA_HEADS, HEAD_DIM)),
        "cache_mem_k": nrm(ks[5], (DEPTH, DEC_BATCH, MEM_LEN, MEM_HEADS, HEAD_DIM)),
        "cache_mem_v": nrm(ks[6], (DEPTH, DEC_BATCH, MEM_LEN, MEM_HEADS, HEAD_DIM)),
        "state_pool": nrm(ks[7], (DEPTH, DEC_BATCH, POOL_BUF, POOL_WIDTH)),
        "page_table": page_table,
        "ffn1_norm": gain(ks[9], (DEPTH, D_MODEL)),
        "ffn1_w_in": nrm(ks[10], (DEPTH, D_MODEL, 2 * D_FF), D_MODEL ** -0.5),
        "ffn1_w_out": nrm(ks[11], (DEPTH, D_FF, D_MODEL), D_FF ** -0.5),
        "mix_norm": gain(ks[12], (DEPTH, D_MODEL)),
        "w_in": nrm(ks[13], (DEPTH, D_MODEL, IN_COLS), D_MODEL ** -0.5),
        "pool_w": nrm(ks[14], (DEPTH, len(POOL_WINDOWS), POOL_GROUP, POOL_GROUP), POOL_GROUP ** -0.5),
        "pool_scale": 1.0 + 0.1 * jax.random.normal(ks[15], (DEPTH, POOL_WIDTH), f32),
        "q_norm": gain(ks[16], (DEPTH, HEAD_DIM)),
        "k_norm": gain(ks[17], (DEPTH, HEAD_DIM)),
        "mem_norm": gain(ks[18], (DEPTH, D_MODEL)),
        "mem_w_kv": nrm(ks[19], (DEPTH, D_MODEL, 2 * MEM_WIDTH), D_MODEL ** -0.5),
        "mem_q_norm": gain(ks[20], (DEPTH, HEAD_DIM)),
        "mem_k_norm": gain(ks[21], (DEPTH, HEAD_DIM)),
        "w_out": nrm(ks[22], (DEPTH, D_MIX, D_MODEL), D_MIX ** -0.5),
        "ffn2_norm": gain(ks[23], (DEPTH, D_MODEL)),
        "ffn2_w_in": nrm(ks[24], (DEPTH, D_MODEL, 2 * D_FF), D_MODEL ** -0.5),
        "ffn2_w_out": nrm(ks[25], (DEPTH, D_FF, D_MODEL), D_FF ** -0.5),
    }


def reference(x_prompt, x_sample, mem_prompt, cache_k, cache_v, cache_mem_k, cache_mem_v, state_pool, page_table,
              ffn1_norm, ffn1_w_in, ffn1_w_out, mix_norm, w_in, pool_w, pool_scale, q_norm, k_norm,
              mem_norm, mem_w_kv, mem_q_norm, mem_k_norm, w_out, ffn2_norm, ffn2_w_in, ffn2_w_out):
    n_p, t_p, _ = x_prompt.shape
    n_s, t_s, _ = x_sample.shape
    past_len = page_table.shape[1] * PAGE_SIZE
    pos_p = jnp.arange(t_p, dtype=jnp.int32)
    pos_s = past_len + jnp.arange(t_s, dtype=jnp.int32)
    empty_kv = jnp.zeros((n_p, 0, MOBA_HEADS, HEAD_DIM), x_prompt.dtype)
    empty_buf = jnp.zeros((n_p, POOL_BUF, POOL_WIDTH), x_prompt.dtype)
    hp, hs = x_prompt, x_sample
    kp_l, vp_l, mkp_l, mvp_l, bp_l, ks_l, vs_l, bs_l = [], [], [], [], [], [], [], []
    for l in range(DEPTH):
        lw = (ffn1_norm[l], ffn1_w_in[l], ffn1_w_out[l], mix_norm[l], w_in[l], pool_w[l], pool_scale[l],
              q_norm[l], k_norm[l], mem_q_norm[l], w_out[l], ffn2_norm[l], ffn2_w_in[l], ffn2_w_out[l])
        mk_p, mv_p = mem_kv(mem_prompt, mem_norm[l], mem_w_kv[l], mem_k_norm[l])
        hp, kp, vp, bp = trunk_layer(hp, pos_p, empty_kv, empty_kv, empty_buf, mk_p, mv_p, *lw)
        k_past = cache_k[l][page_table].reshape(n_s, past_len, MOBA_HEADS, HEAD_DIM)
        v_past = cache_v[l][page_table].reshape(n_s, past_len, MOBA_HEADS, HEAD_DIM)
        hs, ksn, vsn, bs = trunk_layer(hs, pos_s, k_past, v_past, state_pool[l], cache_mem_k[l], cache_mem_v[l], *lw)
        kp_l.append(kp); vp_l.append(vp); mkp_l.append(mk_p); mvp_l.append(mv_p); bp_l.append(bp)
        ks_l.append(ksn); vs_l.append(vsn); bs_l.append(bs)
    return (hp, hs, jnp.stack(kp_l), jnp.stack(vp_l), jnp.stack(mkp_l), jnp.stack(mvp_l), jnp.stack(bp_l),
            jnp.stack(ks_l), jnp.stack(vs_l), jnp.stack(bs_l))
```

```python
import functools

import jax
import jax.numpy as jnp
from jax import lax
from jax.experimental import pallas as pl
from jax.experimental.pallas import tpu as pltpu

F32 = jnp.float32
BF16 = jnp.bfloat16

D_MODEL = 1024
HEAD_DIM = 64
POOL_WIDTH = 256
POOL_WINDOWS = (2, 4, 8, 16)
POOL_GROUP = POOL_WIDTH // len(POOL_WINDOWS)
POOL_BUF = max(POOL_WINDOWS) - 1
MOBA_WIDTH = 512
MOBA_HEADS = MOBA_WIDTH // HEAD_DIM
MOBA_BLOCK = 256
MOBA_TOPK = 3
MEM_WIDTH = 256
MEM_HEADS = MEM_WIDTH // HEAD_DIM
ROT_DIM = HEAD_DIM // 4
ROPE_THETA = 500000.0
D_FF = 2816
RMS_EPS = 1e-6
PAGE_SIZE = 128
ATTN_SCALE = HEAD_DIM ** -0.5

LANES = 128
V7X_VMEM_BYTES = 64 * 1024 * 1024

HEADS_PER_VREG = LANES // HEAD_DIM
MASKED = -1e30
FF_CHUNK = 256
ROW_TILE = 512


def _dot(a, b):
    return jnp.dot(a, b, preferred_element_type=F32)


def _dot_nt(a, b):
    return lax.dot_general(a, b, (((1,), (1,)), ((), ())), preferred_element_type=F32)


def _dot_tn(a, b):
    return lax.dot_general(a, b, (((0,), (0,)), ((), ())), preferred_element_type=F32)


def _split_bf16(x):
    hi = x.astype(BF16)
    lo = (x - hi.astype(F32)).astype(BF16)
    return hi, lo


def _vmem_limit(resident_bytes):
    return int(min(resident_bytes + 16 * 1024 * 1024, V7X_VMEM_BYTES - 4 * 1024 * 1024))


def _rmsnorm(x, g):
    ms = jnp.mean(x * x, axis=-1, keepdims=True)
    return x * lax.rsqrt(ms + RMS_EPS) * g


def _head_rmsnorm(x, g, ones_bd):
    hi, lo = _split_bf16(x * x)
    ms = (_dot(hi, ones_bd) + _dot(lo, ones_bd)) * (1.0 / HEAD_DIM)
    return x * lax.rsqrt(ms + RMS_EPS) * g


def _rope(x, cos, sin):
    width = x.shape[1]
    half = ROT_DIM // 2
    lane = lax.broadcasted_iota(jnp.int32, x.shape, 1)
    first_half = (lane & (HEAD_DIM - 1)) < half
    partner = jnp.where(first_half, pltpu.roll(x, width - half, 1), pltpu.roll(x, half, 1))
    return x * cos + partner * sin


def _swiglu(h_s, wgu_ref, wo_ref, act_s):
    for c in range(D_FF // FF_CHUNK):
        lo = c * FF_CHUNK
        gate = _dot(h_s[...], wgu_ref[:, lo:lo + FF_CHUNK])
        up = _dot(h_s[...], wgu_ref[:, D_FF + lo:D_FF + lo + FF_CHUNK])
        act_s[:, lo:lo + FF_CHUNK] = (jax.nn.silu(gate) * up).astype(BF16)
    return _dot(act_s[...], wo_ref[...])


def _ffn1_inproj_kernel(x_ref, n1_ref, wgu_ref, wo_ref, nmix_ref, win_ref, qg_ref, kg_ref, mg_ref,
                        ones_ref, cos_ref, sin_ref,
                        x1_ref, u_ref, q_ref, k_ref, v_ref, qm_ref, h_s, act_s):
    x = x_ref[...]
    h_s[...] = _rmsnorm(x, n1_ref[...]).astype(BF16)
    x1 = x + 0.5 * _swiglu(h_s, wgu_ref, wo_ref, act_s)
    x1_ref[...] = x1
    h_s[...] = _rmsnorm(x1, nmix_ref[...]).astype(BF16)
    c1 = POOL_WIDTH
    c2 = c1 + MOBA_WIDTH
    c3 = c2 + MOBA_WIDTH
    c4 = c3 + MOBA_WIDTH
    u_ref[...] = _dot(h_s[...], win_ref[:, :c1])
    q = _dot(h_s[...], win_ref[:, c1:c2])
    k = _dot(h_s[...], win_ref[:, c2:c3])
    v_ref[...] = _dot(h_s[...], win_ref[:, c3:c4])
    qm = _dot(h_s[...], win_ref[:, c4:])
    reps = MOBA_WIDTH // LANES
    cos = jnp.concatenate([cos_ref[...]] * reps, axis=1)
    sin = jnp.concatenate([sin_ref[...]] * reps, axis=1)
    ones = ones_ref[...]
    q_ref[...] = _rope(_head_rmsnorm(q, qg_ref[...], ones), cos, sin) * ATTN_SCALE
    k_ref[...] = _rope(_head_rmsnorm(k, kg_ref[...], ones), cos, sin)
    qm_ref[...] = _head_rmsnorm(qm, mg_ref[...], ones_ref[:MEM_WIDTH, :MEM_WIDTH]) * ATTN_SCALE


def _const_spec(shape):
    return pl.BlockSpec(shape, lambda *_: (0,) * len(shape), pipeline_mode=pl.Buffered(1))


def _ffn1_inproj(x, tm, table_period, n1, wgu, wo, nmix, win, qg, kg, mg, ones_bd, cos_tab, sin_tab):
    n = x.shape[0]
    in_cols = win.shape[1]

    def rows(width):
        return pl.BlockSpec((tm, width), lambda i: (i, 0))

    tab = pl.BlockSpec((tm, LANES), lambda i: (i % table_period, 0))
    widths = (D_MODEL, POOL_WIDTH, MOBA_WIDTH, MOBA_WIDTH, MOBA_WIDTH, MEM_WIDTH)
    resident = 2 * (wgu.size + wo.size + win.size + ones_bd.size) + tm * (D_MODEL + D_FF) * 2
    streamed = 2 * 4 * tm * (D_MODEL + sum(widths) + 2 * LANES)
    return pl.pallas_call(
        _ffn1_inproj_kernel,
        grid=(n // tm,),
        in_specs=[rows(D_MODEL), _const_spec((1, D_MODEL)), _const_spec(wgu.shape), _const_spec(wo.shape),
                  _const_spec((1, D_MODEL)), _const_spec((D_MODEL, in_cols)),
                  _const_spec((1, MOBA_WIDTH)), _const_spec((1, MOBA_WIDTH)), _const_spec((1, MEM_WIDTH)),
                  _const_spec(ones_bd.shape), tab, tab],
        out_specs=[rows(w) for w in widths],
        out_shape=[jax.ShapeDtypeStruct((n, w), F32) for w in widths],
        scratch_shapes=[pltpu.VMEM((tm, D_MODEL), BF16), pltpu.VMEM((tm, D_FF), BF16)],
        compiler_params=pltpu.CompilerParams(dimension_semantics=("parallel",),
                                             vmem_limit_bytes=_vmem_limit(resident + streamed)),
        name="ffn1_inproj",
    )(x, n1, wgu, wo, nmix, win, qg, kg, mg, ones_bd, cos_tab, sin_tab)


def _outproj_ffn2_kernel(x1_ref, yp_ref, ya_ref, ym_ref, wout_ref, n2_ref, wgu_ref, wo_ref, y_ref, h_s, act_s):
    c1 = POOL_WIDTH
    c2 = c1 + MOBA_WIDTH
    mix = (_dot(yp_ref[...], wout_ref[:c1, :]) + _dot(ya_ref[...], wout_ref[c1:c2, :])
           + _dot(ym_ref[...], wout_ref[c2:, :]))
    x2 = x1_ref[...] + mix
    h_s[...] = _rmsnorm(x2, n2_ref[...]).astype(BF16)
    y_ref[...] = x2 + 0.5 * _swiglu(h_s, wgu_ref, wo_ref, act_s)


def _outproj_ffn2(x1, y_pool, y_moba, y_mem, tm, wout, n2, wgu, wo):
    n = x1.shape[0]

    def rows(width):
        return pl.BlockSpec((tm, width), lambda i: (i, 0))

    resident = 2 * (wgu.size + wo.size + wout.size) + tm * (D_MODEL + D_FF) * 2
    streamed = 2 * tm * (4 * 2 * D_MODEL + 2 * D_MODEL)
    return pl.pallas_call(
        _outproj_ffn2_kernel,
        grid=(n // tm,),
        in_specs=[rows(D_MODEL), rows(POOL_WIDTH), rows(MOBA_WIDTH), rows(MEM_WIDTH),
                  _const_spec(wout.shape), _const_spec((1, D_MODEL)), _const_spec(wgu.shape), _const_spec(wo.shape)],
        out_specs=rows(D_MODEL),
        out_shape=jax.ShapeDtypeStruct((n, D_MODEL), F32),
        scratch_shapes=[pltpu.VMEM((tm, D_MODEL), BF16), pltpu.VMEM((tm, D_FF), BF16)],
        compiler_params=pltpu.CompilerParams(dimension_semantics=("parallel",),
                                             vmem_limit_bytes=_vmem_limit(resident + streamed)),
        name="outproj_ffn2",
    )(x1, y_pool, y_moba, y_mem, wout, n2, wgu, wo)


def _moba_prompt_kernel(q_ref, k_ref, v_ref, o_ref, kb_s, v0_s, v1_s, kmean_s, bias_s, m_s, l_s, acc_s):
    i = pl.program_id(2)
    seq = k_ref.shape[0]
    n_blocks = seq // MOBA_BLOCK
    n_cols = kmean_s.shape[0]
    v_s = (v0_s, v1_s)

    @pl.when(i == 0)
    def _():
        lane_t = lax.broadcasted_iota(jnp.int32, (seq, LANES), 1)
        v = v_ref[...]
        kb_s[...] = k_ref[...].astype(BF16)
        v0_s[...] = jnp.where(lane_t < HEAD_DIM, v, 0.0).astype(BF16)
        v1_s[...] = jnp.where(lane_t < HEAD_DIM, 0.0, v).astype(BF16)
        kmean_s[...] = jnp.zeros_like(kmean_s)
        for n in range(n_blocks):
            blk = k_ref[n * MOBA_BLOCK:(n + 1) * MOBA_BLOCK, :]
            kmean_s[n:n + 1, :] = jnp.mean(blk, axis=0, keepdims=True)

    q = q_ref[...]
    lane = lax.broadcasted_iota(jnp.int32, q.shape, 1)
    q_f = (jnp.where(lane < HEAD_DIM, q, 0.0), jnp.where(lane < HEAD_DIM, 0.0, q))
    q_b = tuple(a.astype(BF16) for a in q_f)

    km_hi, km_lo = _split_bf16(kmean_s[...])
    col = lax.broadcasted_iota(jnp.int32, (MOBA_BLOCK, n_cols), 1)
    valid = col < i
    for h in range(HEADS_PER_VREG):
        q_lo = (q_f[h] - q_b[h].astype(F32)).astype(BF16)
        s = _dot_nt(q_b[h], km_hi) + _dot_nt(q_b[h], km_lo) + _dot_nt(q_lo, km_hi)
        s = jnp.where(valid, s, -jnp.inf)
        rank = jnp.zeros(s.shape, jnp.int32)
        for m in range(n_blocks):
            sm = s[:, m:m + 1]
            beats = (sm > s) | ((sm == s) & (col > m))
            rank = rank + beats.astype(jnp.int32)
        bias = jnp.where(valid & (rank < MOBA_TOPK), 0.0, MASKED)
        for n in range(n_blocks):
            bias_s[h, n] = bias[:, n:n + 1]

    own = pl.multiple_of(i * MOBA_BLOCK, MOBA_BLOCK)
    k_own = kb_s[pl.ds(own, MOBA_BLOCK), :]
    r_id = lax.broadcasted_iota(jnp.int32, (MOBA_BLOCK, MOBA_BLOCK), 0)
    c_id = lax.broadcasted_iota(jnp.int32, (MOBA_BLOCK, MOBA_BLOCK), 1)
    for h in range(HEADS_PER_VREG):
        lg = jnp.where(c_id <= r_id, _dot_nt(q_b[h], k_own), MASKED)
        m = jnp.max(lg, axis=-1, keepdims=True)
        p = jnp.exp(lg - m)
        m_s[h] = m
        l_s[h] = jnp.sum(p, axis=-1, keepdims=True)
        acc_s[h] = _dot(p.astype(BF16), v_s[h][pl.ds(own, MOBA_BLOCK), :])

    def past_block(j, carry):
        start = pl.multiple_of(j * MOBA_BLOCK, MOBA_BLOCK)
        k_j = kb_s[pl.ds(start, MOBA_BLOCK), :]
        for h in range(HEADS_PER_VREG):
            lg = _dot_nt(q_b[h], k_j) + bias_s[h, j]
            m_old = m_s[h]
            m_new = jnp.maximum(m_old, jnp.max(lg, axis=-1, keepdims=True))
            alpha = jnp.exp(m_old - m_new)
            p = jnp.exp(lg - m_new)
            l_s[h] = alpha * l_s[h] + jnp.sum(p, axis=-1, keepdims=True)
            m_s[h] = m_new
            acc_s[h] = alpha * acc_s[h] + _dot(p.astype(BF16), v_s[h][pl.ds(start, MOBA_BLOCK), :])
        return carry

    lax.fori_loop(0, i, past_block, 0)
    o_ref[...] = (acc_s[0] / l_s[0] + acc_s[1] / l_s[1]).astype(o_ref.dtype)


def _moba_prompt(q, k, v, batch, seq):
    n_blocks = seq // MOBA_BLOCK
    pairs = MOBA_WIDTH // LANES
    score_cols = 16
    assert n_blocks <= score_cols
    q_spec = pl.BlockSpec((MOBA_BLOCK, LANES), lambda b, p, i: (b * n_blocks + i, p))
    kv_spec = pl.BlockSpec((seq, LANES), lambda b, p, i: (b, p))
    return pl.pallas_call(
        _moba_prompt_kernel,
        grid=(batch, pairs, n_blocks),
        in_specs=[q_spec, kv_spec, kv_spec],
        out_specs=q_spec,
        out_shape=jax.ShapeDtypeStruct((batch * seq, MOBA_WIDTH), BF16),
        scratch_shapes=[pltpu.VMEM((seq, LANES), BF16), pltpu.VMEM((seq, LANES), BF16), pltpu.VMEM((seq, LANES), BF16),
                        pltpu.VMEM((score_cols, LANES), F32),
                        pltpu.VMEM((HEADS_PER_VREG, n_blocks, MOBA_BLOCK, 1), F32),
                        pltpu.VMEM((HEADS_PER_VREG, MOBA_BLOCK, 1), F32),
                        pltpu.VMEM((HEADS_PER_VREG, MOBA_BLOCK, 1), F32),
                        pltpu.VMEM((HEADS_PER_VREG, MOBA_BLOCK, LANES), F32)],
        compiler_params=pltpu.CompilerParams(dimension_semantics=("parallel", "parallel", "arbitrary")),
        name="moba_prompt",
    )(q, k, v)


def _moba_sample_kernel(pt_ref, q_ref, kn_ref, vn_ref, ck_ref, cv_ref, o_ref, kbuf, vbuf, lg_s, sem):
    b = pl.program_id(0)
    n_pages = kbuf.shape[0]
    pages_per_block = MOBA_BLOCK // PAGE_SIZE
    n_blocks = n_pages // pages_per_block
    rows = lg_s.shape[1]

    def page_copy(cache_ref, buf, j, s):
        return pltpu.make_async_copy(cache_ref.at[pt_ref[b, j]], buf.at[j], sem.at[s])

    for j in range(n_pages):
        page_copy(ck_ref, kbuf, j, 0).start()
    for j in range(n_pages):
        page_copy(cv_ref, vbuf, j, 1).start()

    q = q_ref[0]
    row = lax.broadcasted_iota(jnp.int32, (rows, MOBA_WIDTH), 0)
    own_head = lax.broadcasted_iota(jnp.int32, (rows, MOBA_WIDTH), 1) // HEAD_DIM == (row & (MOBA_HEADS - 1))
    q_hi, q_lo = _split_bf16(q)
    q_t = jnp.where(own_head, jnp.where(row < MOBA_HEADS, q_hi.astype(F32), q_lo.astype(F32)), 0.0).astype(BF16)

    for j in range(n_pages):
        page_copy(ck_ref, kbuf, j, 0).wait()

    def score_page(j, carry):
        lg_s[j] = _dot(q_t, kbuf[j].astype(BF16))
        return carry

    lax.fori_loop(0, n_pages, score_page, 0)
    parts = lg_s[...]
    lg = parts[:, :MOBA_HEADS, :] + parts[:, MOBA_HEADS:, :]

    blk = lg.reshape(n_blocks, pages_per_block, MOBA_HEADS, PAGE_SIZE)
    blk_sum = blk[:, 0]
    for r in range(1, pages_per_block):
        blk_sum = blk_sum + blk[:, r]
    score_col = jnp.sum(blk_sum, axis=-1, keepdims=True) * (1.0 / MOBA_BLOCK)
    lane = lax.broadcasted_iota(jnp.int32, (MOBA_HEADS, LANES), 1)
    blk_of_lane = lane & (n_blocks - 1)
    score = jnp.zeros((MOBA_HEADS, LANES), F32)
    for n in range(n_blocks):
        score = jnp.where(blk_of_lane == n, score_col[n], score)
    rank = jnp.zeros(score.shape, jnp.int32)
    for k in range(1, n_blocks):
        other = pltpu.roll(score, k, 1)
        beats = (other > score) | ((other == score) & (blk_of_lane >= k))
        rank = rank + beats.astype(jnp.int32)
    bias = jnp.where(rank < MOBA_TOPK, 0.0, MASKED)

    q_rows = jnp.where(own_head[:MOBA_HEADS], q, 0.0)
    lg_self = jnp.sum(q_rows * kn_ref[0], axis=-1, keepdims=True)

    page_bias = [jnp.broadcast_to(bias[:, n:n + 1], (MOBA_HEADS, PAGE_SIZE)) for n in range(n_blocks)]
    lg = (blk + jnp.stack(page_bias)[:, None]).reshape(n_pages, MOBA_HEADS, PAGE_SIZE)
    m = jnp.maximum(jnp.max(jnp.max(lg, axis=0), axis=-1, keepdims=True), lg_self)
    p = jnp.exp(lg - m)
    p_self = jnp.exp(lg_self - m)
    denom = jnp.sum(jnp.sum(p, axis=0), axis=-1, keepdims=True) + p_self
    lg_s[:, :MOBA_HEADS, :] = p / denom
    lg_s[:, MOBA_HEADS:, :] = jnp.zeros((n_pages, rows - MOBA_HEADS, PAGE_SIZE), F32)

    for j in range(n_pages):
        page_copy(cv_ref, vbuf, j, 1).wait()

    def value_page(j, acc):
        return acc + _dot_nt(lg_s[j].astype(BF16), vbuf[j].astype(BF16))

    acc = lax.fori_loop(0, n_pages, value_page, jnp.zeros((rows, MOBA_WIDTH), F32))
    acc = acc[:MOBA_HEADS] + (p_self / denom) * vn_ref[0]
    o_ref[0] = jnp.sum(jnp.where(own_head[:MOBA_HEADS], acc, 0.0), axis=0, keepdims=True).astype(o_ref.dtype)


def _moba_sample(page_table, q, k_new, v_new, cache_k_t, cache_v_t):
    n_rows, n_pages = page_table.shape
    assert (n_pages * PAGE_SIZE) % MOBA_BLOCK == 0
    n_blocks = n_pages * PAGE_SIZE // MOBA_BLOCK
    assert LANES % n_blocks == 0 and n_blocks & (n_blocks - 1) == 0
    row_spec = pl.BlockSpec((1, 1, MOBA_WIDTH), lambda b, pt: (b, 0, 0))
    any_spec = pl.BlockSpec(memory_space=pl.ANY)
    page_buf = pltpu.VMEM((n_pages, MOBA_WIDTH, PAGE_SIZE), F32)
    resident = 2 * n_pages * MOBA_WIDTH * PAGE_SIZE * 4
    return pl.pallas_call(
        _moba_sample_kernel,
        grid_spec=pltpu.PrefetchScalarGridSpec(
            num_scalar_prefetch=1,
            grid=(n_rows,),
            in_specs=[row_spec, row_spec, row_spec, any_spec, any_spec],
            out_specs=row_spec,
            scratch_shapes=[page_buf, page_buf, pltpu.VMEM((n_pages, 2 * MOBA_HEADS, PAGE_SIZE), F32),
                            pltpu.SemaphoreType.DMA((2,))]),
        out_shape=jax.ShapeDtypeStruct((n_rows, 1, MOBA_WIDTH), BF16),
        compiler_params=pltpu.CompilerParams(dimension_semantics=("arbitrary",),
                                             vmem_limit_bytes=_vmem_limit(resident)),
        name="moba_sample",
    )(page_table, q.reshape(n_rows, 1, MOBA_WIDTH), k_new.reshape(n_rows, 1, MOBA_WIDTH),
      v_new.reshape(n_rows, 1, MOBA_WIDTH), cache_k_t, cache_v_t)


def _mem_kv_kernel(mem_ref, g_ref, w_ref, kg_ref, ones_ref, mk_ref, mv_ref):
    h = _rmsnorm(mem_ref[...], g_ref[...]).astype(BF16)
    kv = _dot(h, w_ref[...])
    mk_ref[...] = _head_rmsnorm(kv[:, :MEM_WIDTH], kg_ref[...], ones_ref[...])
    mv_ref[...] = kv[:, MEM_WIDTH:]


def _mem_kv(mem, tm, g, w, kg, ones_bd):
    n = mem.shape[0]
    rows = pl.BlockSpec((tm, MEM_WIDTH), lambda i: (i, 0))
    return pl.pallas_call(
        _mem_kv_kernel,
        grid=(n // tm,),
        in_specs=[pl.BlockSpec((tm, D_MODEL), lambda i: (i, 0)), _const_spec((1, D_MODEL)), _const_spec(w.shape),
                  _const_spec((1, MEM_WIDTH)), _const_spec(ones_bd.shape)],
        out_specs=[rows, rows],
        out_shape=[jax.ShapeDtypeStruct((n, MEM_WIDTH), F32)] * 2,
        compiler_params=pltpu.CompilerParams(dimension_semantics=("parallel",)),
        name="mem_kv",
    )(mem, g, w, kg, ones_bd)


def _pool_select(lane, a2, a4, a8, a16):
    return jnp.where(lane < POOL_GROUP, a2,
                     jnp.where(lane < 2 * POOL_GROUP, a4, jnp.where(lane < 3 * POOL_GROUP, a8, a16)))


def _mix_prompt_kernel(u_ref, halo_ref, qm_ref, mk_ref, mv_ref, wbd_ref, scale_ref, yp_ref, ym_ref):
    t = pl.program_id(1)
    tm = u_ref.shape[0]
    halo_rows = halo_ref.shape[0]
    u = u_ref[...]
    ext = jnp.concatenate([jnp.where(t > 0, halo_ref[...], 0.0), u], axis=0)
    a2 = ext + pltpu.roll(ext, 1, 0)
    a4 = a2 + pltpu.roll(a2, 2, 0)
    a8 = a4 + pltpu.roll(a4, 4, 0)
    a16 = a8 + pltpu.roll(a8, 8, 0)
    lane = lax.broadcasted_iota(jnp.int32, u.shape, 1)
    sums = _pool_select(lane, a2[halo_rows:], a4[halo_rows:], a8[halo_rows:], a16[halo_rows:])
    pos = t * tm + lax.broadcasted_iota(jnp.int32, u.shape, 0)
    window = _pool_select(lane, *POOL_WINDOWS)
    cnt = jnp.minimum(window, pos + 1).astype(F32)
    pooled = sums / cnt - u
    yp_ref[...] = (_dot(pooled.astype(BF16), wbd_ref[...]) * scale_ref[...]).astype(yp_ref.dtype)

    lane_p = lax.broadcasted_iota(jnp.int32, (mk_ref.shape[0], LANES), 1)
    lane_q = lax.broadcasted_iota(jnp.int32, (tm, LANES), 1)
    for pair in range(MEM_WIDTH // LANES):
        cols = slice(pair * LANES, (pair + 1) * LANES)
        qp = qm_ref[:, cols]
        mk = mk_ref[:, cols].astype(BF16)
        mv = mv_ref[:, cols]
        out = None
        for h in range(HEADS_PER_VREG):
            in_head = (lane_q < HEAD_DIM) if h == 0 else (lane_q >= HEAD_DIM)
            in_head_p = (lane_p < HEAD_DIM) if h == 0 else (lane_p >= HEAD_DIM)
            lg = _dot_nt(jnp.where(in_head, qp, 0.0).astype(BF16), mk)
            p = jnp.exp(lg - jnp.max(lg, axis=-1, keepdims=True))
            o = _dot(p.astype(BF16), jnp.where(in_head_p, mv, 0.0).astype(BF16))
            o = o / jnp.sum(p, axis=-1, keepdims=True)
            out = o if out is None else out + o
        ym_ref[:, cols] = out.astype(ym_ref.dtype)


def _mix_prompt(u, qm, mk, mv, wbd, scale, batch, seq, tm):
    halo_rows = 16
    assert halo_rows > POOL_BUF and tm % halo_rows == 0 and seq % tm == 0
    tiles = seq // tm
    mem_len = mk.shape[0] // batch

    def rows(width):
        return pl.BlockSpec((tm, width), lambda b, t: (b * tiles + t, 0))

    halo = pl.BlockSpec((halo_rows, POOL_WIDTH),
                        lambda b, t: (jnp.maximum((b * tiles + t) * (tm // halo_rows) - 1, 0), 0))
    mem = pl.BlockSpec((mem_len, MEM_WIDTH), lambda b, t: (b, 0))
    return pl.pallas_call(
        _mix_prompt_kernel,
        grid=(batch, tiles),
        in_specs=[rows(POOL_WIDTH), halo, rows(MEM_WIDTH), mem, mem, _const_spec(wbd.shape),
                  _const_spec((1, POOL_WIDTH))],
        out_specs=[rows(POOL_WIDTH), rows(MEM_WIDTH)],
        out_shape=[jax.ShapeDtypeStruct((batch * seq, POOL_WIDTH), BF16),
                   jax.ShapeDtypeStruct((batch * seq, MEM_WIDTH), BF16)],
        compiler_params=pltpu.CompilerParams(dimension_semantics=("parallel", "parallel")),
        name="mix_prompt",
    )(u, u, qm, mk, mv, wbd, scale)


def _mix_sample_kernel(u_ref, st_ref, qm_ref, mk_ref, mv_ref, wbd_ref, scale_ref, yp_ref, ym_ref, *, pos):
    rows = u_ref.shape[0]
    u = u_ref[...]
    tail = u
    sums = {}
    for back in range(1, POOL_BUF + 1):
        tail = tail + st_ref[POOL_BUF - back]
        if back + 1 in POOL_WINDOWS:
            sums[back + 1] = tail / float(min(back + 1, pos + 1))
    lane = lax.broadcasted_iota(jnp.int32, u.shape, 1)
    pooled = _pool_select(lane, *(sums[w] for w in POOL_WINDOWS)) - u
    yp_ref[...] = (_dot(pooled.astype(BF16), wbd_ref[...]) * scale_ref[...]).astype(yp_ref.dtype)

    row_t = lax.broadcasted_iota(jnp.int32, (16, MEM_WIDTH), 0)
    head_t = lax.broadcasted_iota(jnp.int32, (16, MEM_WIDTH), 1) // HEAD_DIM
    row_o = lax.broadcasted_iota(jnp.int32, (MEM_HEADS, MEM_WIDTH), 0)
    head_o = lax.broadcasted_iota(jnp.int32, (MEM_HEADS, MEM_WIDTH), 1) // HEAD_DIM
    for r in range(rows):
        q = jnp.broadcast_to(qm_ref[r:r + 1, :], (16, MEM_WIDTH))
        q_t = jnp.where(head_t == row_t, q, 0.0).astype(BF16)
        lg = _dot_nt(mk_ref[r].astype(BF16), q_t)
        p = jnp.exp(lg - jnp.max(lg, axis=0, keepdims=True))
        p = p / jnp.sum(p, axis=0, keepdims=True)
        o = _dot_tn(p.astype(BF16), mv_ref[r].astype(BF16))
        o = jnp.sum(jnp.where(head_o == row_o, o[0:MEM_HEADS, :], 0.0), axis=0, keepdims=True)
        ym_ref[r:r + 1, :] = o.astype(ym_ref.dtype)


def _mix_sample(u, state_t, qm, mk, mv, wbd, scale, rows_per_step, pos):
    n = u.shape[0]
    mem_len = mk.shape[1]

    def rows(width):
        return pl.BlockSpec((rows_per_step, width), lambda i: (i, 0))

    mem = pl.BlockSpec((rows_per_step, mem_len, MEM_WIDTH), lambda i: (i, 0, 0))
    return pl.pallas_call(
        functools.partial(_mix_sample_kernel, pos=pos),
        grid=(n // rows_per_step,),
        in_specs=[rows(POOL_WIDTH), pl.BlockSpec((POOL_BUF, rows_per_step, POOL_WIDTH), lambda i: (0, i, 0)),
                  rows(MEM_WIDTH), mem, mem, _const_spec(wbd.shape), _const_spec((1, POOL_WIDTH))],
        out_specs=[rows(POOL_WIDTH), rows(MEM_WIDTH)],
        out_shape=[jax.ShapeDtypeStruct((n, POOL_WIDTH), BF16), jax.ShapeDtypeStruct((n, MEM_WIDTH), BF16)],
        compiler_params=pltpu.CompilerParams(dimension_semantics=("parallel",)),
        name="mix_sample",
    )(u, state_t, qm, mk, mv, wbd, scale)


def _rope_tables(pos):
    half = ROT_DIM // 2
    inv = jnp.power(jnp.float32(ROPE_THETA), -jnp.arange(half, dtype=F32) * (2.0 / ROT_DIM))
    ang = pos.astype(F32)[:, None] * inv[None, :]
    cos, sin = jnp.cos(ang), jnp.sin(ang)
    rest = HEAD_DIM - ROT_DIM
    ones = jnp.ones((pos.shape[0], rest), F32)
    zeros = jnp.zeros((pos.shape[0], rest), F32)
    cos_h = jnp.concatenate([cos, cos, ones], axis=1)
    sin_h = jnp.concatenate([-sin, sin, zeros], axis=1)
    return jnp.tile(cos_h, (1, HEADS_PER_VREG)), jnp.tile(sin_h, (1, HEADS_PER_VREG))


def _block_diag_ones(width):
    head = jnp.arange(width) // HEAD_DIM
    return (head[:, None] == head[None, :]).astype(BF16)


def _tile_gain(g, width):
    return jnp.tile(g, width // HEAD_DIM).reshape(1, width)


def kernel(x_prompt, x_sample, mem_prompt, cache_k, cache_v, cache_mem_k, cache_mem_v, state_pool, page_table,
           ffn1_norm, ffn1_w_in, ffn1_w_out, mix_norm, w_in, pool_w, pool_scale, q_norm, k_norm,
           mem_norm, mem_w_kv, mem_q_norm, mem_k_norm, w_out, ffn2_norm, ffn2_w_in, ffn2_w_out):
    depth = ffn1_norm.shape[0]
    assert depth == 1
    n_p, t_p, _ = x_prompt.shape
    n_s, t_s, _ = x_sample.shape
    assert t_s == 1
    past_len = page_table.shape[1] * PAGE_SIZE
    mem_len = mem_prompt.shape[1]
    l = 0

    wgu1, wo1 = ffn1_w_in[l].astype(BF16), ffn1_w_out[l].astype(BF16)
    wgu2, wo2 = ffn2_w_in[l].astype(BF16), ffn2_w_out[l].astype(BF16)
    win, wout, wkv = w_in[l].astype(BF16), w_out[l].astype(BF16), mem_w_kv[l].astype(BF16)
    n1, nmix, n2, nmem = (a[l].reshape(1, D_MODEL) for a in (ffn1_norm, mix_norm, ffn2_norm, mem_norm))
    qg, kg = _tile_gain(q_norm[l], MOBA_WIDTH), _tile_gain(k_norm[l], MOBA_WIDTH)
    mqg, mkg = _tile_gain(mem_q_norm[l], MEM_WIDTH), _tile_gain(mem_k_norm[l], MEM_WIDTH)
    ones_bd = _block_diag_ones(MOBA_WIDTH)
    wbd = jax.scipy.linalg.block_diag(*[pool_w[l, g] for g in range(len(POOL_WINDOWS))]).astype(BF16)
    scale = pool_scale[l].reshape(1, POOL_WIDTH)

    xp = x_prompt.reshape(n_p * t_p, D_MODEL)
    cos_p, sin_p = _rope_tables(jnp.arange(t_p, dtype=jnp.int32))
    x1p, up, qp, kp, vp, qmp = _ffn1_inproj(xp, ROW_TILE, t_p // ROW_TILE, n1, wgu1, wo1, nmix, win, qg, kg, mqg,
                                            ones_bd, cos_p, sin_p)
    mkp, mvp = _mem_kv(mem_prompt.reshape(n_p * mem_len, D_MODEL), ROW_TILE, nmem, wkv, mkg,
                       ones_bd[:MEM_WIDTH, :MEM_WIDTH])
    ya_p = _moba_prompt(qp, kp, vp, n_p, t_p)
    yp_p, ym_p = _mix_prompt(up, qmp, mkp, mvp, wbd, scale, n_p, t_p, ROW_TILE)
    y_prompt = _outproj_ffn2(x1p, yp_p, ya_p, ym_p, ROW_TILE, wout, n2, wgu2, wo2)

    xs = x_sample.reshape(n_s, D_MODEL)
    cos_s, sin_s = _rope_tables(jnp.full((n_s,), past_len, jnp.int32))
    x1s, us, qs, ks, vs, qms = _ffn1_inproj(xs, n_s, 1, n1, wgu1, wo1, nmix, win, qg, kg, mqg, ones_bd, cos_s, sin_s)
    n_phys = cache_k.shape[1]

    def pages_feature_major(cache):
        return jnp.transpose(cache[l], (0, 2, 3, 1)).reshape(n_phys, MOBA_WIDTH, PAGE_SIZE)

    ya_s = _moba_sample(page_table, qs, ks, vs, pages_feature_major(cache_k),
                        pages_feature_major(cache_v)).reshape(n_s, MOBA_WIDTH)
    state = state_pool[l]
    yp_s, ym_s = _mix_sample(us, jnp.swapaxes(state, 0, 1), qms,
                             cache_mem_k[l].reshape(n_s, mem_len, MEM_WIDTH),
                             cache_mem_v[l].reshape(n_s, mem_len, MEM_WIDTH), wbd, scale, 8, past_len)
    y_sample = _outproj_ffn2(x1s, yp_s, ya_s, ym_s, n_s, wout, n2, wgu2, wo2)

    def heads(a, rows, t, n_heads):
        return a.reshape(1, rows, t, n_heads, HEAD_DIM)

    pool_state_p = up.reshape(n_p, t_p, POOL_WIDTH)[:, t_p - POOL_BUF:][None]
    pool_state_s = jnp.concatenate([state[:, 1:], us[:, None, :]], axis=1)[None]
    return (y_prompt.reshape(n_p, t_p, D_MODEL), y_sample.reshape(n_s, t_s, D_MODEL),
            heads(kp, n_p, t_p, MOBA_HEADS), heads(vp, n_p, t_p, MOBA_HEADS),
            heads(mkp, n_p, mem_len, MEM_HEADS), heads(mvp, n_p, mem_len, MEM_HEADS),
            pool_state_p,
            heads(ks, n_s, t_s, MOBA_HEADS), heads(vs, n_s, t_s, MOBA_HEADS), pool_state_s)
```

```python
import functools

import jax
import jax.numpy as jnp
from jax import lax
from jax.experimental import pallas as pl
from jax.experimental.pallas import tpu as pltpu

F32 = jnp.float32
BF16 = jnp.bfloat16

D_MODEL = 1024
HEAD_DIM = 64
POOL_WIDTH = 256
POOL_WINDOWS = (2, 4, 8, 16)
POOL_GROUP = POOL_WIDTH // len(POOL_WINDOWS)
POOL_BUF = max(POOL_WINDOWS) - 1
MOBA_WIDTH = 512
MOBA_HEADS = MOBA_WIDTH // HEAD_DIM
MOBA_BLOCK = 256
MOBA_TOPK = 3
MEM_WIDTH = 256
MEM_HEADS = MEM_WIDTH // HEAD_DIM
ROT_DIM = HEAD_DIM // 4
ROT_HALF = ROT_DIM // 2
ROPE_THETA = 500000.0
D_FF = 2816
RMS_EPS = 1e-6
PAGE_SIZE = 128
ATTN_SCALE = HEAD_DIM ** -0.5
LOG2_E = 1.4426950408889634

LANES = 128
SUBLANES = 8
V7X_VMEM_BYTES = 64 * 1024 * 1024

HEADS_PER_VREG = LANES // HEAD_DIM
MASKED = -1e30
FF_CHUNK = 256
ROW_TILE = 512
SCORE_ROWS = 16


def _dot(a, b):
    return jnp.dot(a, b, preferred_element_type=F32)


def _dot_nt(a, b):
    return lax.dot_general(a, b, (((1,), (1,)), ((), ())), preferred_element_type=F32)


def _split_bf16(x):
    hi = x.astype(BF16)
    lo = (x - hi.astype(F32)).astype(BF16)
    return hi, lo


def _vmem_limit(resident_bytes):
    return int(min(resident_bytes + 16 * 1024 * 1024, V7X_VMEM_BYTES - 4 * 1024 * 1024))


def _rmsnorm(x, g):
    ms = jnp.mean(x * x, axis=-1, keepdims=True)
    return x * lax.rsqrt(ms + RMS_EPS) * g


def _head_rmsnorm(x, g, ones_bd):
    hi, lo = _split_bf16(x * x)
    ms = (_dot(hi, ones_bd) + _dot(lo, ones_bd)) * (1.0 / HEAD_DIM)
    return x * lax.rsqrt(ms + RMS_EPS) * g


def _head_rmsnorm_t(x, g):
    heads = []
    for r in range(0, x.shape[0], HEAD_DIM):
        xh = x[r:r + HEAD_DIM]
        ms = jnp.mean(xh * xh, axis=0, keepdims=True)
        heads.append(xh * lax.rsqrt(ms + RMS_EPS))
    return jnp.concatenate(heads, axis=0) * g


def _rope(x, cos, sin):
    width = x.shape[1]
    lane = lax.broadcasted_iota(jnp.int32, x.shape, 1)
    first_half = (lane & (HEAD_DIM - 1)) < ROT_HALF
    partner = jnp.where(first_half, pltpu.roll(x, width - ROT_HALF, 1), pltpu.roll(x, ROT_HALF, 1))
    return x * cos + partner * sin


def _rope_t(x, cos, sin):
    pieces = []
    for r in range(0, x.shape[0], HEAD_DIM):
        x1, x2 = x[r:r + ROT_HALF], x[r + ROT_HALF:r + ROT_DIM]
        pieces += [x1 * cos - x2 * sin, x2 * cos + x1 * sin, x[r + ROT_DIM:r + HEAD_DIM]]
    return jnp.concatenate(pieces, axis=0)


def _lane_tile(table, n):
    return jnp.concatenate([table] * (n // LANES), axis=1)


def _swiglu(h_s, wgu_ref, wo_ref, act_s):
    for c in range(D_FF // FF_CHUNK):
        lo = c * FF_CHUNK
        gate = _dot(h_s[...], wgu_ref[:, lo:lo + FF_CHUNK])
        up = _dot(h_s[...], wgu_ref[:, D_FF + lo:D_FF + lo + FF_CHUNK])
        act_s[:, lo:lo + FF_CHUNK] = (jax.nn.silu(gate) * up).astype(BF16)
    return _dot(act_s[...], wo_ref[...])


def _const_spec(shape):
    return pl.BlockSpec(shape, lambda *_: (0,) * len(shape), pipeline_mode=pl.Buffered(1))


def _ffn1_trunk(x_ref, n1_ref, wgu_ref, wo_ref, nmix_ref, x1_ref, h_s, act_s):
    x = x_ref[...]
    h_s[...] = _rmsnorm(x, n1_ref[...]).astype(BF16)
    x1 = x + 0.5 * _swiglu(h_s, wgu_ref, wo_ref, act_s)
    x1_ref[...] = x1
    h_s[...] = _rmsnorm(x1, nmix_ref[...]).astype(BF16)


def _ffn1_inproj_kernel(x_ref, n1_ref, wgu_ref, wo_ref, nmix_ref, win_ref, qg_ref, kg_ref, mg_ref,
                        ones_ref, cos_ref, sin_ref,
                        x1_ref, u_ref, q_ref, k_ref, v_ref, qm_ref, h_s, act_s):
    _ffn1_trunk(x_ref, n1_ref, wgu_ref, wo_ref, nmix_ref, x1_ref, h_s, act_s)
    c1 = POOL_WIDTH
    c2 = c1 + MOBA_WIDTH
    c3 = c2 + MOBA_WIDTH
    c4 = c3 + MOBA_WIDTH
    u_ref[...] = _dot(h_s[...], win_ref[:, :c1])
    q = _dot(h_s[...], win_ref[:, c1:c2])
    k = _dot(h_s[...], win_ref[:, c2:c3])
    v_ref[...] = _dot(h_s[...], win_ref[:, c3:c4])
    qm = _dot(h_s[...], win_ref[:, c4:])
    reps = MOBA_WIDTH // LANES
    cos = jnp.concatenate([cos_ref[...]] * reps, axis=1)
    sin = jnp.concatenate([sin_ref[...]] * reps, axis=1)
    ones = ones_ref[...]
    q_ref[...] = _rope(_head_rmsnorm(q, qg_ref[...], ones), cos, sin) * ATTN_SCALE
    k_ref[...] = _rope(_head_rmsnorm(k, kg_ref[...], ones), cos, sin)
    qm_ref[...] = _head_rmsnorm(qm, mg_ref[...], ones_ref[:MEM_WIDTH, :MEM_WIDTH]) * ATTN_SCALE


def _ffn1_inproj(x, n1, wgu, wo, nmix, win, qg, kg, mg, ones_bd, cos_tab, sin_tab):
    n = x.shape[0]

    def full(width):
        return pl.BlockSpec((n, width), lambda i: (0, 0))

    widths = (D_MODEL, POOL_WIDTH, MOBA_WIDTH, MOBA_WIDTH, MOBA_WIDTH, MEM_WIDTH)
    resident = 2 * (wgu.size + wo.size + win.size + ones_bd.size)
    return pl.pallas_call(
        _ffn1_inproj_kernel,
        grid=(1,),
        in_specs=[full(D_MODEL), _const_spec((1, D_MODEL)), _const_spec(wgu.shape), _const_spec(wo.shape),
                  _const_spec((1, D_MODEL)), _const_spec(win.shape),
                  _const_spec((1, MOBA_WIDTH)), _const_spec((1, MOBA_WIDTH)), _const_spec((1, MEM_WIDTH)),
                  _const_spec(ones_bd.shape), full(LANES), full(LANES)],
        out_specs=[full(w) for w in widths],
        out_shape=[jax.ShapeDtypeStruct((n, w), F32) for w in widths],
        scratch_shapes=[pltpu.VMEM((n, D_MODEL), BF16), pltpu.VMEM((n, D_FF), BF16)],
        compiler_params=pltpu.CompilerParams(dimension_semantics=("arbitrary",),
                                             vmem_limit_bytes=_vmem_limit(resident)),
        name="ffn1_inproj_sample",
    )(x, n1, wgu, wo, nmix, win, qg, kg, mg, ones_bd, cos_tab, sin_tab)


def _ffn1_inproj_t_kernel(x_ref, n1_ref, wgu_ref, wo_ref, nmix_ref, wnat_ref, wqkv_t_ref, qg_ref, kg_ref, mg_ref,
                          ones_ref, cos_ref, sin_ref,
                          x1_ref, u_ref, qm_ref, qt_ref, kt_ref, vt_ref, h_s, act_s):
    _ffn1_trunk(x_ref, n1_ref, wgu_ref, wo_ref, nmix_ref, x1_ref, h_s, act_s)
    tm = x_ref.shape[0]
    u_ref[...] = _dot(h_s[...], wnat_ref[:, :POOL_WIDTH])
    qm = _dot(h_s[...], wnat_ref[:, POOL_WIDTH:])
    qm_ref[...] = _head_rmsnorm(qm, mg_ref[...], ones_ref[...]) * ATTN_SCALE
    cos, sin = cos_ref[...], sin_ref[...]
    q_t = _dot_nt(wqkv_t_ref[:MOBA_WIDTH, :], h_s[...])
    qt_ref[...] = _rope_t(_head_rmsnorm_t(q_t, _lane_tile(qg_ref[...], tm)), cos, sin) * (ATTN_SCALE * LOG2_E)
    k_t = _dot_nt(wqkv_t_ref[MOBA_WIDTH:2 * MOBA_WIDTH, :], h_s[...])
    kt_ref[...] = _rope_t(_head_rmsnorm_t(k_t, _lane_tile(kg_ref[...], tm)), cos, sin)
    vt_ref[...] = _dot_nt(wqkv_t_ref[2 * MOBA_WIDTH:, :], h_s[...])


def _ffn1_inproj_t(x, batch, seq, tm, n1, wgu, wo, nmix, wnat, wqkv_t, qg_tab, kg_tab, mg, ones_bd, cos_t, sin_t):
    tiles = seq // tm

    def rows(width):
        return pl.BlockSpec((tm, width), lambda i: (i, 0))

    feat = pl.BlockSpec((None, MOBA_WIDTH, tm), lambda i: (i // tiles, 0, i % tiles))
    tab = pl.BlockSpec((ROT_HALF, tm), lambda i: (0, i % tiles))
    resident = 2 * (wgu.size + wo.size + wnat.size + wqkv_t.size + ones_bd.size) + tm * (D_MODEL + D_FF) * 2
    streamed = 2 * 4 * tm * (2 * D_MODEL + POOL_WIDTH + MEM_WIDTH + 3 * MOBA_WIDTH + 2 * ROT_HALF)
    feat_shape = jax.ShapeDtypeStruct((batch, MOBA_WIDTH, seq), F32)
    return pl.pallas_call(
        _ffn1_inproj_t_kernel,
        grid=(batch * tiles,),
        in_specs=[rows(D_MODEL), _const_spec((1, D_MODEL)), _const_spec(wgu.shape), _const_spec(wo.shape),
                  _const_spec((1, D_MODEL)), _const_spec(wnat.shape), _const_spec(wqkv_t.shape),
                  _const_spec(qg_tab.shape), _const_spec(kg_tab.shape), _const_spec((1, MEM_WIDTH)),
                  _const_spec(ones_bd.shape), tab, tab],
        out_specs=[rows(D_MODEL), rows(POOL_WIDTH), rows(MEM_WIDTH), feat, feat, feat],
        out_shape=[jax.ShapeDtypeStruct((batch * seq, w), F32) for w in (D_MODEL, POOL_WIDTH, MEM_WIDTH)]
        + [feat_shape] * 3,
        scratch_shapes=[pltpu.VMEM((tm, D_MODEL), BF16), pltpu.VMEM((tm, D_FF), BF16)],
        compiler_params=pltpu.CompilerParams(dimension_semantics=("parallel",),
                                             vmem_limit_bytes=_vmem_limit(resident + streamed)),
        name="ffn1_inproj",
    )(x, n1, wgu, wo, nmix, wnat, wqkv_t, qg_tab, kg_tab, mg, ones_bd, cos_t, sin_t)


def _outproj_ffn2_kernel(x1_ref, yp_ref, ya_ref, ym_ref, wout_ref, n2_ref, wgu_ref, wo_ref, y_ref, h_s, act_s):
    c1 = POOL_WIDTH
    c2 = c1 + MOBA_WIDTH
    mix = (_dot(yp_ref[...], wout_ref[:c1, :]) + _dot(ya_ref[...], wout_ref[c1:c2, :])
           + _dot(ym_ref[...], wout_ref[c2:, :]))
    x2 = x1_ref[...] + mix
    h_s[...] = _rmsnorm(x2, n2_ref[...]).astype(BF16)
    y_ref[...] = x2 + 0.5 * _swiglu(h_s, wgu_ref, wo_ref, act_s)


def _outproj_ffn2(x1, y_pool, y_moba, y_mem, tm, wout, n2, wgu, wo):
    n = x1.shape[0]

    def rows(width):
        return pl.BlockSpec((tm, width), lambda i: (i, 0))

    resident = 2 * (wgu.size + wo.size + wout.size) + tm * (D_MODEL + D_FF) * 2
    streamed = 2 * tm * (4 * 2 * D_MODEL + 2 * D_MODEL)
    return pl.pallas_call(
        _outproj_ffn2_kernel,
        grid=(n // tm,),
        in_specs=[rows(D_MODEL), rows(POOL_WIDTH), rows(MOBA_WIDTH), rows(MEM_WIDTH),
                  _const_spec(wout.shape), _const_spec((1, D_MODEL)), _const_spec(wgu.shape), _const_spec(wo.shape)],
        out_specs=rows(D_MODEL),
        out_shape=jax.ShapeDtypeStruct((n, D_MODEL), F32),
        scratch_shapes=[pltpu.VMEM((tm, D_MODEL), BF16), pltpu.VMEM((tm, D_FF), BF16)],
        compiler_params=pltpu.CompilerParams(dimension_semantics=("parallel",),
                                             vmem_limit_bytes=_vmem_limit(resident + streamed)),
        name="outproj_ffn2",
    )(x1, y_pool, y_moba, y_mem, wout, n2, wgu, wo)


def _moba_prompt_kernel(qt_ref, kt_ref, vt_ref, o_ref, kb_s, vb_s, km_s, s_s, p_s):
    seq = kt_ref.shape[1]
    n_blocks = seq // MOBA_BLOCK
    blocks = [slice(n * MOBA_BLOCK, (n + 1) * MOBA_BLOCK) for n in range(n_blocks)]
    lane_k = lax.broadcasted_iota(jnp.int32, (MOBA_BLOCK, LANES), 1)
    km_s[...] = jnp.zeros_like(km_s)
    for n in range(n_blocks):
        k_n = kt_ref[:, blocks[n]].T
        km_s[n:n + 1, :] = jnp.mean(k_n, axis=0, keepdims=True)
        for h in range(HEADS_PER_VREG):
            own = (lane_k < HEAD_DIM) if h == 0 else (lane_k >= HEAD_DIM)
            tag_lane = (1 - h) * HEAD_DIM + n
            kb_s[h, blocks[n], :] = jnp.where(own, k_n, jnp.where(lane_k == tag_lane, 1.0, 0.0)).astype(BF16)
    vb_s[...] = vt_ref[...].astype(BF16)

    lane_m = lax.broadcasted_iota(jnp.int32, km_s.shape, 1)
    kmean = km_s[...]
    km_split = [_split_bf16(jnp.where((lane_m < HEAD_DIM) if h == 0 else (lane_m >= HEAD_DIM), kmean, 0.0))
                for h in range(HEADS_PER_VREG)]
    blk = lax.broadcasted_iota(jnp.int32, (SUBLANES, MOBA_BLOCK), 0)
    key_id = lax.broadcasted_iota(jnp.int32, (MOBA_BLOCK, MOBA_BLOCK), 0)
    qry_id = lax.broadcasted_iota(jnp.int32, (MOBA_BLOCK, MOBA_BLOCK), 1)
    spare_rows = HEAD_DIM - SUBLANES

    for i in range(n_blocks):
        q_t = qt_ref[:, blocks[i]]
        q_hi, q_lo = _split_bf16(q_t)
        outs = []
        for h in range(HEADS_PER_VREG):
            if i == 0:
                bias = jnp.zeros((SUBLANES, MOBA_BLOCK), F32)
            else:
                km_hi, km_lo = km_split[h]
                s = (_dot(km_hi, q_hi) + _dot(km_lo, q_hi) + _dot(km_hi, q_lo))[:SUBLANES]
                s = jnp.where(blk < i, s, -jnp.inf)
                rank = jnp.zeros(s.shape, jnp.int32)
                for m in range(i):
                    sm = s[m:m + 1, :]
                    beats = (sm > s) | ((sm == s) & (blk > m))
                    rank = rank + beats.astype(jnp.int32)
                allowed = ((blk < i) & (rank < MOBA_TOPK)) | (blk == i)
                bias = jnp.where(allowed, 0.0, MASKED)
            zeros = jnp.zeros((spare_rows, MOBA_BLOCK), F32)
            if h == 0:
                q_aug = jnp.concatenate([q_t[:HEAD_DIM], bias, zeros], axis=0).astype(BF16)
            else:
                q_aug = jnp.concatenate([bias, zeros, q_t[HEAD_DIM:]], axis=0).astype(BF16)
            m_row = None
            for j in range(i + 1):
                st = _dot(kb_s[h, blocks[j], :], q_aug)
                if j == i:
                    st = jnp.where(key_id <= qry_id, st, MASKED)
                s_s[h, blocks[j], :] = st
                m_j = jnp.max(st, axis=0, keepdims=True)
                m_row = m_j if m_row is None else jnp.maximum(m_row, m_j)
            l_row = None
            for j in range(i + 1):
                p = jnp.exp2(s_s[h, blocks[j], :] - m_row)
                p_s[h, blocks[j], :] = p.astype(BF16)
                l_j = jnp.sum(p, axis=0, keepdims=True)
                l_row = l_j if l_row is None else l_row + l_j
            n_keys = (i + 1) * MOBA_BLOCK
            acc = _dot(vb_s[h * HEAD_DIM:(h + 1) * HEAD_DIM, :n_keys], p_s[h, :n_keys, :])
            outs.append(acc / l_row)
        o_ref[blocks[i], :] = jnp.concatenate(outs, axis=0).T.astype(o_ref.dtype)


def _moba_prompt(q_t, k_t, v_t):
    batch, _, seq = k_t.shape
    n_blocks = seq // MOBA_BLOCK
    pairs = MOBA_WIDTH // LANES
    assert n_blocks <= SUBLANES and n_blocks <= HEAD_DIM
    feat = pl.BlockSpec((None, LANES, seq), lambda b, p: (b, p, 0))
    return pl.pallas_call(
        _moba_prompt_kernel,
        grid=(batch, pairs),
        in_specs=[feat, feat, feat],
        out_specs=pl.BlockSpec((seq, LANES), lambda b, p: (b, p)),
        out_shape=jax.ShapeDtypeStruct((batch * seq, MOBA_WIDTH), BF16),
        scratch_shapes=[pltpu.VMEM((HEADS_PER_VREG, seq, LANES), BF16),
                        pltpu.VMEM((LANES, seq), BF16),
                        pltpu.VMEM((SCORE_ROWS, LANES), F32),
                        pltpu.VMEM((HEADS_PER_VREG, seq, MOBA_BLOCK), F32),
                        pltpu.VMEM((HEADS_PER_VREG, seq, MOBA_BLOCK), BF16)],
        compiler_params=pltpu.CompilerParams(dimension_semantics=("parallel", "parallel")),
        name="moba_prompt",
    )(q_t, k_t, v_t)


def _moba_sample_kernel(pt_ref, q_ref, kn_ref, vn_ref, ck_ref, cv_ref, o_ref, kbuf, vbuf, lg_s, sem):
    b = pl.program_id(0)
    n_pages = kbuf.shape[0]
    pages_per_block = MOBA_BLOCK // PAGE_SIZE
    n_blocks = n_pages // pages_per_block
    rows = lg_s.shape[1]

    def page_copy(cache_ref, buf, j, s):
        return pltpu.make_async_copy(cache_ref.at[pt_ref[b, j]], buf.at[j], sem.at[s])

    for j in range(n_pages):
        page_copy(ck_ref, kbuf, j, 0).start()
    for j in range(n_pages):
        page_copy(cv_ref, vbuf, j, 1).start()

    q = q_ref[0]
    row = lax.broadcasted_iota(jnp.int32, (rows, MOBA_WIDTH), 0)
    own_head = lax.broadcasted_iota(jnp.int32, (rows, MOBA_WIDTH), 1) // HEAD_DIM == (row & (MOBA_HEADS - 1))
    q_hi, q_lo = _split_bf16(q)
    q_t = jnp.where(own_head, jnp.where(row < MOBA_HEADS, q_hi.astype(F32), q_lo.astype(F32)), 0.0).astype(BF16)

    for j in range(n_pages):
        page_copy(ck_ref, kbuf, j, 0).wait()

    def score_page(j, carry):
        lg_s[j] = _dot(q_t, kbuf[j].astype(BF16))
        return carry

    lax.fori_loop(0, n_pages, score_page, 0)
    parts = lg_s[...]
    lg = parts[:, :MOBA_HEADS, :] + parts[:, MOBA_HEADS:, :]

    blk = lg.reshape(n_blocks, pages_per_block, MOBA_HEADS, PAGE_SIZE)
    blk_sum = blk[:, 0]
    for r in range(1, pages_per_block):
        blk_sum = blk_sum + blk[:, r]
    score_col = jnp.sum(blk_sum, axis=-1, keepdims=True) * (1.0 / MOBA_BLOCK)
    lane = lax.broadcasted_iota(jnp.int32, (MOBA_HEADS, LANES), 1)
    blk_of_lane = lane & (n_blocks - 1)
    score = jnp.zeros((MOBA_HEADS, LANES), F32)
    for n in range(n_blocks):
        score = jnp.where(blk_of_lane == n, score_col[n], score)
    rank = jnp.zeros(score.shape, jnp.int32)
    for k in range(1, n_blocks):
        other = pltpu.roll(score, k, 1)
        beats = (other > score) | ((other == score) & (blk_of_lane >= k))
        rank = rank + beats.astype(jnp.int32)
    bias = jnp.where(rank < MOBA_TOPK, 0.0, MASKED)

    q_rows = jnp.where(own_head[:MOBA_HEADS], q, 0.0)
    lg_self = jnp.sum(q_rows * kn_ref[0], axis=-1, keepdims=True)

    page_bias = [jnp.broadcast_to(bias[:, n:n + 1], (MOBA_HEADS, PAGE_SIZE)) for n in range(n_blocks)]
    lg = (blk + jnp.stack(page_bias)[:, None]).reshape(n_pages, MOBA_HEADS, PAGE_SIZE)
    m = jnp.maximum(jnp.max(jnp.max(lg, axis=0), axis=-1, keepdims=True), lg_self)
    p = jnp.exp(lg - m)
    p_self = jnp.exp(lg_self - m)
    denom = jnp.sum(jnp.sum(p, axis=0), axis=-1, keepdims=True) + p_self
    lg_s[:, :MOBA_HEADS, :] = p / denom
    lg_s[:, MOBA_HEADS:, :] = jnp.zeros((n_pages, rows - MOBA_HEADS, PAGE_SIZE), F32)

    for j in range(n_pages):
        page_copy(cv_ref, vbuf, j, 1).wait()

    def value_page(j, acc):
        return acc + _dot_nt(lg_s[j].astype(BF16), vbuf[j].astype(BF16))

    acc = lax.fori_loop(0, n_pages, value_page, jnp.zeros((rows, MOBA_WIDTH), F32))
    acc = acc[:MOBA_HEADS] + (p_self / denom) * vn_ref[0]
    o_ref[0] = jnp.sum(jnp.where(own_head[:MOBA_HEADS], acc, 0.0), axis=0, keepdims=True).astype(o_ref.dtype)


def _moba_sample(page_table, q, k_new, v_new, cache_k_t, cache_v_t):
    n_rows, n_pages = page_table.shape
    assert (n_pages * PAGE_SIZE) % MOBA_BLOCK == 0
    n_blocks = n_pages * PAGE_SIZE // MOBA_BLOCK
    assert LANES % n_blocks == 0 and n_blocks & (n_blocks - 1) == 0
    row_spec = pl.BlockSpec((1, 1, MOBA_WIDTH), lambda b, pt: (b, 0, 0))
    any_spec = pl.BlockSpec(memory_space=pl.ANY)
    page_buf = pltpu.VMEM((n_pages, MOBA_WIDTH, PAGE_SIZE), F32)
    resident = 2 * n_pages * MOBA_WIDTH * PAGE_SIZE * 4
    return pl.pallas_call(
        _moba_sample_kernel,
        grid_spec=pltpu.PrefetchScalarGridSpec(
            num_scalar_prefetch=1,
            grid=(n_rows,),
            in_specs=[row_spec, row_spec, row_spec, any_spec, any_spec],
            out_specs=row_spec,
            scratch_shapes=[page_buf, page_buf, pltpu.VMEM((n_pages, 2 * MOBA_HEADS, PAGE_SIZE), F32),
                            pltpu.SemaphoreType.DMA((2,))]),
        out_shape=jax.ShapeDtypeStruct((n_rows, 1, MOBA_WIDTH), BF16),
        compiler_params=pltpu.CompilerParams(dimension_semantics=("arbitrary",),
                                             vmem_limit_bytes=_vmem_limit(resident)),
        name="moba_sample",
    )(page_table, q.reshape(n_rows, 1, MOBA_WIDTH), k_new.reshape(n_rows, 1, MOBA_WIDTH),
      v_new.reshape(n_rows, 1, MOBA_WIDTH), cache_k_t, cache_v_t)


def _mem_kv_kernel(mem_ref, g_ref, wt_ref, kg_ref, mk_ref, mv_ref):
    h = _rmsnorm(mem_ref[...], g_ref[...]).astype(BF16)
    kv_t = _dot_nt(wt_ref[...], h)
    mk_ref[...] = _head_rmsnorm_t(kv_t[:MEM_WIDTH], _lane_tile(kg_ref[...], h.shape[0]))
    mv_ref[...] = kv_t[MEM_WIDTH:]


def _mem_kv(mem, batch, mem_len, g, w_t, kg_tab):
    feat = pl.BlockSpec((None, MEM_WIDTH, mem_len), lambda b: (b, 0, 0))
    return pl.pallas_call(
        _mem_kv_kernel,
        grid=(batch,),
        in_specs=[pl.BlockSpec((mem_len, D_MODEL), lambda b: (b, 0)), _const_spec((1, D_MODEL)),
                  _const_spec(w_t.shape), _const_spec(kg_tab.shape)],
        out_specs=[feat, feat],
        out_shape=[jax.ShapeDtypeStruct((batch, MEM_WIDTH, mem_len), F32)] * 2,
        compiler_params=pltpu.CompilerParams(dimension_semantics=("parallel",)),
        name="mem_kv",
    )(mem, g, w_t, kg_tab)


def _pool_select(lane, a2, a4, a8, a16):
    return jnp.where(lane < POOL_GROUP, a2,
                     jnp.where(lane < 2 * POOL_GROUP, a4, jnp.where(lane < 3 * POOL_GROUP, a8, a16)))


def _mix_prompt_kernel(u_ref, halo_ref, qm_ref, mk_ref, mv_ref, wbd_ref, scale_ref, yp_ref, ym_ref):
    t = pl.program_id(1)
    tm = u_ref.shape[0]
    halo_rows = halo_ref.shape[0]
    u = u_ref[...]
    ext = jnp.concatenate([jnp.where(t > 0, halo_ref[...], 0.0), u], axis=0)
    a2 = ext + pltpu.roll(ext, 1, 0)
    a4 = a2 + pltpu.roll(a2, 2, 0)
    a8 = a4 + pltpu.roll(a4, 4, 0)
    a16 = a8 + pltpu.roll(a8, 8, 0)
    lane = lax.broadcasted_iota(jnp.int32, u.shape, 1)
    sums = _pool_select(lane, a2[halo_rows:], a4[halo_rows:], a8[halo_rows:], a16[halo_rows:])
    pos = t * tm + lax.broadcasted_iota(jnp.int32, u.shape, 0)
    window = _pool_select(lane, *POOL_WINDOWS)
    cnt = jnp.minimum(window, pos + 1).astype(F32)
    pooled = sums / cnt - u
    yp_ref[...] = (_dot(pooled.astype(BF16), wbd_ref[...]) * scale_ref[...]).astype(yp_ref.dtype)

    mem_len = mk_ref.shape[1]
    row_v = lax.broadcasted_iota(jnp.int32, (LANES, mem_len), 0)
    lane_q = lax.broadcasted_iota(jnp.int32, (tm, LANES), 1)
    for pair in range(MEM_WIDTH // LANES):
        cols = slice(pair * LANES, (pair + 1) * LANES)
        qp = qm_ref[:, cols]
        mk_t = mk_ref[cols, :].astype(BF16)
        mv_t = mv_ref[cols, :]
        out = None
        for h in range(HEADS_PER_VREG):
            in_head = (lane_q < HEAD_DIM) if h == 0 else (lane_q >= HEAD_DIM)
            in_head_v = (row_v < HEAD_DIM) if h == 0 else (row_v >= HEAD_DIM)
            lg = _dot(jnp.where(in_head, qp, 0.0).astype(BF16), mk_t)
            p = jnp.exp(lg - jnp.max(lg, axis=-1, keepdims=True))
            o = _dot_nt(p.astype(BF16), jnp.where(in_head_v, mv_t, 0.0).astype(BF16))
            o = o / jnp.sum(p, axis=-1, keepdims=True)
            out = o if out is None else out + o
        ym_ref[:, cols] = out.astype(ym_ref.dtype)


def _mix_prompt(u, qm, mk_t, mv_t, wbd, scale, batch, seq, tm):
    halo_rows = 16
    assert halo_rows > POOL_BUF and tm % halo_rows == 0 and seq % tm == 0
    tiles = seq // tm
    mem_len = mk_t.shape[2]

    def rows(width):
        return pl.BlockSpec((tm, width), lambda b, t: (b * tiles + t, 0))

    halo = pl.BlockSpec((halo_rows, POOL_WIDTH),
                        lambda b, t: (jnp.maximum((b * tiles + t) * (tm // halo_rows) - 1, 0), 0))
    mem = pl.BlockSpec((None, MEM_WIDTH, mem_len), lambda b, t: (b, 0, 0))
    return pl.pallas_call(
        _mix_prompt_kernel,
        grid=(batch, tiles),
        in_specs=[rows(POOL_WIDTH), halo, rows(MEM_WIDTH), mem, mem, _const_spec(wbd.shape),
                  _const_spec((1, POOL_WIDTH))],
        out_specs=[rows(POOL_WIDTH), rows(MEM_WIDTH)],
        out_shape=[jax.ShapeDtypeStruct((batch * seq, POOL_WIDTH), BF16),
                   jax.ShapeDtypeStruct((batch * seq, MEM_WIDTH), BF16)],
        compiler_params=pltpu.CompilerParams(dimension_semantics=("parallel", "parallel")),
        name="mix_prompt",
    )(u, u, qm, mk_t, mv_t, wbd, scale)


def _mix_sample_kernel(u_ref, st_ref, qm_ref, mk_ref, mv_ref, wbd_ref, scale_ref, yp_ref, ym_ref, *, pos):
    rows = u_ref.shape[0]
    u = u_ref[...]
    tail = u
    sums = {}
    for back in range(1, POOL_BUF + 1):
        tail = tail + st_ref[POOL_BUF - back]
        if back + 1 in POOL_WINDOWS:
            sums[back + 1] = tail / float(min(back + 1, pos + 1))
    lane = lax.broadcasted_iota(jnp.int32, u.shape, 1)
    pooled = _pool_select(lane, *(sums[w] for w in POOL_WINDOWS)) - u
    yp_ref[...] = (_dot(pooled.astype(BF16), wbd_ref[...]) * scale_ref[...]).astype(yp_ref.dtype)

    row_t = lax.broadcasted_iota(jnp.int32, (SCORE_ROWS, MEM_WIDTH), 0)
    own_head = lax.broadcasted_iota(jnp.int32, (SCORE_ROWS, MEM_WIDTH), 1) // HEAD_DIM == row_t
    for r in range(rows):
        q = jnp.broadcast_to(qm_ref[r:r + 1, :], (SCORE_ROWS, MEM_WIDTH))
        q_t = jnp.where(own_head, q, 0.0).astype(BF16)
        lg = _dot(q_t, mk_ref[r].astype(BF16))
        p = jnp.exp(lg - jnp.max(lg, axis=-1, keepdims=True))
        p = p / jnp.sum(p, axis=-1, keepdims=True)
        o = _dot_nt(p.astype(BF16), mv_ref[r].astype(BF16))
        ym_ref[r:r + 1, :] = jnp.sum(jnp.where(own_head, o, 0.0), axis=0, keepdims=True).astype(ym_ref.dtype)


def _mix_sample(u, state_t, qm, mk_t, mv_t, wbd, scale, rows_per_step, pos):
    n = u.shape[0]
    mem_len = mk_t.shape[2]

    def rows(width):
        return pl.BlockSpec((rows_per_step, width), lambda i: (i, 0))

    mem = pl.BlockSpec((rows_per_step, MEM_WIDTH, mem_len), lambda i: (i, 0, 0))
    return pl.pallas_call(
        functools.partial(_mix_sample_kernel, pos=pos),
        grid=(n // rows_per_step,),
        in_specs=[rows(POOL_WIDTH), pl.BlockSpec((POOL_BUF, rows_per_step, POOL_WIDTH), lambda i: (0, i, 0)),
                  rows(MEM_WIDTH), mem, mem, _const_spec(wbd.shape), _const_spec((1, POOL_WIDTH))],
        out_specs=[rows(POOL_WIDTH), rows(MEM_WIDTH)],
        out_shape=[jax.ShapeDtypeStruct((n, POOL_WIDTH), BF16), jax.ShapeDtypeStruct((n, MEM_WIDTH), BF16)],
        compiler_params=pltpu.CompilerParams(dimension_semantics=("parallel",)),
        name="mix_sample",
    )(u, state_t, qm, mk_t, mv_t, wbd, scale)


def _rope_angles(pos):
    inv = jnp.power(jnp.float32(ROPE_THETA), -jnp.arange(ROT_HALF, dtype=F32) * (2.0 / ROT_DIM))
    return pos.astype(F32)[:, None] * inv[None, :]


def _rope_tables(pos):
    ang = _rope_angles(pos)
    cos, sin = jnp.cos(ang), jnp.sin(ang)
    rest = HEAD_DIM - ROT_DIM
    ones = jnp.ones((pos.shape[0], rest), F32)
    zeros = jnp.zeros((pos.shape[0], rest), F32)
    cos_h = jnp.concatenate([cos, cos, ones], axis=1)
    sin_h = jnp.concatenate([-sin, sin, zeros], axis=1)
    return jnp.tile(cos_h, (1, HEADS_PER_VREG)), jnp.tile(sin_h, (1, HEADS_PER_VREG))


def _block_diag_ones(width):
    head = jnp.arange(width) // HEAD_DIM
    return (head[:, None] == head[None, :]).astype(BF16)


def _tile_gain(g, width):
    return jnp.tile(g, width // HEAD_DIM).reshape(1, width)


def _gain_table(g, width):
    return jnp.broadcast_to(jnp.tile(g, width // HEAD_DIM)[:, None], (width, LANES))


def _feature_major_to_heads(a_t, n_heads):
    batch, _, tokens = a_t.shape
    return jnp.transpose(a_t.reshape(batch, n_heads, HEAD_DIM, tokens), (0, 3, 1, 2))[None]


def kernel(x_prompt, x_sample, mem_prompt, cache_k, cache_v, cache_mem_k, cache_mem_v, state_pool, page_table,
           ffn1_norm, ffn1_w_in, ffn1_w_out, mix_norm, w_in, pool_w, pool_scale, q_norm, k_norm,
           mem_norm, mem_w_kv, mem_q_norm, mem_k_norm, w_out, ffn2_norm, ffn2_w_in, ffn2_w_out):
    depth = ffn1_norm.shape[0]
    assert depth == 1
    n_p, t_p, _ = x_prompt.shape
    n_s, t_s, _ = x_sample.shape
    assert t_s == 1
    past_len = page_table.shape[1] * PAGE_SIZE
    mem_len = mem_prompt.shape[1]
    l = 0

    wgu1, wo1 = ffn1_w_in[l].astype(BF16), ffn1_w_out[l].astype(BF16)
    wgu2, wo2 = ffn2_w_in[l].astype(BF16), ffn2_w_out[l].astype(BF16)
    win, wout = w_in[l].astype(BF16), w_out[l].astype(BF16)
    c1 = POOL_WIDTH
    c4 = c1 + 3 * MOBA_WIDTH
    wnat = jnp.concatenate([win[:, :c1], win[:, c4:]], axis=1)
    wqkv_t = win[:, c1:c4].T
    wkv_t = mem_w_kv[l].astype(BF16).T
    n1, nmix, n2, nmem = (a[l].reshape(1, D_MODEL) for a in (ffn1_norm, mix_norm, ffn2_norm, mem_norm))
    qg, kg = _tile_gain(q_norm[l], MOBA_WIDTH), _tile_gain(k_norm[l], MOBA_WIDTH)
    mqg = _tile_gain(mem_q_norm[l], MEM_WIDTH)
    qg_tab, kg_tab = _gain_table(q_norm[l], MOBA_WIDTH), _gain_table(k_norm[l], MOBA_WIDTH)
    mkg_tab = _gain_table(mem_k_norm[l], MEM_WIDTH)
    ones_bd = _block_diag_ones(MOBA_WIDTH)
    ones_mem = ones_bd[:MEM_WIDTH, :MEM_WIDTH]
    wbd = jax.scipy.linalg.block_diag(*[pool_w[l, g] for g in range(len(POOL_WINDOWS))]).astype(BF16)
    scale = pool_scale[l].reshape(1, POOL_WIDTH)

    xp = x_prompt.reshape(n_p * t_p, D_MODEL)
    ang_p = _rope_angles(jnp.arange(t_p, dtype=jnp.int32)).T
    x1p, up, qmp, qt_p, kt_p, vt_p = _ffn1_inproj_t(xp, n_p, t_p, ROW_TILE, n1, wgu1, wo1, nmix, wnat, wqkv_t,
                                                   qg_tab, kg_tab, mqg, ones_mem, jnp.cos(ang_p), jnp.sin(ang_p))
    mkt_p, mvt_p = _mem_kv(mem_prompt.reshape(n_p * mem_len, D_MODEL), n_p, mem_len, nmem, wkv_t, mkg_tab)
    ya_p = _moba_prompt(qt_p, kt_p, vt_p)
    yp_p, ym_p = _mix_prompt(up, qmp, mkt_p, mvt_p, wbd, scale, n_p, t_p, ROW_TILE)
    y_prompt = _outproj_ffn2(x1p, yp_p, ya_p, ym_p, ROW_TILE, wout, n2, wgu2, wo2)

    xs = x_sample.reshape(n_s, D_MODEL)
    cos_s, sin_s = _rope_tables(jnp.full((n_s,), past_len, jnp.int32))
    x1s, us, qs, ks, vs, qms = _ffn1_inproj(xs, n1, wgu1, wo1, nmix, win, qg, kg, mqg, ones_bd, cos_s, sin_s)

    def feature_major(cache, n_heads):
        rows, tokens = cache.shape[1], cache.shape[2]
        return jnp.transpose(cache[l], (0, 2, 3, 1)).reshape(rows, n_heads * HEAD_DIM, tokens)

    ya_s = _moba_sample(page_table, qs, ks, vs, feature_major(cache_k, MOBA_HEADS),
                        feature_major(cache_v, MOBA_HEADS)).reshape(n_s, MOBA_WIDTH)
    state = state_pool[l]
    yp_s, ym_s = _mix_sample(us, jnp.swapaxes(state, 0, 1), qms, feature_major(cache_mem_k, MEM_HEADS),
                             feature_major(cache_mem_v, MEM_HEADS), wbd, scale, 8, past_len)
    y_sample = _outproj_ffn2(x1s, yp_s, ya_s, ym_s, n_s, wout, n2, wgu2, wo2)

    pool_state_p = up.reshape(n_p, t_p, POOL_WIDTH)[:, t_p - POOL_BUF:][None]
    pool_state_s = jnp.concatenate([state[:, 1:], us[:, None, :]], axis=1)[None]
    return (y_prompt.reshape(n_p, t_p, D_MODEL), y_sample.reshape(n_s, t_s, D_MODEL),
            _feature_major_to_heads(kt_p, MOBA_HEADS), _feature_major_to_heads(vt_p, MOBA_HEADS),
            _feature_major_to_heads(mkt_p, MEM_HEADS), _feature_major_to_heads(mvt_p, MEM_HEADS),
            pool_state_p,
            ks.reshape(1, n_s, t_s, MOBA_HEADS, HEAD_DIM), vs.reshape(1, n_s, t_s, MOBA_HEADS, HEAD_DIM),
            pool_state_s)
```

```python
import functools

import jax
import jax.numpy as jnp
from jax import lax
from jax.experimental import pallas as pl
from jax.experimental.pallas import tpu as pltpu

F32 = jnp.float32
BF16 = jnp.bfloat16

D_MODEL = 1024
HEAD_DIM = 64
POOL_WIDTH = 256
POOL_WINDOWS = (2, 4, 8, 16)
POOL_GROUP = POOL_WIDTH // len(POOL_WINDOWS)
POOL_BUF = max(POOL_WINDOWS) - 1
MOBA_WIDTH = 512
MOBA_HEADS = MOBA_WIDTH // HEAD_DIM
MOBA_BLOCK = 256
MOBA_TOPK = 3
MEM_WIDTH = 256
MEM_HEADS = MEM_WIDTH // HEAD_DIM
ROT_DIM = HEAD_DIM // 4
ROT_HALF = ROT_DIM // 2
ROPE_THETA = 500000.0
D_FF = 2816
RMS_EPS = 1e-6
PAGE_SIZE = 128
ATTN_SCALE = HEAD_DIM ** -0.5
LOG2_E = 1.4426950408889634

LANES = 128
SUBLANES = 8
V7X_VMEM_BYTES = 64 * 1024 * 1024

HEADS_PER_VREG = LANES // HEAD_DIM
MASKED = -1e30
FF_CHUNK = 256
ROW_TILE = 512
SCORE_ROWS = 16
SCORE_UNROLL = 4


def _dot(a, b):
    return jnp.dot(a, b, preferred_element_type=F32)


def _dot_nt(a, b):
    return lax.dot_general(a, b, (((1,), (1,)), ((), ())), preferred_element_type=F32)


def _split_bf16(x):
    hi = x.astype(BF16)
    lo = (x - hi.astype(F32)).astype(BF16)
    return hi, lo


def _vmem_limit(resident_bytes):
    return int(min(resident_bytes + 16 * 1024 * 1024, V7X_VMEM_BYTES - 4 * 1024 * 1024))


def _rmsnorm(x, g):
    ms = jnp.mean(x * x, axis=-1, keepdims=True)
    return x * lax.rsqrt(ms + RMS_EPS) * g


def _head_rmsnorm(x, g, ones_bd):
    hi, lo = _split_bf16(x * x)
    ms = (_dot(hi, ones_bd) + _dot(lo, ones_bd)) * (1.0 / HEAD_DIM)
    return x * lax.rsqrt(ms + RMS_EPS) * g


def _head_rmsnorm_t(x, g):
    heads = []
    for r in range(0, x.shape[0], HEAD_DIM):
        xh = x[r:r + HEAD_DIM]
        ms = jnp.mean(xh * xh, axis=0, keepdims=True)
        heads.append(xh * lax.rsqrt(ms + RMS_EPS))
    return jnp.concatenate(heads, axis=0) * g


def _rope(x, cos, sin):
    width = x.shape[1]
    lane = lax.broadcasted_iota(jnp.int32, x.shape, 1)
    first_half = (lane & (HEAD_DIM - 1)) < ROT_HALF
    partner = jnp.where(first_half, pltpu.roll(x, width - ROT_HALF, 1), pltpu.roll(x, ROT_HALF, 1))
    return x * cos + partner * sin


def _rope_t(x, cos, sin):
    pieces = []
    for r in range(0, x.shape[0], HEAD_DIM):
        x1, x2 = x[r:r + ROT_HALF], x[r + ROT_HALF:r + ROT_DIM]
        pieces += [x1 * cos - x2 * sin, x2 * cos + x1 * sin, x[r + ROT_DIM:r + HEAD_DIM]]
    return jnp.concatenate(pieces, axis=0)


def _lane_tile(table, n):
    return jnp.concatenate([table] * (n // LANES), axis=1)


def _swiglu(h_s, wgu_ref, wo_ref, act_s):
    for c in range(D_FF // FF_CHUNK):
        lo = c * FF_CHUNK
        gate = _dot(h_s[...], wgu_ref[:, lo:lo + FF_CHUNK])
        up = _dot(h_s[...], wgu_ref[:, D_FF + lo:D_FF + lo + FF_CHUNK])
        act_s[:, lo:lo + FF_CHUNK] = (jax.nn.silu(gate) * up).astype(BF16)
    return _dot(act_s[...], wo_ref[...])


def _const_spec(shape):
    return pl.BlockSpec(shape, lambda *_: (0,) * len(shape), pipeline_mode=pl.Buffered(1))


def _ffn1_trunk(x_ref, n1_ref, wgu_ref, wo_ref, nmix_ref, x1_ref, h_s, act_s):
    x = x_ref[...]
    h_s[...] = _rmsnorm(x, n1_ref[...]).astype(BF16)
    x1 = x + 0.5 * _swiglu(h_s, wgu_ref, wo_ref, act_s)
    x1_ref[...] = x1
    h_s[...] = _rmsnorm(x1, nmix_ref[...]).astype(BF16)


def _ffn1_inproj_kernel(x_ref, n1_ref, wgu_ref, wo_ref, nmix_ref, win_ref, qg_ref, kg_ref, mg_ref,
                        ones_ref, cos_ref, sin_ref,
                        x1_ref, u_ref, q_ref, k_ref, v_ref, qm_ref, h_s, act_s):
    _ffn1_trunk(x_ref, n1_ref, wgu_ref, wo_ref, nmix_ref, x1_ref, h_s, act_s)
    c1 = POOL_WIDTH
    c2 = c1 + MOBA_WIDTH
    c3 = c2 + MOBA_WIDTH
    c4 = c3 + MOBA_WIDTH
    u_ref[...] = _dot(h_s[...], win_ref[:, :c1])
    q = _dot(h_s[...], win_ref[:, c1:c2])
    k = _dot(h_s[...], win_ref[:, c2:c3])
    v_ref[...] = _dot(h_s[...], win_ref[:, c3:c4])
    qm = _dot(h_s[...], win_ref[:, c4:])
    reps = MOBA_WIDTH // LANES
    cos = jnp.concatenate([cos_ref[...]] * reps, axis=1)
    sin = jnp.concatenate([sin_ref[...]] * reps, axis=1)
    ones = ones_ref[...]
    q_ref[...] = _rope(_head_rmsnorm(q, qg_ref[...], ones), cos, sin) * ATTN_SCALE
    k_ref[...] = _rope(_head_rmsnorm(k, kg_ref[...], ones), cos, sin)
    qm_ref[...] = _head_rmsnorm(qm, mg_ref[...], ones_ref[:MEM_WIDTH, :MEM_WIDTH]) * ATTN_SCALE


def _ffn1_inproj(x, n1, wgu, wo, nmix, win, qg, kg, mg, ones_bd, cos_tab, sin_tab):
    n = x.shape[0]

    def full(width):
        return pl.BlockSpec((n, width), lambda i: (0, 0))

    widths = (D_MODEL, POOL_WIDTH, MOBA_WIDTH, MOBA_WIDTH, MOBA_WIDTH, MEM_WIDTH)
    resident = 2 * (wgu.size + wo.size + win.size + ones_bd.size)
    return pl.pallas_call(
        _ffn1_inproj_kernel,
        grid=(1,),
        in_specs=[full(D_MODEL), _const_spec((1, D_MODEL)), _const_spec(wgu.shape), _const_spec(wo.shape),
                  _const_spec((1, D_MODEL)), _const_spec(win.shape),
                  _const_spec((1, MOBA_WIDTH)), _const_spec((1, MOBA_WIDTH)), _const_spec((1, MEM_WIDTH)),
                  _const_spec(ones_bd.shape), full(LANES), full(LANES)],
        out_specs=[full(w) for w in widths],
        out_shape=[jax.ShapeDtypeStruct((n, w), F32) for w in widths],
        scratch_shapes=[pltpu.VMEM((n, D_MODEL), BF16), pltpu.VMEM((n, D_FF), BF16)],
        compiler_params=pltpu.CompilerParams(dimension_semantics=("arbitrary",),
                                             vmem_limit_bytes=_vmem_limit(resident)),
        name="ffn1_inproj_sample",
    )(x, n1, wgu, wo, nmix, win, qg, kg, mg, ones_bd, cos_tab, sin_tab)


def _ffn1_inproj_t_kernel(x_ref, n1_ref, wgu_ref, wo_ref, nmix_ref, wnat_ref, wqkv_t_ref, qg_ref, kg_ref, mg_ref,
                          ones_ref, cos_ref, sin_ref,
                          x1_ref, u_ref, qm_ref, qt_ref, kt_ref, vt_ref, h_s, act_s):
    _ffn1_trunk(x_ref, n1_ref, wgu_ref, wo_ref, nmix_ref, x1_ref, h_s, act_s)
    tm = x_ref.shape[0]
    u_ref[...] = _dot(h_s[...], wnat_ref[:, :POOL_WIDTH])
    qm = _dot(h_s[...], wnat_ref[:, POOL_WIDTH:])
    qm_ref[...] = _head_rmsnorm(qm, mg_ref[...], ones_ref[...]) * ATTN_SCALE
    cos, sin = cos_ref[...], sin_ref[...]
    q_t = _dot_nt(wqkv_t_ref[:MOBA_WIDTH, :], h_s[...])
    qt_ref[...] = _rope_t(_head_rmsnorm_t(q_t, _lane_tile(qg_ref[...], tm)), cos, sin) * (ATTN_SCALE * LOG2_E)
    k_t = _dot_nt(wqkv_t_ref[MOBA_WIDTH:2 * MOBA_WIDTH, :], h_s[...])
    kt_ref[...] = _rope_t(_head_rmsnorm_t(k_t, _lane_tile(kg_ref[...], tm)), cos, sin)
    vt_ref[...] = _dot_nt(wqkv_t_ref[2 * MOBA_WIDTH:, :], h_s[...])


def _ffn1_inproj_t(x, batch, seq, tm, n1, wgu, wo, nmix, wnat, wqkv_t, qg_tab, kg_tab, mg, ones_bd, cos_t, sin_t):
    tiles = seq // tm

    def rows(width):
        return pl.BlockSpec((tm, width), lambda i: (i, 0))

    feat = pl.BlockSpec((None, MOBA_WIDTH, tm), lambda i: (i // tiles, 0, i % tiles))
    tab = pl.BlockSpec((ROT_HALF, tm), lambda i: (0, i % tiles))
    resident = 2 * (wgu.size + wo.size + wnat.size + wqkv_t.size + ones_bd.size) + tm * (D_MODEL + D_FF) * 2
    streamed = 2 * 4 * tm * (2 * D_MODEL + POOL_WIDTH + MEM_WIDTH + 3 * MOBA_WIDTH + 2 * ROT_HALF)
    feat_shape = jax.ShapeDtypeStruct((batch, MOBA_WIDTH, seq), F32)
    return pl.pallas_call(
        _ffn1_inproj_t_kernel,
        grid=(batch * tiles,),
        in_specs=[rows(D_MODEL), _const_spec((1, D_MODEL)), _const_spec(wgu.shape), _const_spec(wo.shape),
                  _const_spec((1, D_MODEL)), _const_spec(wnat.shape), _const_spec(wqkv_t.shape),
                  _const_spec(qg_tab.shape), _const_spec(kg_tab.shape), _const_spec((1, MEM_WIDTH)),
                  _const_spec(ones_bd.shape), tab, tab],
        out_specs=[rows(D_MODEL), rows(POOL_WIDTH), rows(MEM_WIDTH), feat, feat, feat],
        out_shape=[jax.ShapeDtypeStruct((batch * seq, w), F32) for w in (D_MODEL, POOL_WIDTH, MEM_WIDTH)]
        + [feat_shape] * 3,
        scratch_shapes=[pltpu.VMEM((tm, D_MODEL), BF16), pltpu.VMEM((tm, D_FF), BF16)],
        compiler_params=pltpu.CompilerParams(dimension_semantics=("parallel",),
                                             vmem_limit_bytes=_vmem_limit(resident + streamed)),
        name="ffn1_inproj",
    )(x, n1, wgu, wo, nmix, wnat, wqkv_t, qg_tab, kg_tab, mg, ones_bd, cos_t, sin_t)


def _outproj_ffn2_kernel(x1_ref, yp_ref, ya_ref, ym_ref, wout_ref, n2_ref, wgu_ref, wo_ref, y_ref, h_s, act_s):
    c1 = POOL_WIDTH
    c2 = c1 + MOBA_WIDTH
    mix = (_dot(yp_ref[...], wout_ref[:c1, :]) + _dot(ya_ref[...], wout_ref[c1:c2, :])
           + _dot(ym_ref[...], wout_ref[c2:, :]))
    x2 = x1_ref[...] + mix
    h_s[...] = _rmsnorm(x2, n2_ref[...]).astype(BF16)
    y_ref[...] = x2 + 0.5 * _swiglu(h_s, wgu_ref, wo_ref, act_s)


def _outproj_ffn2(x1, y_pool, y_moba, y_mem, tm, wout, n2, wgu, wo):
    n = x1.shape[0]

    def rows(width):
        return pl.BlockSpec((tm, width), lambda i: (i, 0))

    resident = 2 * (wgu.size + wo.size + wout.size) + tm * (D_MODEL + D_FF) * 2
    streamed = 2 * tm * (4 * 2 * D_MODEL + 2 * D_MODEL)
    return pl.pallas_call(
        _outproj_ffn2_kernel,
        grid=(n // tm,),
        in_specs=[rows(D_MODEL), rows(POOL_WIDTH), rows(MOBA_WIDTH), rows(MEM_WIDTH),
                  _const_spec(wout.shape), _const_spec((1, D_MODEL)), _const_spec(wgu.shape), _const_spec(wo.shape)],
        out_specs=rows(D_MODEL),
        out_shape=jax.ShapeDtypeStruct((n, D_MODEL), F32),
        scratch_shapes=[pltpu.VMEM((tm, D_MODEL), BF16), pltpu.VMEM((tm, D_FF), BF16)],
        compiler_params=pltpu.CompilerParams(dimension_semantics=("parallel",),
                                             vmem_limit_bytes=_vmem_limit(resident + streamed)),
        name="outproj_ffn2",
    )(x1, y_pool, y_moba, y_mem, wout, n2, wgu, wo)


def _moba_prompt_kernel(qt_ref, kt_ref, vt_ref, o_ref, kb_s, vb_s, km_s, s_s, p_s):
    seq = kt_ref.shape[1]
    n_blocks = seq // MOBA_BLOCK
    blocks = [slice(n * MOBA_BLOCK, (n + 1) * MOBA_BLOCK) for n in range(n_blocks)]
    lane_k = lax.broadcasted_iota(jnp.int32, (MOBA_BLOCK, LANES), 1)
    km_s[...] = jnp.zeros_like(km_s)
    for n in range(n_blocks):
        k_n = kt_ref[:, blocks[n]].T
        km_s[n:n + 1, :] = jnp.mean(k_n, axis=0, keepdims=True)
        for h in range(HEADS_PER_VREG):
            own = (lane_k < HEAD_DIM) if h == 0 else (lane_k >= HEAD_DIM)
            tag_lane = (1 - h) * HEAD_DIM + n
            kb_s[h, blocks[n], :] = jnp.where(own, k_n, jnp.where(lane_k == tag_lane, 1.0, 0.0)).astype(BF16)
    vb_s[...] = vt_ref[...].astype(BF16)

    lane_m = lax.broadcasted_iota(jnp.int32, km_s.shape, 1)
    kmean = km_s[...]
    km_split = [_split_bf16(jnp.where((lane_m < HEAD_DIM) if h == 0 else (lane_m >= HEAD_DIM), kmean, 0.0))
                for h in range(HEADS_PER_VREG)]
    blk = lax.broadcasted_iota(jnp.int32, (SUBLANES, MOBA_BLOCK), 0)
    key_id = lax.broadcasted_iota(jnp.int32, (MOBA_BLOCK, MOBA_BLOCK), 0)
    qry_id = lax.broadcasted_iota(jnp.int32, (MOBA_BLOCK, MOBA_BLOCK), 1)
    spare_rows = HEAD_DIM - SUBLANES

    for i in range(n_blocks):
        q_t = qt_ref[:, blocks[i]]
        q_hi, q_lo = _split_bf16(q_t)
        outs = []
        for h in range(HEADS_PER_VREG):
            if i == 0:
                bias = jnp.zeros((SUBLANES, MOBA_BLOCK), F32)
            else:
                km_hi, km_lo = km_split[h]
                s = (_dot(km_hi, q_hi) + _dot(km_lo, q_hi) + _dot(km_hi, q_lo))[:SUBLANES]
                s = jnp.where(blk < i, s, -jnp.inf)
                rank = jnp.zeros(s.shape, jnp.int32)
                for m in range(i):
                    sm = s[m:m + 1, :]
                    beats = (sm > s) | ((sm == s) & (blk > m))
                    rank = rank + beats.astype(jnp.int32)
                allowed = ((blk < i) & (rank < MOBA_TOPK)) | (blk == i)
                bias = jnp.where(allowed, 0.0, MASKED)
            zeros = jnp.zeros((spare_rows, MOBA_BLOCK), F32)
            if h == 0:
                q_aug = jnp.concatenate([q_t[:HEAD_DIM], bias, zeros], axis=0).astype(BF16)
            else:
                q_aug = jnp.concatenate([bias, zeros, q_t[HEAD_DIM:]], axis=0).astype(BF16)
            m_row = None
            for j in range(i + 1):
                st = _dot(kb_s[h, blocks[j], :], q_aug)
                if j == i:
                    st = jnp.where(key_id <= qry_id, st, MASKED)
                s_s[h, blocks[j], :] = st
                m_j = jnp.max(st, axis=0, keepdims=True)
                m_row = m_j if m_row is None else jnp.maximum(m_row, m_j)
            l_row = None
            for j in range(i + 1):
                p = jnp.exp2(s_s[h, blocks[j], :] - m_row)
                p_s[h, blocks[j], :] = p.astype(BF16)
                l_j = jnp.sum(p, axis=0, keepdims=True)
                l_row = l_j if l_row is None else l_row + l_j
            n_keys = (i + 1) * MOBA_BLOCK
            acc = _dot(vb_s[h * HEAD_DIM:(h + 1) * HEAD_DIM, :n_keys], p_s[h, :n_keys, :])
            outs.append(acc / l_row)
        o_ref[blocks[i], :] = jnp.concatenate(outs, axis=0).T.astype(o_ref.dtype)


def _moba_prompt(q_t, k_t, v_t):
    batch, _, seq = k_t.shape
    n_blocks = seq // MOBA_BLOCK
    pairs = MOBA_WIDTH // LANES
    assert n_blocks <= SUBLANES and n_blocks <= HEAD_DIM
    feat = pl.BlockSpec((None, LANES, seq), lambda b, p: (b, p, 0))
    return pl.pallas_call(
        _moba_prompt_kernel,
        grid=(batch, pairs),
        in_specs=[feat, feat, feat],
        out_specs=pl.BlockSpec((seq, LANES), lambda b, p: (b, p)),
        out_shape=jax.ShapeDtypeStruct((batch * seq, MOBA_WIDTH), BF16),
        scratch_shapes=[pltpu.VMEM((HEADS_PER_VREG, seq, LANES), BF16),
                        pltpu.VMEM((LANES, seq), BF16),
                        pltpu.VMEM((SCORE_ROWS, LANES), F32),
                        pltpu.VMEM((HEADS_PER_VREG, seq, MOBA_BLOCK), F32),
                        pltpu.VMEM((HEADS_PER_VREG, seq, MOBA_BLOCK), BF16)],
        compiler_params=pltpu.CompilerParams(dimension_semantics=("parallel", "parallel")),
        name="moba_prompt",
    )(q_t, k_t, v_t)


def _moba_sample_scores_kernel(pt_ref, qc_ref, q_ref, kn_ref, ck_ref, psel_ref, ids_ref, pself_ref,
                               kbuf, qb_s, lg_s, sem):
    b = pl.program_id(0)
    n_rows = pl.num_programs(0)
    n_pages = kbuf.shape[1]
    pages_per_block = MOBA_BLOCK // PAGE_SIZE
    n_blocks = n_pages // pages_per_block
    slot = lax.rem(b, 2)

    def page_copy(row, j, s):
        return pltpu.make_async_copy(ck_ref.at[pt_ref[row, j]], kbuf.at[s, j], sem.at[s])

    @pl.when(b == 0)
    def _():
        for j in range(n_pages):
            page_copy(0, j, 0).start()

    @pl.when(b + 1 < n_rows)
    def _():
        for j in range(n_pages):
            page_copy(b + 1, j, 1 - slot).start()

    qb_s[...] = jnp.broadcast_to(qc_ref[0], qb_s.shape)
    for j in range(n_pages):
        page_copy(b, j, slot).wait()

    def score_pages(g, carry):
        for u in range(SCORE_UNROLL):
            j = g * SCORE_UNROLL + u
            prod = kbuf[slot, j] * qb_s[...]
            lg_s[j] = jnp.sum(prod.reshape(MOBA_HEADS, HEAD_DIM, PAGE_SIZE), axis=1)
        return carry

    lax.fori_loop(0, n_pages // SCORE_UNROLL, score_pages, 0)
    lg = lg_s[...]

    blk = lg.reshape(n_blocks, pages_per_block, MOBA_HEADS, PAGE_SIZE)
    blk_sum = blk[:, 0]
    for r in range(1, pages_per_block):
        blk_sum = blk_sum + blk[:, r]
    score_col = jnp.sum(blk_sum, axis=-1, keepdims=True) * (1.0 / MOBA_BLOCK)
    lane = lax.broadcasted_iota(jnp.int32, (MOBA_HEADS, LANES), 1)
    blk_of_lane = lane & (n_blocks - 1)
    score = jnp.zeros((MOBA_HEADS, LANES), F32)
    for n in range(n_blocks):
        score = jnp.where(blk_of_lane == n, score_col[n], score)
    rank = jnp.zeros(score.shape, jnp.int32)
    for k in range(1, n_blocks):
        other = pltpu.roll(score, k, 1)
        beats = (other > score) | ((other == score) & (blk_of_lane >= k))
        rank = rank + beats.astype(jnp.int32)
    bias = jnp.where(rank < MOBA_TOPK, 0.0, MASKED)

    q = q_ref[0]
    own_head = (lax.broadcasted_iota(jnp.int32, (MOBA_HEADS, MOBA_WIDTH), 1) // HEAD_DIM
                == lax.broadcasted_iota(jnp.int32, (MOBA_HEADS, MOBA_WIDTH), 0))
    lg_self = jnp.sum(jnp.where(own_head, q, 0.0) * kn_ref[0], axis=-1, keepdims=True)

    page_bias = [jnp.broadcast_to(bias[:, n:n + 1], (MOBA_HEADS, PAGE_SIZE)) for n in range(n_blocks)]
    lg = blk + jnp.stack(page_bias)[:, None]
    m = jnp.maximum(jnp.max(jnp.max(jnp.max(lg, axis=0), axis=0), axis=-1, keepdims=True), lg_self)
    p = jnp.exp(lg - m)
    p_self = jnp.exp(lg_self - m)
    denom = jnp.sum(jnp.sum(jnp.sum(p, axis=0), axis=0), axis=-1, keepdims=True) + p_self
    p = p / denom
    pself_ref[0] = jnp.broadcast_to(p_self / denom, (MOBA_HEADS, LANES))

    first_period = lane < n_blocks
    ids = jnp.zeros((MOBA_HEADS, LANES), F32)
    for t in range(MOBA_TOPK):
        is_t = jnp.where(rank == t, 1.0, 0.0)
        for r in range(pages_per_block):
            kept = jnp.zeros((MOBA_HEADS, PAGE_SIZE), F32)
            for n in range(n_blocks):
                kept = kept + is_t[:, n:n + 1] * p[n, r]
            psel_ref[0, t * pages_per_block + r] = kept
        blk_id = jnp.sum(jnp.where(first_period, is_t * blk_of_lane.astype(F32), 0.0), axis=-1, keepdims=True)
        ids = jnp.where(lane == t, blk_id, ids)
    ids_ref[0] = ids.astype(jnp.int32)


def _moba_sample_values_kernel(pt_ref, ids_ref, psel_ref, pself_ref, vn_ref, cv_ref, o_ref, vbuf, sem):
    b = pl.program_id(0)
    n_rows = pl.num_programs(0)
    pages_per_block = MOBA_BLOCK // PAGE_SIZE
    per_head = MOBA_TOPK * pages_per_block
    slot = lax.rem(b, 2)

    def chunk_copy(row, c, s):
        h, t, r = c // per_head, (c % per_head) // pages_per_block, c % pages_per_block
        page = pt_ref[row, ids_ref[row, h * MOBA_TOPK + t] * pages_per_block + r]
        return pltpu.make_async_copy(cv_ref.at[page, pl.ds(h * HEAD_DIM, HEAD_DIM), :], vbuf.at[s, c], sem.at[s])

    n_chunks = MOBA_HEADS * per_head

    @pl.when(b == 0)
    def _():
        for c in range(n_chunks):
            chunk_copy(0, c, 0).start()

    @pl.when(b + 1 < n_rows)
    def _():
        for c in range(n_chunks):
            chunk_copy(b + 1, c, 1 - slot).start()

    for c in range(n_chunks):
        chunk_copy(b, c, slot).wait()

    heads = []
    for h in range(MOBA_HEADS):
        acc = jnp.zeros((HEAD_DIM, PAGE_SIZE), F32)
        for tr in range(per_head):
            acc = acc + vbuf[slot, h * per_head + tr] * psel_ref[0, tr, h:h + 1, :]
        heads.append(acc)
    acc_hi, acc_lo = _split_bf16(jnp.concatenate(heads, axis=0))
    ones = jnp.ones((SUBLANES, PAGE_SIZE), BF16)
    past = (_dot_nt(ones, acc_hi) + _dot_nt(ones, acc_lo))[0:1]
    own_head = (lax.broadcasted_iota(jnp.int32, (MOBA_HEADS, MOBA_WIDTH), 1) // HEAD_DIM
                == lax.broadcasted_iota(jnp.int32, (MOBA_HEADS, MOBA_WIDTH), 0))
    p_self = jnp.sum(jnp.where(own_head, _lane_tile(pself_ref[0], MOBA_WIDTH), 0.0), axis=0, keepdims=True)
    o_ref[0] = (past + p_self * vn_ref[0]).astype(o_ref.dtype)


def _moba_sample(page_table, q, k_new, v_new, cache_k_t, cache_v_t):
    n_rows, n_pages = page_table.shape
    pages_per_block = MOBA_BLOCK // PAGE_SIZE
    assert n_pages % pages_per_block == 0 and n_pages % SCORE_UNROLL == 0
    n_blocks = n_pages // pages_per_block
    assert LANES % n_blocks == 0 and n_blocks & (n_blocks - 1) == 0 and n_blocks >= MOBA_TOPK
    kept_pages = MOBA_TOPK * pages_per_block
    row_spec = pl.BlockSpec((1, 1, MOBA_WIDTH), lambda b, *_: (b, 0, 0))
    col_spec = pl.BlockSpec((1, MOBA_WIDTH, 1), lambda b, *_: (b, 0, 0))
    psel_spec = pl.BlockSpec((1, kept_pages, MOBA_HEADS, PAGE_SIZE), lambda b, *_: (b, 0, 0, 0))
    tile_spec = pl.BlockSpec((1, MOBA_HEADS, LANES), lambda b, *_: (b, 0, 0))
    any_spec = pl.BlockSpec(memory_space=pl.ANY)
    k_bytes = 2 * n_pages * MOBA_WIDTH * PAGE_SIZE * 4
    p_sel, ids, p_self = pl.pallas_call(
        _moba_sample_scores_kernel,
        grid_spec=pltpu.PrefetchScalarGridSpec(
            num_scalar_prefetch=1,
            grid=(n_rows,),
            in_specs=[col_spec, row_spec, row_spec, any_spec],
            out_specs=[psel_spec, tile_spec, tile_spec],
            scratch_shapes=[pltpu.VMEM((2, n_pages, MOBA_WIDTH, PAGE_SIZE), F32),
                            pltpu.VMEM((MOBA_WIDTH, PAGE_SIZE), F32),
                            pltpu.VMEM((n_pages, MOBA_HEADS, PAGE_SIZE), F32),
                            pltpu.SemaphoreType.DMA((2,))]),
        out_shape=[jax.ShapeDtypeStruct((n_rows, kept_pages, MOBA_HEADS, PAGE_SIZE), F32),
                   jax.ShapeDtypeStruct((n_rows, MOBA_HEADS, LANES), jnp.int32),
                   jax.ShapeDtypeStruct((n_rows, MOBA_HEADS, LANES), F32)],
        compiler_params=pltpu.CompilerParams(dimension_semantics=("arbitrary",),
                                             vmem_limit_bytes=_vmem_limit(k_bytes)),
        name="moba_sample_scores",
    )(page_table, q.reshape(n_rows, MOBA_WIDTH, 1), q.reshape(n_rows, 1, MOBA_WIDTH),
      k_new.reshape(n_rows, 1, MOBA_WIDTH), cache_k_t)
    kept_ids = ids[:, :, :MOBA_TOPK].reshape(n_rows, MOBA_HEADS * MOBA_TOPK)
    return pl.pallas_call(
        _moba_sample_values_kernel,
        grid_spec=pltpu.PrefetchScalarGridSpec(
            num_scalar_prefetch=2,
            grid=(n_rows,),
            in_specs=[psel_spec, tile_spec, row_spec, any_spec],
            out_specs=row_spec,
            scratch_shapes=[pltpu.VMEM((2, MOBA_HEADS * kept_pages, HEAD_DIM, PAGE_SIZE), F32),
                            pltpu.SemaphoreType.DMA((2,))]),
        out_shape=jax.ShapeDtypeStruct((n_rows, 1, MOBA_WIDTH), BF16),
        compiler_params=pltpu.CompilerParams(dimension_semantics=("arbitrary",)),
        name="moba_sample_values",
    )(page_table, kept_ids, p_sel, p_self, v_new.reshape(n_rows, 1, MOBA_WIDTH), cache_v_t)


def _mem_kv_kernel(mem_ref, g_ref, wt_ref, kg_ref, mk_ref, mv_ref):
    h = _rmsnorm(mem_ref[...], g_ref[...]).astype(BF16)
    kv_t = _dot_nt(wt_ref[...], h)
    mk_ref[...] = _head_rmsnorm_t(kv_t[:MEM_WIDTH], _lane_tile(kg_ref[...], h.shape[0]))
    mv_ref[...] = kv_t[MEM_WIDTH:]


def _mem_kv(mem, batch, mem_len, g, w_t, kg_tab):
    feat = pl.BlockSpec((None, MEM_WIDTH, mem_len), lambda b: (b, 0, 0))
    return pl.pallas_call(
        _mem_kv_kernel,
        grid=(batch,),
        in_specs=[pl.BlockSpec((mem_len, D_MODEL), lambda b: (b, 0)), _const_spec((1, D_MODEL)),
                  _const_spec(w_t.shape), _const_spec(kg_tab.shape)],
        out_specs=[feat, feat],
        out_shape=[jax.ShapeDtypeStruct((batch, MEM_WIDTH, mem_len), F32)] * 2,
        compiler_params=pltpu.CompilerParams(dimension_semantics=("parallel",)),
        name="mem_kv",
    )(mem, g, w_t, kg_tab)


def _pool_select(lane, a2, a4, a8, a16):
    return jnp.where(lane < POOL_GROUP, a2,
                     jnp.where(lane < 2 * POOL_GROUP, a4, jnp.where(lane < 3 * POOL_GROUP, a8, a16)))


def _mix_prompt_kernel(u_ref, halo_ref, qm_ref, mk_ref, mv_ref, wbd_ref, scale_ref, yp_ref, ym_ref):
    t = pl.program_id(1)
    tm = u_ref.shape[0]
    halo_rows = halo_ref.shape[0]
    u = u_ref[...]
    ext = jnp.concatenate([jnp.where(t > 0, halo_ref[...], 0.0), u], axis=0)
    a2 = ext + pltpu.roll(ext, 1, 0)
    a4 = a2 + pltpu.roll(a2, 2, 0)
    a8 = a4 + pltpu.roll(a4, 4, 0)
    a16 = a8 + pltpu.roll(a8, 8, 0)
    lane = lax.broadcasted_iota(jnp.int32, u.shape, 1)
    sums = _pool_select(lane, a2[halo_rows:], a4[halo_rows:], a8[halo_rows:], a16[halo_rows:])
    pos = t * tm + lax.broadcasted_iota(jnp.int32, u.shape, 0)
    window = _pool_select(lane, *POOL_WINDOWS)
    cnt = jnp.minimum(window, pos + 1).astype(F32)
    pooled = sums / cnt - u
    yp_ref[...] = (_dot(pooled.astype(BF16), wbd_ref[...]) * scale_ref[...]).astype(yp_ref.dtype)

    mem_len = mk_ref.shape[1]
    row_v = lax.broadcasted_iota(jnp.int32, (LANES, mem_len), 0)
    lane_q = lax.broadcasted_iota(jnp.int32, (tm, LANES), 1)
    for pair in range(MEM_WIDTH // LANES):
        cols = slice(pair * LANES, (pair + 1) * LANES)
        qp = qm_ref[:, cols]
        mk_t = mk_ref[cols, :].astype(BF16)
        mv_t = mv_ref[cols, :]
        out = None
        for h in range(HEADS_PER_VREG):
            in_head = (lane_q < HEAD_DIM) if h == 0 else (lane_q >= HEAD_DIM)
            in_head_v = (row_v < HEAD_DIM) if h == 0 else (row_v >= HEAD_DIM)
            lg = _dot(jnp.where(in_head, qp, 0.0).astype(BF16), mk_t)
            p = jnp.exp(lg - jnp.max(lg, axis=-1, keepdims=True))
            o = _dot_nt(p.astype(BF16), jnp.where(in_head_v, mv_t, 0.0).astype(BF16))
            o = o / jnp.sum(p, axis=-1, keepdims=True)
            out = o if out is None else out + o
        ym_ref[:, cols] = out.astype(ym_ref.dtype)


def _mix_prompt(u, qm, mk_t, mv_t, wbd, scale, batch, seq, tm):
    halo_rows = 16
    assert halo_rows > POOL_BUF and tm % halo_rows == 0 and seq % tm == 0
    tiles = seq // tm
    mem_len = mk_t.shape[2]

    def rows(width):
        return pl.BlockSpec((tm, width), lambda b, t: (b * tiles + t, 0))

    halo = pl.BlockSpec((halo_rows, POOL_WIDTH),
                        lambda b, t: (jnp.maximum((b * tiles + t) * (tm // halo_rows) - 1, 0), 0))
    mem = pl.BlockSpec((None, MEM_WIDTH, mem_len), lambda b, t: (b, 0, 0))
    return pl.pallas_call(
        _mix_prompt_kernel,
        grid=(batch, tiles),
        in_specs=[rows(POOL_WIDTH), halo, rows(MEM_WIDTH), mem, mem, _const_spec(wbd.shape),
                  _const_spec((1, POOL_WIDTH))],
        out_specs=[rows(POOL_WIDTH), rows(MEM_WIDTH)],
        out_shape=[jax.ShapeDtypeStruct((batch * seq, POOL_WIDTH), BF16),
                   jax.ShapeDtypeStruct((batch * seq, MEM_WIDTH), BF16)],
        compiler_params=pltpu.CompilerParams(dimension_semantics=("parallel", "parallel")),
        name="mix_prompt",
    )(u, u, qm, mk_t, mv_t, wbd, scale)


def _mix_sample_kernel(u_ref, st_ref, qm_ref, mk_ref, mv_ref, wbd_ref, scale_ref, yp_ref, ym_ref, *, pos):
    rows = u_ref.shape[0]
    u = u_ref[...]
    tail = u
    sums = {}
    for back in range(1, POOL_BUF + 1):
        tail = tail + st_ref[POOL_BUF - back]
        if back + 1 in POOL_WINDOWS:
            sums[back + 1] = tail / float(min(back + 1, pos + 1))
    lane = lax.broadcasted_iota(jnp.int32, u.shape, 1)
    pooled = _pool_select(lane, *(sums[w] for w in POOL_WINDOWS)) - u
    yp_ref[...] = (_dot(pooled.astype(BF16), wbd_ref[...]) * scale_ref[...]).astype(yp_ref.dtype)

    row_t = lax.broadcasted_iota(jnp.int32, (SCORE_ROWS, MEM_WIDTH), 0)
    own_head = lax.broadcasted_iota(jnp.int32, (SCORE_ROWS, MEM_WIDTH), 1) // HEAD_DIM == row_t
    for r in range(rows):
        q = jnp.broadcast_to(qm_ref[r:r + 1, :], (SCORE_ROWS, MEM_WIDTH))
        q_t = jnp.where(own_head, q, 0.0).astype(BF16)
        lg = _dot(q_t, mk_ref[r].astype(BF16))
        p = jnp.exp(lg - jnp.max(lg, axis=-1, keepdims=True))
        p = p / jnp.sum(p, axis=-1, keepdims=True)
        o = _dot_nt(p.astype(BF16), mv_ref[r].astype(BF16))
        ym_ref[r:r + 1, :] = jnp.sum(jnp.where(own_head, o, 0.0), axis=0, keepdims=True).astype(ym_ref.dtype)


def _mix_sample(u, state_t, qm, mk_t, mv_t, wbd, scale, rows_per_step, pos):
    n = u.shape[0]
    mem_len = mk_t.shape[2]

    def rows(width):
        return pl.BlockSpec((rows_per_step, width), lambda i: (i, 0))

    mem = pl.BlockSpec((rows_per_step, MEM_WIDTH, mem_len), lambda i: (i, 0, 0))
    return pl.pallas_call(
        functools.partial(_mix_sample_kernel, pos=pos),
        grid=(n // rows_per_step,),
        in_specs=[rows(POOL_WIDTH), pl.BlockSpec((POOL_BUF, rows_per_step, POOL_WIDTH), lambda i: (0, i, 0)),
                  rows(MEM_WIDTH), mem, mem, _const_spec(wbd.shape), _const_spec((1, POOL_WIDTH))],
        out_specs=[rows(POOL_WIDTH), rows(MEM_WIDTH)],
        out_shape=[jax.ShapeDtypeStruct((n, POOL_WIDTH), BF16), jax.ShapeDtypeStruct((n, MEM_WIDTH), BF16)],
        compiler_params=pltpu.CompilerParams(dimension_semantics=("parallel",)),
        name="mix_sample",
    )(u, state_t, qm, mk_t, mv_t, wbd, scale)


def _rope_angles(pos):
    inv = jnp.power(jnp.float32(ROPE_THETA), -jnp.arange(ROT_HALF, dtype=F32) * (2.0 / ROT_DIM))
    return pos.astype(F32)[:, None] * inv[None, :]


def _rope_tables(pos):
    ang = _rope_angles(pos)
    cos, sin = jnp.cos(ang), jnp.sin(ang)
    rest = HEAD_DIM - ROT_DIM
    ones = jnp.ones((pos.shape[0], rest), F32)
    zeros = jnp.zeros((pos.shape[0], rest), F32)
    cos_h = jnp.concatenate([cos, cos, ones], axis=1)
    sin_h = jnp.concatenate([-sin, sin, zeros], axis=1)
    return jnp.tile(cos_h, (1, HEADS_PER_VREG)), jnp.tile(sin_h, (1, HEADS_PER_VREG))


def _block_diag_ones(width):
    head = jnp.arange(width) // HEAD_DIM
    return (head[:, None] == head[None, :]).astype(BF16)


def _tile_gain(g, width):
    return jnp.tile(g, width // HEAD_DIM).reshape(1, width)


def _gain_table(g, width):
    return jnp.broadcast_to(jnp.tile(g, width // HEAD_DIM)[:, None], (width, LANES))


def _feature_major_to_heads(a_t, n_heads):
    batch, _, tokens = a_t.shape
    return jnp.transpose(a_t.reshape(batch, n_heads, HEAD_DIM, tokens), (0, 3, 1, 2))[None]


def kernel(x_prompt, x_sample, mem_prompt, cache_k, cache_v, cache_mem_k, cache_mem_v, state_pool, page_table,
           ffn1_norm, ffn1_w_in, ffn1_w_out, mix_norm, w_in, pool_w, pool_scale, q_norm, k_norm,
           mem_norm, mem_w_kv, mem_q_norm, mem_k_norm, w_out, ffn2_norm, ffn2_w_in, ffn2_w_out):
    depth = ffn1_norm.shape[0]
    assert depth == 1
    n_p, t_p, _ = x_prompt.shape
    n_s, t_s, _ = x_sample.shape
    assert t_s == 1
    past_len = page_table.shape[1] * PAGE_SIZE
    mem_len = mem_prompt.shape[1]
    l = 0

    wgu1, wo1 = ffn1_w_in[l].astype(BF16), ffn1_w_out[l].astype(BF16)
    wgu2, wo2 = ffn2_w_in[l].astype(BF16), ffn2_w_out[l].astype(BF16)
    win, wout = w_in[l].astype(BF16), w_out[l].astype(BF16)
    c1 = POOL_WIDTH
    c4 = c1 + 3 * MOBA_WIDTH
    wnat = jnp.concatenate([win[:, :c1], win[:, c4:]], axis=1)
    wqkv_t = win[:, c1:c4].T
    wkv_t = mem_w_kv[l].astype(BF16).T
    n1, nmix, n2, nmem = (a[l].reshape(1, D_MODEL) for a in (ffn1_norm, mix_norm, ffn2_norm, mem_norm))
    qg, kg = _tile_gain(q_norm[l], MOBA_WIDTH), _tile_gain(k_norm[l], MOBA_WIDTH)
    mqg = _tile_gain(mem_q_norm[l], MEM_WIDTH)
    qg_tab, kg_tab = _gain_table(q_norm[l], MOBA_WIDTH), _gain_table(k_norm[l], MOBA_WIDTH)
    mkg_tab = _gain_table(mem_k_norm[l], MEM_WIDTH)
    ones_bd = _block_diag_ones(MOBA_WIDTH)
    ones_mem = ones_bd[:MEM_WIDTH, :MEM_WIDTH]
    wbd = jax.scipy.linalg.block_diag(*[pool_w[l, g] for g in range(len(POOL_WINDOWS))]).astype(BF16)
    scale = pool_scale[l].reshape(1, POOL_WIDTH)

    xp = x_prompt.reshape(n_p * t_p, D_MODEL)
    ang_p = _rope_angles(jnp.arange(t_p, dtype=jnp.int32)).T
    x1p, up, qmp, qt_p, kt_p, vt_p = _ffn1_inproj_t(xp, n_p, t_p, ROW_TILE, n1, wgu1, wo1, nmix, wnat, wqkv_t,
                                                   qg_tab, kg_tab, mqg, ones_mem, jnp.cos(ang_p), jnp.sin(ang_p))
    mkt_p, mvt_p = _mem_kv(mem_prompt.reshape(n_p * mem_len, D_MODEL), n_p, mem_len, nmem, wkv_t, mkg_tab)
    ya_p = _moba_prompt(qt_p, kt_p, vt_p)
    yp_p, ym_p = _mix_prompt(up, qmp, mkt_p, mvt_p, wbd, scale, n_p, t_p, ROW_TILE)
    y_prompt = _outproj_ffn2(x1p, yp_p, ya_p, ym_p, ROW_TILE, wout, n2, wgu2, wo2)

    xs = x_sample.reshape(n_s, D_MODEL)
    cos_s, sin_s = _rope_tables(jnp.full((n_s,), past_len, jnp.int32))
    x1s, us, qs, ks, vs, qms = _ffn1_inproj(xs, n1, wgu1, wo1, nmix, win, qg, kg, mqg, ones_bd, cos_s, sin_s)

    def feature_major(cache, n_heads):
        rows, tokens = cache.shape[1], cache.shape[2]
        return jnp.transpose(cache[l], (0, 2, 3, 1)).reshape(rows, n_heads * HEAD_DIM, tokens)

    ya_s = _moba_sample(page_table, qs, ks, vs, feature_major(cache_k, MOBA_HEADS),
                        feature_major(cache_v, MOBA_HEADS)).reshape(n_s, MOBA_WIDTH)
    state = state_pool[l]
    yp_s, ym_s = _mix_sample(us, jnp.swapaxes(state, 0, 1), qms, feature_major(cache_mem_k, MEM_HEADS),
                             feature_major(cache_mem_v, MEM_HEADS), wbd, scale, 8, past_len)
    y_sample = _outproj_ffn2(x1s, yp_s, ya_s, ym_s, n_s, wout, n2, wgu2, wo2)

    pool_state_p = up.reshape(n_p, t_p, POOL_WIDTH)[:, t_p - POOL_BUF:][None]
    pool_state_s = jnp.concatenate([state[:, 1:], us[:, None, :]], axis=1)[None]
    return (y_prompt.reshape(n_p, t_p, D_MODEL), y_sample.reshape(n_s, t_s, D_MODEL),
            _feature_major_to_heads(kt_p, MOBA_HEADS), _feature_major_to_heads(vt_p, MOBA_HEADS),
            _feature_major_to_heads(mkt_p, MEM_HEADS), _feature_major_to_heads(mvt_p, MEM_HEADS),
            pool_state_p,
            ks.reshape(1, n_s, t_s, MOBA_HEADS, HEAD_DIM), vs.reshape(1, n_s, t_s, MOBA_HEADS, HEAD_DIM),
            pool_state_s)
```

```python
import functools

import jax
import jax.numpy as jnp
from jax import lax
from jax.experimental import pallas as pl
from jax.experimental.pallas import tpu as pltpu

F32 = jnp.float32
BF16 = jnp.bfloat16

D_MODEL = 1024
HEAD_DIM = 64
POOL_WIDTH = 256
POOL_WINDOWS = (2, 4, 8, 16)
POOL_GROUP = POOL_WIDTH // len(POOL_WINDOWS)
POOL_BUF = max(POOL_WINDOWS) - 1
MOBA_WIDTH = 512
MOBA_HEADS = MOBA_WIDTH // HEAD_DIM
MOBA_BLOCK = 256
MOBA_TOPK = 3
MEM_WIDTH = 256
MEM_HEADS = MEM_WIDTH // HEAD_DIM
ROT_DIM = HEAD_DIM // 4
ROT_HALF = ROT_DIM // 2
ROPE_THETA = 500000.0
D_FF = 2816
RMS_EPS = 1e-6
PAGE_SIZE = 128
ATTN_SCALE = HEAD_DIM ** -0.5
LOG2_E = 1.4426950408889634

LANES = 128
SUBLANES = 8
V7X_VMEM_BYTES = 64 * 1024 * 1024

HEADS_PER_VREG = LANES // HEAD_DIM
MASKED = -1e30
FF_CHUNK = 256
FF_CHUNKS = D_FF // FF_CHUNK
ROW_TILE = 512
SCORE_ROWS = 16


def _dot(a, b):
    return jnp.dot(a, b, preferred_element_type=F32)


def _dot_nt(a, b):
    return lax.dot_general(a, b, (((1,), (1,)), ((), ())), preferred_element_type=F32)


def _split_bf16(x):
    hi = x.astype(BF16)
    lo = (x - hi.astype(F32)).astype(BF16)
    return hi, lo


def _vmem_limit(resident_bytes):
    return int(min(resident_bytes + 16 * 1024 * 1024, V7X_VMEM_BYTES - 4 * 1024 * 1024))


def _rmsnorm(x, g):
    ms = jnp.mean(x * x, axis=-1, keepdims=True)
    return x * lax.rsqrt(ms + RMS_EPS) * g


def _head_rmsnorm(x, g, ones_bd):
    hi, lo = _split_bf16(x * x)
    ms = (_dot(hi, ones_bd) + _dot(lo, ones_bd)) * (1.0 / HEAD_DIM)
    return x * lax.rsqrt(ms + RMS_EPS) * g


def _head_rmsnorm_t(x, g):
    heads = []
    for r in range(0, x.shape[0], HEAD_DIM):
        xh = x[r:r + HEAD_DIM]
        ms = jnp.mean(xh * xh, axis=0, keepdims=True)
        heads.append(xh * lax.rsqrt(ms + RMS_EPS))
    return jnp.concatenate(heads, axis=0) * g


def _rope(x, cos, sin):
    width = x.shape[1]
    lane = lax.broadcasted_iota(jnp.int32, x.shape, 1)
    first_half = (lane & (HEAD_DIM - 1)) < ROT_HALF
    partner = jnp.where(first_half, pltpu.roll(x, width - ROT_HALF, 1), pltpu.roll(x, ROT_HALF, 1))
    return x * cos + partner * sin


def _rope_t(x, cos, sin):
    pieces = []
    for r in range(0, x.shape[0], HEAD_DIM):
        x1, x2 = x[r:r + ROT_HALF], x[r + ROT_HALF:r + ROT_DIM]
        pieces += [x1 * cos - x2 * sin, x2 * cos + x1 * sin, x[r + ROT_DIM:r + HEAD_DIM]]
    return jnp.concatenate(pieces, axis=0)


def _lane_tile(table, n):
    return jnp.concatenate([table] * (n // LANES), axis=1)


def _swiglu(h_s, wgu_ref, wo_ref, act_s, side_work=None):
    for c in range(FF_CHUNKS):
        lo = c * FF_CHUNK
        gate = _dot(h_s[...], wgu_ref[:, lo:lo + FF_CHUNK])
        up = _dot(h_s[...], wgu_ref[:, D_FF + lo:D_FF + lo + FF_CHUNK])
        act = jax.nn.silu(gate) * up
        act_s[:, lo:lo + FF_CHUNK] = act.astype(BF16)
        if side_work is not None:
            side_work(c, act)
    return _dot(act_s[...], wo_ref[...])


def _const_spec(shape):
    return pl.BlockSpec(shape, lambda *_: (0,) * len(shape), pipeline_mode=pl.Buffered(1))


def _ffn1_trunk(x_ref, n1_ref, wgu_ref, wo_ref, nmix_ref, x1_ref, h_s, act_s):
    x = x_ref[...]
    h_s[...] = _rmsnorm(x, n1_ref[...]).astype(BF16)
    x1 = x + 0.5 * _swiglu(h_s, wgu_ref, wo_ref, act_s)
    x1_ref[...] = x1
    h_s[...] = _rmsnorm(x1, nmix_ref[...]).astype(BF16)


def _ffn1_inproj_kernel(x_ref, n1_ref, wgu_ref, wo_ref, nmix_ref, win_ref, qg_ref, kg_ref, mg_ref,
                        ones_ref, cos_ref, sin_ref,
                        x1_ref, u_ref, q_ref, k_ref, v_ref, qm_ref, h_s, act_s):
    _ffn1_trunk(x_ref, n1_ref, wgu_ref, wo_ref, nmix_ref, x1_ref, h_s, act_s)
    c1 = POOL_WIDTH
    c2 = c1 + MOBA_WIDTH
    c3 = c2 + MOBA_WIDTH
    c4 = c3 + MOBA_WIDTH
    u_ref[...] = _dot(h_s[...], win_ref[:, :c1])
    q = _dot(h_s[...], win_ref[:, c1:c2])
    k = _dot(h_s[...], win_ref[:, c2:c3])
    v_ref[...] = _dot(h_s[...], win_ref[:, c3:c4])
    qm = _dot(h_s[...], win_ref[:, c4:])
    reps = MOBA_WIDTH // LANES
    cos = jnp.concatenate([cos_ref[...]] * reps, axis=1)
    sin = jnp.concatenate([sin_ref[...]] * reps, axis=1)
    ones = ones_ref[...]
    q_ref[...] = _rope(_head_rmsnorm(q, qg_ref[...], ones), cos, sin) * ATTN_SCALE
    k_ref[...] = _rope(_head_rmsnorm(k, kg_ref[...], ones), cos, sin)
    qm_ref[...] = _head_rmsnorm(qm, mg_ref[...], ones_ref[:MEM_WIDTH, :MEM_WIDTH]) * ATTN_SCALE


def _ffn1_inproj(x, n1, wgu, wo, nmix, win, qg, kg, mg, ones_bd, cos_tab, sin_tab):
    n = x.shape[0]

    def full(width):
        return pl.BlockSpec((n, width), lambda i: (0, 0))

    widths = (D_MODEL, POOL_WIDTH, MOBA_WIDTH, MOBA_WIDTH, MOBA_WIDTH, MEM_WIDTH)
    resident = 2 * (wgu.size + wo.size + win.size + ones_bd.size)
    return pl.pallas_call(
        _ffn1_inproj_kernel,
        grid=(1,),
        in_specs=[full(D_MODEL), _const_spec((1, D_MODEL)), _const_spec(wgu.shape), _const_spec(wo.shape),
                  _const_spec((1, D_MODEL)), _const_spec(win.shape),
                  _const_spec((1, MOBA_WIDTH)), _const_spec((1, MOBA_WIDTH)), _const_spec((1, MEM_WIDTH)),
                  _const_spec(ones_bd.shape), full(LANES), full(LANES)],
        out_specs=[full(w) for w in widths],
        out_shape=[jax.ShapeDtypeStruct((n, w), F32) for w in widths],
        scratch_shapes=[pltpu.VMEM((n, D_MODEL), BF16), pltpu.VMEM((n, D_FF), BF16)],
        compiler_params=pltpu.CompilerParams(dimension_semantics=("arbitrary",),
                                             vmem_limit_bytes=_vmem_limit(resident)),
        name="ffn1_inproj_sample",
    )(x, n1, wgu, wo, nmix, win, qg, kg, mg, ones_bd, cos_tab, sin_tab)


def _ffn1_inproj_t_kernel(x_ref, n1_ref, wgu_ref, wo_ref, nmix_ref, wnat_ref, wqkv_t_ref, qg_ref, kg_ref, mg_ref,
                          ones_ref, cos_ref, sin_ref,
                          x1_ref, u_ref, qm_ref, qt_ref, kt_ref, vt_ref, h_s, act_s):
    _ffn1_trunk(x_ref, n1_ref, wgu_ref, wo_ref, nmix_ref, x1_ref, h_s, act_s)
    tm = x_ref.shape[0]
    u_ref[...] = _dot(h_s[...], wnat_ref[:, :POOL_WIDTH])
    qm = _dot(h_s[...], wnat_ref[:, POOL_WIDTH:])
    qm_ref[...] = _head_rmsnorm(qm, mg_ref[...], ones_ref[...]) * ATTN_SCALE
    cos, sin = cos_ref[...], sin_ref[...]
    q_t = _dot_nt(wqkv_t_ref[:MOBA_WIDTH, :], h_s[...])
    qt_ref[...] = _rope_t(_head_rmsnorm_t(q_t, _lane_tile(qg_ref[...], tm)), cos, sin) * (ATTN_SCALE * LOG2_E)
    k_t = _dot_nt(wqkv_t_ref[MOBA_WIDTH:2 * MOBA_WIDTH, :], h_s[...])
    kt_ref[...] = _rope_t(_head_rmsnorm_t(k_t, _lane_tile(kg_ref[...], tm)), cos, sin)
    vt_ref[...] = _dot_nt(wqkv_t_ref[2 * MOBA_WIDTH:, :], h_s[...])


def _ffn1_inproj_t(x, batch, seq, tm, n1, wgu, wo, nmix, wnat, wqkv_t, qg_tab, kg_tab, mg, ones_bd, cos_t, sin_t):
    tiles = seq // tm

    def rows(width):
        return pl.BlockSpec((tm, width), lambda i: (i, 0))

    feat = pl.BlockSpec((None, MOBA_WIDTH, tm), lambda i: (i // tiles, 0, i % tiles))
    tab = pl.BlockSpec((ROT_HALF, tm), lambda i: (0, i % tiles))
    resident = 2 * (wgu.size + wo.size + wnat.size + wqkv_t.size + ones_bd.size) + tm * (D_MODEL + D_FF) * 2
    streamed = 2 * 4 * tm * (2 * D_MODEL + POOL_WIDTH + MEM_WIDTH + 3 * MOBA_WIDTH + 2 * ROT_HALF)
    feat_shape = jax.ShapeDtypeStruct((batch, MOBA_WIDTH, seq), F32)
    return pl.pallas_call(
        _ffn1_inproj_t_kernel,
        grid=(batch * tiles,),
        in_specs=[rows(D_MODEL), _const_spec((1, D_MODEL)), _const_spec(wgu.shape), _const_spec(wo.shape),
                  _const_spec((1, D_MODEL)), _const_spec(wnat.shape), _const_spec(wqkv_t.shape),
                  _const_spec(qg_tab.shape), _const_spec(kg_tab.shape), _const_spec((1, MEM_WIDTH)),
                  _const_spec(ones_bd.shape), tab, tab],
        out_specs=[rows(D_MODEL), rows(POOL_WIDTH), rows(MEM_WIDTH), feat, feat, feat],
        out_shape=[jax.ShapeDtypeStruct((batch * seq, w), F32) for w in (D_MODEL, POOL_WIDTH, MEM_WIDTH)]
        + [feat_shape] * 3,
        scratch_shapes=[pltpu.VMEM((tm, D_MODEL), BF16), pltpu.VMEM((tm, D_FF), BF16)],
        compiler_params=pltpu.CompilerParams(dimension_semantics=("parallel",),
                                             vmem_limit_bytes=_vmem_limit(resident + streamed)),
        name="ffn1_inproj",
    )(x, n1, wgu, wo, nmix, wnat, wqkv_t, qg_tab, kg_tab, mg, ones_bd, cos_t, sin_t)


def _outproj_ffn2_kernel(x1_ref, yp_ref, ya_ref, ym_ref, wout_ref, n2_ref, wgu_ref, wo_ref, y_ref, h_s, act_s):
    c1 = POOL_WIDTH
    c2 = c1 + MOBA_WIDTH
    mix = (_dot(yp_ref[...], wout_ref[:c1, :]) + _dot(ya_ref[...], wout_ref[c1:c2, :])
           + _dot(ym_ref[...], wout_ref[c2:, :]))
    x2 = x1_ref[...] + mix
    h_s[...] = _rmsnorm(x2, n2_ref[...]).astype(BF16)
    y_ref[...] = x2 + 0.5 * _swiglu(h_s, wgu_ref, wo_ref, act_s)


def _outproj_ffn2(x1, y_pool, y_moba, y_mem, tm, wout, n2, wgu, wo):
    n = x1.shape[0]

    def rows(width):
        return pl.BlockSpec((tm, width), lambda i: (i, 0))

    resident = 2 * (wgu.size + wo.size + wout.size) + tm * (D_MODEL + D_FF) * 2
    streamed = 2 * tm * (4 * 2 * D_MODEL + 2 * D_MODEL)
    return pl.pallas_call(
        _outproj_ffn2_kernel,
        grid=(n // tm,),
        in_specs=[rows(D_MODEL), rows(POOL_WIDTH), rows(MOBA_WIDTH), rows(MEM_WIDTH),
                  _const_spec(wout.shape), _const_spec((1, D_MODEL)), _const_spec(wgu.shape), _const_spec(wo.shape)],
        out_specs=rows(D_MODEL),
        out_shape=jax.ShapeDtypeStruct((n, D_MODEL), F32),
        scratch_shapes=[pltpu.VMEM((tm, D_MODEL), BF16), pltpu.VMEM((tm, D_FF), BF16)],
        compiler_params=pltpu.CompilerParams(dimension_semantics=("parallel",),
                                             vmem_limit_bytes=_vmem_limit(resident + streamed)),
        name="outproj_ffn2",
    )(x1, y_pool, y_moba, y_mem, wout, n2, wgu, wo)


def _moba_prompt_kernel(qt_ref, kt_ref, vt_ref, o_ref, kb_s, vb_s, km_s, s_s, p_s):
    seq = kt_ref.shape[1]
    n_blocks = seq // MOBA_BLOCK
    blocks = [slice(n * MOBA_BLOCK, (n + 1) * MOBA_BLOCK) for n in range(n_blocks)]
    lane_k = lax.broadcasted_iota(jnp.int32, (MOBA_BLOCK, LANES), 1)
    km_s[...] = jnp.zeros_like(km_s)
    for n in range(n_blocks):
        k_n = kt_ref[:, blocks[n]].T
        km_s[n:n + 1, :] = jnp.mean(k_n, axis=0, keepdims=True)
        for h in range(HEADS_PER_VREG):
            own = (lane_k < HEAD_DIM) if h == 0 else (lane_k >= HEAD_DIM)
            tag_lane = (1 - h) * HEAD_DIM + n
            kb_s[h, blocks[n], :] = jnp.where(own, k_n, jnp.where(lane_k == tag_lane, 1.0, 0.0)).astype(BF16)
    vb_s[...] = vt_ref[...].astype(BF16)

    lane_m = lax.broadcasted_iota(jnp.int32, km_s.shape, 1)
    kmean = km_s[...]
    km_split = [_split_bf16(jnp.where((lane_m < HEAD_DIM) if h == 0 else (lane_m >= HEAD_DIM), kmean, 0.0))
                for h in range(HEADS_PER_VREG)]
    blk = lax.broadcasted_iota(jnp.int32, (SUBLANES, MOBA_BLOCK), 0)
    key_id = lax.broadcasted_iota(jnp.int32, (MOBA_BLOCK, MOBA_BLOCK), 0)
    qry_id = lax.broadcasted_iota(jnp.int32, (MOBA_BLOCK, MOBA_BLOCK), 1)
    spare_rows = HEAD_DIM - SUBLANES

    def tile_rows(i, j):
        start = MOBA_BLOCK * (i * (i + 1) // 2 + j)
        return slice(start, start + MOBA_BLOCK)

    def query_operand(i, h):
        q_t = qt_ref[:, blocks[i]]
        if i == 0:
            bias = jnp.zeros((SUBLANES, MOBA_BLOCK), F32)
        else:
            q_hi, q_lo = _split_bf16(q_t)
            km_hi, km_lo = km_split[h]
            s = (_dot(km_hi, q_hi) + _dot(km_lo, q_hi) + _dot(km_hi, q_lo))[:SUBLANES]
            s = jnp.where(blk < i, s, -jnp.inf)
            rank = jnp.zeros(s.shape, jnp.int32)
            for m in range(i):
                sm = s[m:m + 1, :]
                beats = (sm > s) | ((sm == s) & (blk > m))
                rank = rank + beats.astype(jnp.int32)
            allowed = ((blk < i) & (rank < MOBA_TOPK)) | (blk == i)
            bias = jnp.where(allowed, 0.0, MASKED)
        zeros = jnp.zeros((spare_rows, MOBA_BLOCK), F32)
        if h == 0:
            return jnp.concatenate([q_t[:HEAD_DIM], bias, zeros], axis=0).astype(BF16)
        return jnp.concatenate([bias, zeros, q_t[HEAD_DIM:]], axis=0).astype(BF16)

    def logits_tile(i, h, j, q_aug):
        st = _dot(kb_s[h, blocks[j], :], q_aug)
        if j == i:
            st = jnp.where(key_id <= qry_id, st, MASKED)
        s_s[h, tile_rows(i, j), :] = st
        return jnp.max(st, axis=0, keepdims=True)

    def prob_tile(i, h, j, m_row):
        p = jnp.exp2(s_s[h, tile_rows(i, j), :] - m_row)
        p_s[h, tile_rows(i, j), :] = p.astype(BF16)
        return jnp.sum(p, axis=0, keepdims=True)

    def weighted_values(i, h, l_row):
        keys = slice(tile_rows(i, 0).start, tile_rows(i, i).stop)
        acc = _dot(vb_s[h * HEAD_DIM:(h + 1) * HEAD_DIM, :(i + 1) * MOBA_BLOCK], p_s[h, keys, :])
        return acc / l_row

    groups = [(i, h) for i in range(n_blocks) for h in range(HEADS_PER_VREG)]

    def combine(acc, new, op):
        return new if acc is None else op(acc, new)

    q_next = query_operand(*groups[0])
    m_next = None
    for j in range(groups[0][0] + 1):
        m_next = combine(m_next, logits_tile(*groups[0], j, q_next), jnp.maximum)
    outs = []
    for g, (i, h) in enumerate(groups):
        m_row, m_next, l_row = m_next, None, None
        nxt = groups[g + 1] if g + 1 < len(groups) else None
        if nxt is not None:
            q_next = query_operand(*nxt)
        for j in range(max(i + 1, nxt[0] + 1 if nxt is not None else 0)):
            if j <= i:
                l_row = combine(l_row, prob_tile(i, h, j, m_row), jnp.add)
            if nxt is not None and j <= nxt[0]:
                m_next = combine(m_next, logits_tile(*nxt, j, q_next), jnp.maximum)
        outs.append(weighted_values(i, h, l_row))
        if h == HEADS_PER_VREG - 1:
            o_ref[blocks[i], :] = jnp.concatenate(outs, axis=0).T.astype(o_ref.dtype)
            outs = []


def _moba_prompt(q_t, k_t, v_t):
    batch, _, seq = k_t.shape
    n_blocks = seq // MOBA_BLOCK
    pairs = MOBA_WIDTH // LANES
    assert n_blocks <= SUBLANES and n_blocks <= HEAD_DIM
    tile_rows = MOBA_BLOCK * (n_blocks * (n_blocks + 1) // 2)
    feat = pl.BlockSpec((None, LANES, seq), lambda b, p: (b, p, 0))
    return pl.pallas_call(
        _moba_prompt_kernel,
        grid=(batch, pairs),
        in_specs=[feat, feat, feat],
        out_specs=pl.BlockSpec((seq, LANES), lambda b, p: (b, p)),
        out_shape=jax.ShapeDtypeStruct((batch * seq, MOBA_WIDTH), BF16),
        scratch_shapes=[pltpu.VMEM((HEADS_PER_VREG, seq, LANES), BF16),
                        pltpu.VMEM((LANES, seq), BF16),
                        pltpu.VMEM((SCORE_ROWS, LANES), F32),
                        pltpu.VMEM((HEADS_PER_VREG, tile_rows, MOBA_BLOCK), F32),
                        pltpu.VMEM((HEADS_PER_VREG, tile_rows, MOBA_BLOCK), BF16)],
        compiler_params=pltpu.CompilerParams(
            dimension_semantics=("parallel", "parallel"),
            vmem_limit_bytes=_vmem_limit(HEADS_PER_VREG * tile_rows * MOBA_BLOCK * 6 + 8 * LANES * seq * 4)),
        name="moba_prompt",
    )(q_t, k_t, v_t)


def _score_pages(pages, qb_s, after):
    zero = pltpu.bitcast(lax.shift_right_logical(pltpu.bitcast(after, jnp.uint32), jnp.uint32(32)), F32)
    head_row = lax.broadcasted_iota(jnp.int32, (MOBA_HEADS, PAGE_SIZE), 0)
    tiles = [jnp.zeros((MOBA_HEADS, PAGE_SIZE), F32) for _ in pages]
    for h in range(MOBA_HEADS):
        rows = slice(h * HEAD_DIM, (h + 1) * HEAD_DIM)
        q_h = (qb_s[rows, :].reshape(HEAD_DIM // SUBLANES, SUBLANES, PAGE_SIZE) + zero).reshape(HEAD_DIM, PAGE_SIZE)
        for n, page in enumerate(pages):
            logit = jnp.sum(page[rows, :] * q_h, axis=0, keepdims=True)
            tiles[n] = jnp.where(head_row == h, logit, tiles[n])
    return tiles


def _select_blocks(lg, q_ref, kn_ref, psel_ref, ids_ref, pself_ref):
    n_pages = lg.shape[0]
    pages_per_block = MOBA_BLOCK // PAGE_SIZE
    n_blocks = n_pages // pages_per_block

    blk = lg.reshape(n_blocks, pages_per_block, MOBA_HEADS, PAGE_SIZE)
    blk_sum = blk[:, 0]
    for r in range(1, pages_per_block):
        blk_sum = blk_sum + blk[:, r]
    score_col = jnp.sum(blk_sum, axis=-1, keepdims=True) * (1.0 / MOBA_BLOCK)
    lane = lax.broadcasted_iota(jnp.int32, (MOBA_HEADS, LANES), 1)
    blk_of_lane = lane & (n_blocks - 1)
    score = jnp.zeros((MOBA_HEADS, LANES), F32)
    for n in range(n_blocks):
        score = jnp.where(blk_of_lane == n, score_col[n], score)
    rank = jnp.zeros(score.shape, jnp.int32)
    for k in range(1, n_blocks):
        other = pltpu.roll(score, k, 1)
        beats = (other > score) | ((other == score) & (blk_of_lane >= k))
        rank = rank + beats.astype(jnp.int32)
    bias = jnp.where(rank < MOBA_TOPK, 0.0, MASKED)

    q = q_ref[0]
    own_head = (lax.broadcasted_iota(jnp.int32, (MOBA_HEADS, MOBA_WIDTH), 1) // HEAD_DIM
                == lax.broadcasted_iota(jnp.int32, (MOBA_HEADS, MOBA_WIDTH), 0))
    lg_self = jnp.sum(jnp.where(own_head, q, 0.0) * kn_ref[0], axis=-1, keepdims=True)

    page_bias = [jnp.broadcast_to(bias[:, n:n + 1], (MOBA_HEADS, PAGE_SIZE)) for n in range(n_blocks)]
    lg = blk + jnp.stack(page_bias)[:, None]
    m = jnp.maximum(jnp.max(jnp.max(jnp.max(lg, axis=0), axis=0), axis=-1, keepdims=True), lg_self)
    p = jnp.exp(lg - m)
    p_self = jnp.exp(lg_self - m)
    denom = jnp.sum(jnp.sum(jnp.sum(p, axis=0), axis=0), axis=-1, keepdims=True) + p_self
    p = p / denom
    pself_ref[0] = jnp.broadcast_to(p_self / denom, (MOBA_HEADS, LANES))

    first_period = lane < n_blocks
    ids = jnp.zeros((MOBA_HEADS, LANES), F32)
    for t in range(MOBA_TOPK):
        is_t = jnp.where(rank == t, 1.0, 0.0)
        for r in range(pages_per_block):
            kept = jnp.zeros((MOBA_HEADS, PAGE_SIZE), F32)
            for n in range(n_blocks):
                kept = kept + is_t[:, n:n + 1] * p[n, r]
            psel_ref[0, t * pages_per_block + r] = kept
        blk_id = jnp.sum(jnp.where(first_period, is_t * blk_of_lane.astype(F32), 0.0), axis=-1, keepdims=True)
        ids = jnp.where(lane == t, blk_id, ids)
    ids_ref[0] = ids.astype(jnp.int32)


def _moba_sample_values_kernel(pt_ref, ids_ref, psel_ref, pself_ref, vn_ref, cv_ref, o_ref, vbuf, sem):
    b = pl.program_id(0)
    n_rows = pl.num_programs(0)
    pages_per_block = MOBA_BLOCK // PAGE_SIZE
    per_head = MOBA_TOPK * pages_per_block
    slot = lax.rem(b, 2)

    def chunk_copy(row, c, s):
        h, t, r = c // per_head, (c % per_head) // pages_per_block, c % pages_per_block
        page = pt_ref[row, ids_ref[row, h * MOBA_TOPK + t] * pages_per_block + r]
        return pltpu.make_async_copy(cv_ref.at[page, pl.ds(h * HEAD_DIM, HEAD_DIM), :], vbuf.at[s, c], sem.at[s])

    n_chunks = MOBA_HEADS * per_head

    @pl.when(b == 0)
    def _():
        for c in range(n_chunks):
            chunk_copy(0, c, 0).start()

    @pl.when(b + 1 < n_rows)
    def _():
        for c in range(n_chunks):
            chunk_copy(b + 1, c, 1 - slot).start()

    for c in range(n_chunks):
        chunk_copy(b, c, slot).wait()

    heads = []
    for h in range(MOBA_HEADS):
        acc = jnp.zeros((HEAD_DIM, PAGE_SIZE), F32)
        for tr in range(per_head):
            acc = acc + vbuf[slot, h * per_head + tr] * psel_ref[0, tr, h:h + 1, :]
        heads.append(acc)
    acc_hi, acc_lo = _split_bf16(jnp.concatenate(heads, axis=0))
    ones = jnp.ones((SUBLANES, PAGE_SIZE), BF16)
    past = (_dot_nt(ones, acc_hi) + _dot_nt(ones, acc_lo))[0:1]
    own_head = (lax.broadcasted_iota(jnp.int32, (MOBA_HEADS, MOBA_WIDTH), 1) // HEAD_DIM
                == lax.broadcasted_iota(jnp.int32, (MOBA_HEADS, MOBA_WIDTH), 0))
    p_self = jnp.sum(jnp.where(own_head, _lane_tile(pself_ref[0], MOBA_WIDTH), 0.0), axis=0, keepdims=True)
    o_ref[0] = (past + p_self * vn_ref[0]).astype(o_ref.dtype)


def _sample_specs(n_pages):
    pages_per_block = MOBA_BLOCK // PAGE_SIZE
    kept_pages = MOBA_TOPK * pages_per_block
    row_spec = pl.BlockSpec((1, 1, MOBA_WIDTH), lambda b, *_: (b, 0, 0))
    psel_spec = pl.BlockSpec((1, kept_pages, MOBA_HEADS, PAGE_SIZE), lambda b, *_: (b, 0, 0, 0))
    tile_spec = pl.BlockSpec((1, MOBA_HEADS, LANES), lambda b, *_: (b, 0, 0))
    return kept_pages, row_spec, psel_spec, tile_spec


def _outproj_ffn2_scores_kernel(pt_ref, x1_ref, yp_ref, ya_ref, ym_ref, wout_ref, n2_ref, wgu_ref, wo_ref,
                                qc_ref, q_ref, kn_ref, ck_ref,
                                y_ref, psel_ref, ids_ref, pself_ref,
                                h_s, act_s, kbuf_a, kbuf_b, qb_s, lg_s, sem):
    b = pl.program_id(0)
    n_rows = pl.num_programs(0)
    n_a, n_b = kbuf_a.shape[0], kbuf_b.shape[0]
    per_chunk = -(-(n_a + n_b) // FF_CHUNKS)
    switch_chunk = n_a // per_chunk
    assert n_a == switch_chunk * per_chunk

    def copy_a(row, j):
        return pltpu.make_async_copy(ck_ref.at[pt_ref[row, j]], kbuf_a.at[j], sem.at[0])

    def copy_b(row, j):
        return pltpu.make_async_copy(ck_ref.at[pt_ref[row, n_a + j]], kbuf_b.at[j], sem.at[1])

    @pl.when(b == 0)
    def _():
        for j in range(n_a):
            copy_a(0, j).start()

    for j in range(n_b):
        copy_b(b, j).start()
    qb_s[...] = jnp.broadcast_to(qc_ref[0], qb_s.shape)
    for j in range(n_a):
        copy_a(b, j).wait()

    def side_work(c, act):
        if c == switch_chunk:
            for j in range(n_b):
                copy_b(b, j).wait()

            @pl.when(b + 1 < n_rows)
            def _():
                for j in range(n_a):
                    copy_a(b + 1, j).start()
        page_ids = range(c * per_chunk, min((c + 1) * per_chunk, n_a + n_b))
        pages = [kbuf_a.at[j] if j < n_a else kbuf_b.at[j - n_a] for j in page_ids]
        for j, tile in zip(page_ids, _score_pages(pages, qb_s, act[:SUBLANES, :PAGE_SIZE])):
            lg_s[j] = tile

    c1 = POOL_WIDTH
    c2 = c1 + MOBA_WIDTH
    mix = (_dot(yp_ref[...], wout_ref[:c1, :]) + _dot(ya_ref[...], wout_ref[c1:c2, :])
           + _dot(ym_ref[...], wout_ref[c2:, :]))
    x2 = x1_ref[...] + mix
    h_s[...] = _rmsnorm(x2, n2_ref[...]).astype(BF16)
    y_ref[...] = x2 + 0.5 * _swiglu(h_s, wgu_ref, wo_ref, act_s, side_work)
    _select_blocks(lg_s[...], q_ref, kn_ref, psel_ref, ids_ref, pself_ref)


def _outproj_ffn2_with_scores(x1, y_pool, y_moba, y_mem, tm, wout, n2, wgu, wo, page_table, q, k_new, cache_k_t):
    n = x1.shape[0]
    n_rows, n_pages = page_table.shape
    assert n // tm == n_rows
    pages_per_block = MOBA_BLOCK // PAGE_SIZE
    assert n_pages % pages_per_block == 0
    n_blocks = n_pages // pages_per_block
    assert LANES % n_blocks == 0 and n_blocks & (n_blocks - 1) == 0 and n_blocks >= MOBA_TOPK
    per_chunk = -(-n_pages // FF_CHUNKS)
    n_a = per_chunk * (FF_CHUNKS // 2)
    kept_pages, row_spec, psel_spec, tile_spec = _sample_specs(n_pages)
    col_spec = pl.BlockSpec((1, MOBA_WIDTH, 1), lambda b, *_: (b, 0, 0))

    def rows(width):
        return pl.BlockSpec((tm, width), lambda i, *_: (i, 0))

    resident = (2 * (wgu.size + wo.size + wout.size) + tm * (D_MODEL + D_FF) * 2
                + (n_pages + 1) * MOBA_WIDTH * PAGE_SIZE * 4)
    streamed = 2 * tm * (4 * 2 * D_MODEL + 2 * D_MODEL)
    return pl.pallas_call(
        _outproj_ffn2_scores_kernel,
        grid_spec=pltpu.PrefetchScalarGridSpec(
            num_scalar_prefetch=1,
            grid=(n_rows,),
            in_specs=[rows(D_MODEL), rows(POOL_WIDTH), rows(MOBA_WIDTH), rows(MEM_WIDTH),
                      _const_spec(wout.shape), _const_spec((1, D_MODEL)), _const_spec(wgu.shape),
                      _const_spec(wo.shape), col_spec, row_spec, row_spec, pl.BlockSpec(memory_space=pl.ANY)],
            out_specs=[rows(D_MODEL), psel_spec, tile_spec, tile_spec],
            scratch_shapes=[pltpu.VMEM((tm, D_MODEL), BF16), pltpu.VMEM((tm, D_FF), BF16),
                            pltpu.VMEM((n_a, MOBA_WIDTH, PAGE_SIZE), F32),
                            pltpu.VMEM((n_pages - n_a, MOBA_WIDTH, PAGE_SIZE), F32),
                            pltpu.VMEM((MOBA_WIDTH, PAGE_SIZE), F32),
                            pltpu.VMEM((n_pages, MOBA_HEADS, PAGE_SIZE), F32),
                            pltpu.SemaphoreType.DMA((2,))]),
        out_shape=[jax.ShapeDtypeStruct((n, D_MODEL), F32),
                   jax.ShapeDtypeStruct((n_rows, kept_pages, MOBA_HEADS, PAGE_SIZE), F32),
                   jax.ShapeDtypeStruct((n_rows, MOBA_HEADS, LANES), jnp.int32),
                   jax.ShapeDtypeStruct((n_rows, MOBA_HEADS, LANES), F32)],
        compiler_params=pltpu.CompilerParams(dimension_semantics=("arbitrary",),
                                             vmem_limit_bytes=_vmem_limit(resident + streamed)),
        name="outproj_ffn2_scores",
    )(page_table, x1, y_pool, y_moba, y_mem, wout, n2, wgu, wo, q.reshape(n_rows, MOBA_WIDTH, 1),
      q.reshape(n_rows, 1, MOBA_WIDTH), k_new.reshape(n_rows, 1, MOBA_WIDTH), cache_k_t)


def _moba_sample_values(page_table, p_sel, ids, p_self, v_new, cache_v_t):
    n_rows, n_pages = page_table.shape
    kept_pages, row_spec, psel_spec, tile_spec = _sample_specs(n_pages)
    any_spec = pl.BlockSpec(memory_space=pl.ANY)
    kept_ids = ids[:, :, :MOBA_TOPK].reshape(n_rows, MOBA_HEADS * MOBA_TOPK)
    return pl.pallas_call(
        _moba_sample_values_kernel,
        grid_spec=pltpu.PrefetchScalarGridSpec(
            num_scalar_prefetch=2,
            grid=(n_rows,),
            in_specs=[psel_spec, tile_spec, row_spec, any_spec],
            out_specs=row_spec,
            scratch_shapes=[pltpu.VMEM((2, MOBA_HEADS * kept_pages, HEAD_DIM, PAGE_SIZE), F32),
                            pltpu.SemaphoreType.DMA((2,))]),
        out_shape=jax.ShapeDtypeStruct((n_rows, 1, MOBA_WIDTH), BF16),
        compiler_params=pltpu.CompilerParams(dimension_semantics=("arbitrary",)),
        name="moba_sample_values",
    )(page_table, kept_ids, p_sel, p_self, v_new.reshape(n_rows, 1, MOBA_WIDTH), cache_v_t)


def _mem_kv_kernel(mem_ref, g_ref, wt_ref, kg_ref, mk_ref, mv_ref):
    h = _rmsnorm(mem_ref[...], g_ref[...]).astype(BF16)
    kv_t = _dot_nt(wt_ref[...], h)
    mk_ref[...] = _head_rmsnorm_t(kv_t[:MEM_WIDTH], _lane_tile(kg_ref[...], h.shape[0]))
    mv_ref[...] = kv_t[MEM_WIDTH:]


def _mem_kv(mem, batch, mem_len, g, w_t, kg_tab):
    feat = pl.BlockSpec((None, MEM_WIDTH, mem_len), lambda b: (b, 0, 0))
    return pl.pallas_call(
        _mem_kv_kernel,
        grid=(batch,),
        in_specs=[pl.BlockSpec((mem_len, D_MODEL), lambda b: (b, 0)), _const_spec((1, D_MODEL)),
                  _const_spec(w_t.shape), _const_spec(kg_tab.shape)],
        out_specs=[feat, feat],
        out_shape=[jax.ShapeDtypeStruct((batch, MEM_WIDTH, mem_len), F32)] * 2,
        compiler_params=pltpu.CompilerParams(dimension_semantics=("parallel",)),
        name="mem_kv",
    )(mem, g, w_t, kg_tab)


def _pool_select(lane, a2, a4, a8, a16):
    return jnp.where(lane < POOL_GROUP, a2,
                     jnp.where(lane < 2 * POOL_GROUP, a4, jnp.where(lane < 3 * POOL_GROUP, a8, a16)))


def _mix_prompt_kernel(u_ref, halo_ref, qm_ref, mk_ref, mv_ref, wbd_ref, scale_ref, yp_ref, ym_ref):
    t = pl.program_id(1)
    tm = u_ref.shape[0]
    halo_rows = halo_ref.shape[0]
    u = u_ref[...]
    ext = jnp.concatenate([jnp.where(t > 0, halo_ref[...], 0.0), u], axis=0)
    a2 = ext + pltpu.roll(ext, 1, 0)
    a4 = a2 + pltpu.roll(a2, 2, 0)
    a8 = a4 + pltpu.roll(a4, 4, 0)
    a16 = a8 + pltpu.roll(a8, 8, 0)
    lane = lax.broadcasted_iota(jnp.int32, u.shape, 1)
    sums = _pool_select(lane, a2[halo_rows:], a4[halo_rows:], a8[halo_rows:], a16[halo_rows:])
    pos = t * tm + lax.broadcasted_iota(jnp.int32, u.shape, 0)
    window = _pool_select(lane, *POOL_WINDOWS)
    cnt = jnp.minimum(window, pos + 1).astype(F32)
    pooled = sums / cnt - u
    yp_ref[...] = (_dot(pooled.astype(BF16), wbd_ref[...]) * scale_ref[...]).astype(yp_ref.dtype)

    mem_len = mk_ref.shape[1]
    row_v = lax.broadcasted_iota(jnp.int32, (LANES, mem_len), 0)
    lane_q = lax.broadcasted_iota(jnp.int32, (tm, LANES), 1)
    for pair in range(MEM_WIDTH // LANES):
        cols = slice(pair * LANES, (pair + 1) * LANES)
        qp = qm_ref[:, cols]
        mk_t = mk_ref[cols, :].astype(BF16)
        mv_t = mv_ref[cols, :]
        out = None
        for h in range(HEADS_PER_VREG):
            in_head = (lane_q < HEAD_DIM) if h == 0 else (lane_q >= HEAD_DIM)
            in_head_v = (row_v < HEAD_DIM) if h == 0 else (row_v >= HEAD_DIM)
            lg = _dot(jnp.where(in_head, qp, 0.0).astype(BF16), mk_t)
            p = jnp.exp(lg - jnp.max(lg, axis=-1, keepdims=True))
            o = _dot_nt(p.astype(BF16), jnp.where(in_head_v, mv_t, 0.0).astype(BF16))
            o = o / jnp.sum(p, axis=-1, keepdims=True)
            out = o if out is None else out + o
        ym_ref[:, cols] = out.astype(ym_ref.dtype)


def _mix_prompt(u, qm, mk_t, mv_t, wbd, scale, batch, seq, tm):
    halo_rows = 16
    assert halo_rows > POOL_BUF and tm % halo_rows == 0 and seq % tm == 0
    tiles = seq // tm
    mem_len = mk_t.shape[2]

    def rows(width):
        return pl.BlockSpec((tm, width), lambda b, t: (b * tiles + t, 0))

    halo = pl.BlockSpec((halo_rows, POOL_WIDTH),
                        lambda b, t: (jnp.maximum((b * tiles + t) * (tm // halo_rows) - 1, 0), 0))
    mem = pl.BlockSpec((None, MEM_WIDTH, mem_len), lambda b, t: (b, 0, 0))
    return pl.pallas_call(
        _mix_prompt_kernel,
        grid=(batch, tiles),
        in_specs=[rows(POOL_WIDTH), halo, rows(MEM_WIDTH), mem, mem, _const_spec(wbd.shape),
                  _const_spec((1, POOL_WIDTH))],
        out_specs=[rows(POOL_WIDTH), rows(MEM_WIDTH)],
        out_shape=[jax.ShapeDtypeStruct((batch * seq, POOL_WIDTH), BF16),
                   jax.ShapeDtypeStruct((batch * seq, MEM_WIDTH), BF16)],
        compiler_params=pltpu.CompilerParams(dimension_semantics=("parallel", "parallel")),
        name="mix_prompt",
    )(u, u, qm, mk_t, mv_t, wbd, scale)


def _mix_sample_kernel(u_ref, st_ref, qm_ref, mk_ref, mv_ref, wbd_ref, scale_ref, yp_ref, ym_ref, *, pos):
    rows = u_ref.shape[0]
    u = u_ref[...]
    tail = u
    sums = {}
    for back in range(1, POOL_BUF + 1):
        tail = tail + st_ref[POOL_BUF - back]
        if back + 1 in POOL_WINDOWS:
            sums[back + 1] = tail / float(min(back + 1, pos + 1))
    lane = lax.broadcasted_iota(jnp.int32, u.shape, 1)
    pooled = _pool_select(lane, *(sums[w] for w in POOL_WINDOWS)) - u
    yp_ref[...] = (_dot(pooled.astype(BF16), wbd_ref[...]) * scale_ref[...]).astype(yp_ref.dtype)

    row_t = lax.broadcasted_iota(jnp.int32, (SCORE_ROWS, MEM_WIDTH), 0)
    own_head = lax.broadcasted_iota(jnp.int32, (SCORE_ROWS, MEM_WIDTH), 1) // HEAD_DIM == row_t
    for r in range(rows):
        q = jnp.broadcast_to(qm_ref[r:r + 1, :], (SCORE_ROWS, MEM_WIDTH))
        q_t = jnp.where(own_head, q, 0.0).astype(BF16)
        lg = _dot(q_t, mk_ref[r].astype(BF16))
        p = jnp.exp(lg - jnp.max(lg, axis=-1, keepdims=True))
        p = p / jnp.sum(p, axis=-1, keepdims=True)
        o = _dot_nt(p.astype(BF16), mv_ref[r].astype(BF16))
        ym_ref[r:r + 1, :] = jnp.sum(jnp.where(own_head, o, 0.0), axis=0, keepdims=True).astype(ym_ref.dtype)


def _mix_sample(u, state_t, qm, mk_t, mv_t, wbd, scale, rows_per_step, pos):
    n = u.shape[0]
    mem_len = mk_t.shape[2]

    def rows(width):
        return pl.BlockSpec((rows_per_step, width), lambda i: (i, 0))

    mem = pl.BlockSpec((rows_per_step, MEM_WIDTH, mem_len), lambda i: (i, 0, 0))
    return pl.pallas_call(
        functools.partial(_mix_sample_kernel, pos=pos),
        grid=(n // rows_per_step,),
        in_specs=[rows(POOL_WIDTH), pl.BlockSpec((POOL_BUF, rows_per_step, POOL_WIDTH), lambda i: (0, i, 0)),
                  rows(MEM_WIDTH), mem, mem, _const_spec(wbd.shape), _const_spec((1, POOL_WIDTH))],
        out_specs=[rows(POOL_WIDTH), rows(MEM_WIDTH)],
        out_shape=[jax.ShapeDtypeStruct((n, POOL_WIDTH), BF16), jax.ShapeDtypeStruct((n, MEM_WIDTH), BF16)],
        compiler_params=pltpu.CompilerParams(dimension_semantics=("parallel",)),
        name="mix_sample",
    )(u, state_t, qm, mk_t, mv_t, wbd, scale)


def _rope_angles(pos):
    inv = jnp.power(jnp.float32(ROPE_THETA), -jnp.arange(ROT_HALF, dtype=F32) * (2.0 / ROT_DIM))
    return pos.astype(F32)[:, None] * inv[None, :]


def _rope_tables(pos):
    ang = _rope_angles(pos)
    cos, sin = jnp.cos(ang), jnp.sin(ang)
    rest = HEAD_DIM - ROT_DIM
    ones = jnp.ones((pos.shape[0], rest), F32)
    zeros = jnp.zeros((pos.shape[0], rest), F32)
    cos_h = jnp.concatenate([cos, cos, ones], axis=1)
    sin_h = jnp.concatenate([-sin, sin, zeros], axis=1)
    return jnp.tile(cos_h, (1, HEADS_PER_VREG)), jnp.tile(sin_h, (1, HEADS_PER_VREG))


def _block_diag_ones(width):
    head = jnp.arange(width) // HEAD_DIM
    return (head[:, None] == head[None, :]).astype(BF16)


def _tile_gain(g, width):
    return jnp.tile(g, width // HEAD_DIM).reshape(1, width)


def _gain_table(g, width):
    return jnp.broadcast_to(jnp.tile(g, width // HEAD_DIM)[:, None], (width, LANES))


def _feature_major_to_heads(a_t, n_heads):
    batch, _, tokens = a_t.shape
    return jnp.transpose(a_t.reshape(batch, n_heads, HEAD_DIM, tokens), (0, 3, 1, 2))[None]


def kernel(x_prompt, x_sample, mem_prompt, cache_k, cache_v, cache_mem_k, cache_mem_v, state_pool, page_table,
           ffn1_norm, ffn1_w_in, ffn1_w_out, mix_norm, w_in, pool_w, pool_scale, q_norm, k_norm,
           mem_norm, mem_w_kv, mem_q_norm, mem_k_norm, w_out, ffn2_norm, ffn2_w_in, ffn2_w_out):
    depth = ffn1_norm.shape[0]
    assert depth == 1
    n_p, t_p, _ = x_prompt.shape
    n_s, t_s, _ = x_sample.shape
    assert t_s == 1
    past_len = page_table.shape[1] * PAGE_SIZE
    mem_len = mem_prompt.shape[1]
    l = 0

    wgu1, wo1 = ffn1_w_in[l].astype(BF16), ffn1_w_out[l].astype(BF16)
    wgu2, wo2 = ffn2_w_in[l].astype(BF16), ffn2_w_out[l].astype(BF16)
    win, wout = w_in[l].astype(BF16), w_out[l].astype(BF16)
    c1 = POOL_WIDTH
    c4 = c1 + 3 * MOBA_WIDTH
    wnat = jnp.concatenate([win[:, :c1], win[:, c4:]], axis=1)
    wqkv_t = win[:, c1:c4].T
    wkv_t = mem_w_kv[l].astype(BF16).T
    n1, nmix, n2, nmem = (a[l].reshape(1, D_MODEL) for a in (ffn1_norm, mix_norm, ffn2_norm, mem_norm))
    qg, kg = _tile_gain(q_norm[l], MOBA_WIDTH), _tile_gain(k_norm[l], MOBA_WIDTH)
    mqg = _tile_gain(mem_q_norm[l], MEM_WIDTH)
    qg_tab, kg_tab = _gain_table(q_norm[l], MOBA_WIDTH), _gain_table(k_norm[l], MOBA_WIDTH)
    mkg_tab = _gain_table(mem_k_norm[l], MEM_WIDTH)
    ones_bd = _block_diag_ones(MOBA_WIDTH)
    ones_mem = ones_bd[:MEM_WIDTH, :MEM_WIDTH]
    wbd = jax.scipy.linalg.block_diag(*[pool_w[l, g] for g in range(len(POOL_WINDOWS))]).astype(BF16)
    scale = pool_scale[l].reshape(1, POOL_WIDTH)

    xp = x_prompt.reshape(n_p * t_p, D_MODEL)
    ang_p = _rope_angles(jnp.arange(t_p, dtype=jnp.int32)).T
    x1p, up, qmp, qt_p, kt_p, vt_p = _ffn1_inproj_t(xp, n_p, t_p, ROW_TILE, n1, wgu1, wo1, nmix, wnat, wqkv_t,
                                                   qg_tab, kg_tab, mqg, ones_mem, jnp.cos(ang_p), jnp.sin(ang_p))
    mkt_p, mvt_p = _mem_kv(mem_prompt.reshape(n_p * mem_len, D_MODEL), n_p, mem_len, nmem, wkv_t, mkg_tab)
    ya_p = _moba_prompt(qt_p, kt_p, vt_p)
    yp_p, ym_p = _mix_prompt(up, qmp, mkt_p, mvt_p, wbd, scale, n_p, t_p, ROW_TILE)

    xs = x_sample.reshape(n_s, D_MODEL)
    cos_s, sin_s = _rope_tables(jnp.full((n_s,), past_len, jnp.int32))
    x1s, us, qs, ks, vs, qms = _ffn1_inproj(xs, n1, wgu1, wo1, nmix, win, qg, kg, mqg, ones_bd, cos_s, sin_s)

    def feature_major(cache, n_heads):
        rows, tokens = cache.shape[1], cache.shape[2]
        return jnp.transpose(cache[l], (0, 2, 3, 1)).reshape(rows, n_heads * HEAD_DIM, tokens)

    y_prompt, p_sel, kept_ids, p_self = _outproj_ffn2_with_scores(
        x1p, yp_p, ya_p, ym_p, ROW_TILE, wout, n2, wgu2, wo2, page_table, qs, ks, feature_major(cache_k, MOBA_HEADS))
    ya_s = _moba_sample_values(page_table, p_sel, kept_ids, p_self, vs,
                               feature_major(cache_v, MOBA_HEADS)).reshape(n_s, MOBA_WIDTH)
    state = state_pool[l]
    yp_s, ym_s = _mix_sample(us, jnp.swapaxes(state, 0, 1), qms, feature_major(cache_mem_k, MEM_HEADS),
                             feature_major(cache_mem_v, MEM_HEADS), wbd, scale, 8, past_len)
    y_sample = _outproj_ffn2(x1s, yp_s, ya_s, ym_s, n_s, wout, n2, wgu2, wo2)

    pool_state_p = up.reshape(n_p, t_p, POOL_WIDTH)[:, t_p - POOL_BUF:][None]
    pool_state_s = jnp.concatenate([state[:, 1:], us[:, None, :]], axis=1)[None]
    return (y_prompt.reshape(n_p, t_p, D_MODEL), y_sample.reshape(n_s, t_s, D_MODEL),
            _feature_major_to_heads(kt_p, MOBA_HEADS), _feature_major_to_heads(vt_p, MOBA_HEADS),
            _feature_major_to_heads(mkt_p, MEM_HEADS), _feature_major_to_heads(mvt_p, MEM_HEADS),
            pool_state_p,
            ks.reshape(1, n_s, t_s, MOBA_HEADS, HEAD_DIM), vs.reshape(1, n_s, t_s, MOBA_HEADS, HEAD_DIM),
            pool_state_s)
```

```python
import functools

import jax
import jax.numpy as jnp
from jax import lax
from jax.experimental import pallas as pl
from jax.experimental.pallas import tpu as pltpu

F32 = jnp.float32
BF16 = jnp.bfloat16

D_MODEL = 1024
HEAD_DIM = 64
POOL_WIDTH = 256
POOL_WINDOWS = (2, 4, 8, 16)
POOL_GROUP = POOL_WIDTH // len(POOL_WINDOWS)
POOL_BUF = max(POOL_WINDOWS) - 1
MOBA_WIDTH = 512
MOBA_HEADS = MOBA_WIDTH // HEAD_DIM
MOBA_BLOCK = 256
MOBA_TOPK = 3
MEM_WIDTH = 256
MEM_HEADS = MEM_WIDTH // HEAD_DIM
ROT_DIM = HEAD_DIM // 4
ROT_HALF = ROT_DIM // 2
ROPE_THETA = 500000.0
D_FF = 2816
RMS_EPS = 1e-6
PAGE_SIZE = 128
ATTN_SCALE = HEAD_DIM ** -0.5
LOG2_E = 1.4426950408889634

LANES = 128
SUBLANES = 8
V7X_VMEM_BYTES = 64 * 1024 * 1024

HEADS_PER_VREG = LANES // HEAD_DIM
MASKED = -1e30
FF_CHUNK = 256
FF_CHUNKS = D_FF // FF_CHUNK
ROW_TILE = 512
SCORE_ROWS = 16


def _dot(a, b):
    return jnp.dot(a, b, preferred_element_type=F32)


def _dot_nt(a, b):
    return lax.dot_general(a, b, (((1,), (1,)), ((), ())), preferred_element_type=F32)


def _split_bf16(x):
    hi = x.astype(BF16)
    lo = (x - hi.astype(F32)).astype(BF16)
    return hi, lo


def _vmem_limit(resident_bytes):
    return int(min(resident_bytes + 16 * 1024 * 1024, V7X_VMEM_BYTES - 4 * 1024 * 1024))


def _rmsnorm(x, g):
    ms = jnp.mean(x * x, axis=-1, keepdims=True)
    return x * lax.rsqrt(ms + RMS_EPS) * g


def _head_rmsnorm(x, g, ones_bd):
    hi, lo = _split_bf16(x * x)
    ms = (_dot(hi, ones_bd) + _dot(lo, ones_bd)) * (1.0 / HEAD_DIM)
    return x * lax.rsqrt(ms + RMS_EPS) * g


def _head_rmsnorm_t(x, g):
    heads = []
    for r in range(0, x.shape[0], HEAD_DIM):
        xh = x[r:r + HEAD_DIM]
        ms = jnp.mean(xh * xh, axis=0, keepdims=True)
        heads.append(xh * lax.rsqrt(ms + RMS_EPS))
    return jnp.concatenate(heads, axis=0) * g


def _rope(x, cos, sin):
    width = x.shape[1]
    lane = lax.broadcasted_iota(jnp.int32, x.shape, 1)
    first_half = (lane & (HEAD_DIM - 1)) < ROT_HALF
    partner = jnp.where(first_half, pltpu.roll(x, width - ROT_HALF, 1), pltpu.roll(x, ROT_HALF, 1))
    return x * cos + partner * sin


def _rope_t(x, cos, sin):
    pieces = []
    for r in range(0, x.shape[0], HEAD_DIM):
        x1, x2 = x[r:r + ROT_HALF], x[r + ROT_HALF:r + ROT_DIM]
        pieces += [x1 * cos - x2 * sin, x2 * cos + x1 * sin, x[r + ROT_DIM:r + HEAD_DIM]]
    return jnp.concatenate(pieces, axis=0)


def _lane_tile(table, n):
    return jnp.concatenate([table] * (n // LANES), axis=1)


def _swiglu(h_s, wgu_ref, wo_ref, act_s, side_work=None):
    for c in range(FF_CHUNKS):
        lo = c * FF_CHUNK
        gate = _dot(h_s[...], wgu_ref[:, lo:lo + FF_CHUNK])
        up = _dot(h_s[...], wgu_ref[:, D_FF + lo:D_FF + lo + FF_CHUNK])
        act = jax.nn.silu(gate) * up
        act_s[:, lo:lo + FF_CHUNK] = act.astype(BF16)
        if side_work is not None:
            side_work(c, act)
    return _dot(act_s[...], wo_ref[...])


def _const_spec(shape):
    return pl.BlockSpec(shape, lambda *_: (0,) * len(shape), pipeline_mode=pl.Buffered(1))


def _ffn1_trunk(x_ref, n1_ref, wgu_ref, wo_ref, nmix_ref, x1_ref, h_s, act_s):
    x = x_ref[...]
    h_s[...] = _rmsnorm(x, n1_ref[...]).astype(BF16)
    x1 = x + 0.5 * _swiglu(h_s, wgu_ref, wo_ref, act_s)
    x1_ref[...] = x1
    h_s[...] = _rmsnorm(x1, nmix_ref[...]).astype(BF16)


def _ffn1_inproj_kernel(x_ref, n1_ref, wgu_ref, wo_ref, nmix_ref, win_ref, qg_ref, kg_ref, mg_ref,
                        ones_ref, cos_ref, sin_ref,
                        x1_ref, u_ref, q_ref, k_ref, v_ref, qm_ref, h_s, act_s):
    _ffn1_trunk(x_ref, n1_ref, wgu_ref, wo_ref, nmix_ref, x1_ref, h_s, act_s)
    c1 = POOL_WIDTH
    c2 = c1 + MOBA_WIDTH
    c3 = c2 + MOBA_WIDTH
    c4 = c3 + MOBA_WIDTH
    u_ref[...] = _dot(h_s[...], win_ref[:, :c1])
    q = _dot(h_s[...], win_ref[:, c1:c2])
    k = _dot(h_s[...], win_ref[:, c2:c3])
    v_ref[...] = _dot(h_s[...], win_ref[:, c3:c4])
    qm = _dot(h_s[...], win_ref[:, c4:])
    reps = MOBA_WIDTH // LANES
    cos = jnp.concatenate([cos_ref[...]] * reps, axis=1)
    sin = jnp.concatenate([sin_ref[...]] * reps, axis=1)
    ones = ones_ref[...]
    q_ref[...] = _rope(_head_rmsnorm(q, qg_ref[...], ones), cos, sin) * ATTN_SCALE
    k_ref[...] = _rope(_head_rmsnorm(k, kg_ref[...], ones), cos, sin)
    qm_ref[...] = _head_rmsnorm(qm, mg_ref[...], ones_ref[:MEM_WIDTH, :MEM_WIDTH]) * ATTN_SCALE


def _ffn1_inproj(x, n1, wgu, wo, nmix, win, qg, kg, mg, ones_bd, cos_tab, sin_tab):
    n = x.shape[0]

    def full(width):
        return pl.BlockSpec((n, width), lambda i: (0, 0))

    widths = (D_MODEL, POOL_WIDTH, MOBA_WIDTH, MOBA_WIDTH, MOBA_WIDTH, MEM_WIDTH)
    resident = 2 * (wgu.size + wo.size + win.size + ones_bd.size)
    return pl.pallas_call(
        _ffn1_inproj_kernel,
        grid=(1,),
        in_specs=[full(D_MODEL), _const_spec((1, D_MODEL)), _const_spec(wgu.shape), _const_spec(wo.shape),
                  _const_spec((1, D_MODEL)), _const_spec(win.shape),
                  _const_spec((1, MOBA_WIDTH)), _const_spec((1, MOBA_WIDTH)), _const_spec((1, MEM_WIDTH)),
                  _const_spec(ones_bd.shape), full(LANES), full(LANES)],
        out_specs=[full(w) for w in widths],
        out_shape=[jax.ShapeDtypeStruct((n, w), F32) for w in widths],
        scratch_shapes=[pltpu.VMEM((n, D_MODEL), BF16), pltpu.VMEM((n, D_FF), BF16)],
        compiler_params=pltpu.CompilerParams(dimension_semantics=("arbitrary",),
                                             vmem_limit_bytes=_vmem_limit(resident)),
        name="ffn1_inproj_sample",
    )(x, n1, wgu, wo, nmix, win, qg, kg, mg, ones_bd, cos_tab, sin_tab)


def _ffn1_inproj_t_kernel(x_ref, n1_ref, wgu_ref, wo_ref, nmix_ref, wnat_ref, wqkv_t_ref, qg_ref, kg_ref, mg_ref,
                          ones_ref, cos_ref, sin_ref,
                          x1_ref, u_ref, qm_ref, qt_ref, kt_ref, vt_ref, h_s, act_s):
    _ffn1_trunk(x_ref, n1_ref, wgu_ref, wo_ref, nmix_ref, x1_ref, h_s, act_s)
    tm = x_ref.shape[0]
    u_ref[...] = _dot(h_s[...], wnat_ref[:, :POOL_WIDTH])
    qm = _dot(h_s[...], wnat_ref[:, POOL_WIDTH:])
    qm_ref[...] = _head_rmsnorm(qm, mg_ref[...], ones_ref[...]) * ATTN_SCALE
    cos, sin = cos_ref[...], sin_ref[...]
    q_t = _dot_nt(wqkv_t_ref[:MOBA_WIDTH, :], h_s[...])
    qt_ref[...] = _rope_t(_head_rmsnorm_t(q_t, _lane_tile(qg_ref[...], tm)), cos, sin) * (ATTN_SCALE * LOG2_E)
    k_t = _dot_nt(wqkv_t_ref[MOBA_WIDTH:2 * MOBA_WIDTH, :], h_s[...])
    kt_ref[...] = _rope_t(_head_rmsnorm_t(k_t, _lane_tile(kg_ref[...], tm)), cos, sin)
    vt_ref[...] = _dot_nt(wqkv_t_ref[2 * MOBA_WIDTH:, :], h_s[...])


def _ffn1_inproj_t(x, batch, seq, tm, n1, wgu, wo, nmix, wnat, wqkv_t, qg_tab, kg_tab, mg, ones_bd, cos_t, sin_t):
    tiles = seq // tm

    def rows(width):
        return pl.BlockSpec((tm, width), lambda i: (i, 0))

    feat = pl.BlockSpec((None, MOBA_WIDTH, tm), lambda i: (i // tiles, 0, i % tiles))
    tab = pl.BlockSpec((ROT_HALF, tm), lambda i: (0, i % tiles))
    resident = 2 * (wgu.size + wo.size + wnat.size + wqkv_t.size + ones_bd.size) + tm * (D_MODEL + D_FF) * 2
    streamed = 2 * 4 * tm * (2 * D_MODEL + POOL_WIDTH + MEM_WIDTH + 3 * MOBA_WIDTH + 2 * ROT_HALF)
    feat_shape = jax.ShapeDtypeStruct((batch, MOBA_WIDTH, seq), F32)
    return pl.pallas_call(
        _ffn1_inproj_t_kernel,
        grid=(batch * tiles,),
        in_specs=[rows(D_MODEL), _const_spec((1, D_MODEL)), _const_spec(wgu.shape), _const_spec(wo.shape),
                  _const_spec((1, D_MODEL)), _const_spec(wnat.shape), _const_spec(wqkv_t.shape),
                  _const_spec(qg_tab.shape), _const_spec(kg_tab.shape), _const_spec((1, MEM_WIDTH)),
                  _const_spec(ones_bd.shape), tab, tab],
        out_specs=[rows(D_MODEL), rows(POOL_WIDTH), rows(MEM_WIDTH), feat, feat, feat],
        out_shape=[jax.ShapeDtypeStruct((batch * seq, w), F32) for w in (D_MODEL, POOL_WIDTH, MEM_WIDTH)]
        + [feat_shape] * 3,
        scratch_shapes=[pltpu.VMEM((tm, D_MODEL), BF16), pltpu.VMEM((tm, D_FF), BF16)],
        compiler_params=pltpu.CompilerParams(dimension_semantics=("parallel",),
                                             vmem_limit_bytes=_vmem_limit(resident + streamed)),
        name="ffn1_inproj",
    )(x, n1, wgu, wo, nmix, wnat, wqkv_t, qg_tab, kg_tab, mg, ones_bd, cos_t, sin_t)


def _outproj_ffn2_kernel(x1_ref, yp_ref, ya_ref, ym_ref, wout_ref, n2_ref, wgu_ref, wo_ref, y_ref, h_s, act_s):
    c1 = POOL_WIDTH
    c2 = c1 + MOBA_WIDTH
    mix = (_dot(yp_ref[...], wout_ref[:c1, :]) + _dot(ya_ref[...], wout_ref[c1:c2, :])
           + _dot(ym_ref[...], wout_ref[c2:, :]))
    x2 = x1_ref[...] + mix
    h_s[...] = _rmsnorm(x2, n2_ref[...]).astype(BF16)
    y_ref[...] = x2 + 0.5 * _swiglu(h_s, wgu_ref, wo_ref, act_s)


def _outproj_ffn2(x1, y_pool, y_moba, y_mem, tm, wout, n2, wgu, wo):
    n = x1.shape[0]

    def rows(width):
        return pl.BlockSpec((tm, width), lambda i: (i, 0))

    resident = 2 * (wgu.size + wo.size + wout.size) + tm * (D_MODEL + D_FF) * 2
    streamed = 2 * tm * (4 * 2 * D_MODEL + 2 * D_MODEL)
    return pl.pallas_call(
        _outproj_ffn2_kernel,
        grid=(n // tm,),
        in_specs=[rows(D_MODEL), rows(POOL_WIDTH), rows(MOBA_WIDTH), rows(MEM_WIDTH),
                  _const_spec(wout.shape), _const_spec((1, D_MODEL)), _const_spec(wgu.shape), _const_spec(wo.shape)],
        out_specs=rows(D_MODEL),
        out_shape=jax.ShapeDtypeStruct((n, D_MODEL), F32),
        scratch_shapes=[pltpu.VMEM((tm, D_MODEL), BF16), pltpu.VMEM((tm, D_FF), BF16)],
        compiler_params=pltpu.CompilerParams(dimension_semantics=("parallel",),
                                             vmem_limit_bytes=_vmem_limit(resident + streamed)),
        name="outproj_ffn2",
    )(x1, y_pool, y_moba, y_mem, wout, n2, wgu, wo)


def _moba_prompt_kernel(qt_ref, kt_ref, vt_ref, o_ref, kb_s, vb_s, km_s, s_s, p_s):
    seq = kt_ref.shape[1]
    n_blocks = seq // MOBA_BLOCK
    blocks = [slice(n * MOBA_BLOCK, (n + 1) * MOBA_BLOCK) for n in range(n_blocks)]
    lane_k = lax.broadcasted_iota(jnp.int32, (MOBA_BLOCK, LANES), 1)
    km_s[...] = jnp.zeros_like(km_s)
    for n in range(n_blocks):
        k_n = kt_ref[:, blocks[n]].T
        km_s[n:n + 1, :] = jnp.mean(k_n, axis=0, keepdims=True)
        for h in range(HEADS_PER_VREG):
            own = (lane_k < HEAD_DIM) if h == 0 else (lane_k >= HEAD_DIM)
            tag_lane = (1 - h) * HEAD_DIM + n
            kb_s[h, blocks[n], :] = jnp.where(own, k_n, jnp.where(lane_k == tag_lane, 1.0, 0.0)).astype(BF16)
    vb_s[...] = vt_ref[...].astype(BF16)

    lane_m = lax.broadcasted_iota(jnp.int32, km_s.shape, 1)
    kmean = km_s[...]
    km_split = [_split_bf16(jnp.where((lane_m < HEAD_DIM) if h == 0 else (lane_m >= HEAD_DIM), kmean, 0.0))
                for h in range(HEADS_PER_VREG)]
    blk = lax.broadcasted_iota(jnp.int32, (SUBLANES, MOBA_BLOCK), 0)
    key_id = lax.broadcasted_iota(jnp.int32, (MOBA_BLOCK, MOBA_BLOCK), 0)
    qry_id = lax.broadcasted_iota(jnp.int32, (MOBA_BLOCK, MOBA_BLOCK), 1)
    spare_rows = HEAD_DIM - SUBLANES

    def tile_rows(i, j):
        start = MOBA_BLOCK * (i * (i + 1) // 2 + j)
        return slice(start, start + MOBA_BLOCK)

    def query_operand(i, h):
        q_t = qt_ref[:, blocks[i]]
        if i == 0:
            bias = jnp.zeros((SUBLANES, MOBA_BLOCK), F32)
        else:
            q_hi, q_lo = _split_bf16(q_t)
            km_hi, km_lo = km_split[h]
            s = (_dot(km_hi, q_hi) + _dot(km_lo, q_hi) + _dot(km_hi, q_lo))[:SUBLANES]
            s = jnp.where(blk < i, s, -jnp.inf)
            rank = jnp.zeros(s.shape, jnp.int32)
            for m in range(i):
                sm = s[m:m + 1, :]
                beats = (sm > s) | ((sm == s) & (blk > m))
                rank = rank + beats.astype(jnp.int32)
            allowed = ((blk < i) & (rank < MOBA_TOPK)) | (blk == i)
            bias = jnp.where(allowed, 0.0, MASKED)
        zeros = jnp.zeros((spare_rows, MOBA_BLOCK), F32)
        if h == 0:
            return jnp.concatenate([q_t[:HEAD_DIM], bias, zeros], axis=0).astype(BF16)
        return jnp.concatenate([bias, zeros, q_t[HEAD_DIM:]], axis=0).astype(BF16)

    def logits_tile(i, h, j, q_aug):
        st = _dot(kb_s[h, blocks[j], :], q_aug)
        if j == i:
            st = jnp.where(key_id <= qry_id, st, MASKED)
        s_s[h, tile_rows(i, j), :] = st
        return jnp.max(st, axis=0, keepdims=True)

    def prob_tile(i, h, j, m_row):
        p = jnp.exp2(s_s[h, tile_rows(i, j), :] - m_row)
        p_s[h, tile_rows(i, j), :] = p.astype(BF16)
        return jnp.sum(p, axis=0, keepdims=True)

    def weighted_values(i, h, l_row):
        keys = slice(tile_rows(i, 0).start, tile_rows(i, i).stop)
        acc = _dot(vb_s[h * HEAD_DIM:(h + 1) * HEAD_DIM, :(i + 1) * MOBA_BLOCK], p_s[h, keys, :])
        return acc / l_row

    groups = [(i, h) for i in range(n_blocks) for h in range(HEADS_PER_VREG)]

    def combine(acc, new, op):
        return new if acc is None else op(acc, new)

    q_next = query_operand(*groups[0])
    m_next = None
    for j in range(groups[0][0] + 1):
        m_next = combine(m_next, logits_tile(*groups[0], j, q_next), jnp.maximum)
    outs = []
    for g, (i, h) in enumerate(groups):
        m_row, m_next, l_row = m_next, None, None
        nxt = groups[g + 1] if g + 1 < len(groups) else None
        if nxt is not None:
            q_next = query_operand(*nxt)
        for j in range(max(i + 1, nxt[0] + 1 if nxt is not None else 0)):
            if j <= i:
                l_row = combine(l_row, prob_tile(i, h, j, m_row), jnp.add)
            if nxt is not None and j <= nxt[0]:
                m_next = combine(m_next, logits_tile(*nxt, j, q_next), jnp.maximum)
        outs.append(weighted_values(i, h, l_row))
        if h == HEADS_PER_VREG - 1:
            o_ref[blocks[i], :] = jnp.concatenate(outs, axis=0).T.astype(o_ref.dtype)
            outs = []


def _moba_prompt(q_t, k_t, v_t):
    batch, _, seq = k_t.shape
    n_blocks = seq // MOBA_BLOCK
    pairs = MOBA_WIDTH // LANES
    assert n_blocks <= SUBLANES and n_blocks <= HEAD_DIM
    tile_rows = MOBA_BLOCK * (n_blocks * (n_blocks + 1) // 2)
    feat = pl.BlockSpec((None, LANES, seq), lambda b, p: (b, p, 0))
    return pl.pallas_call(
        _moba_prompt_kernel,
        grid=(batch, pairs),
        in_specs=[feat, feat, feat],
        out_specs=pl.BlockSpec((seq, LANES), lambda b, p: (b, p)),
        out_shape=jax.ShapeDtypeStruct((batch * seq, MOBA_WIDTH), BF16),
        scratch_shapes=[pltpu.VMEM((HEADS_PER_VREG, seq, LANES), BF16),
                        pltpu.VMEM((LANES, seq), BF16),
                        pltpu.VMEM((SCORE_ROWS, LANES), F32),
                        pltpu.VMEM((HEADS_PER_VREG, tile_rows, MOBA_BLOCK), F32),
                        pltpu.VMEM((HEADS_PER_VREG, tile_rows, MOBA_BLOCK), BF16)],
        compiler_params=pltpu.CompilerParams(
            dimension_semantics=("parallel", "parallel"),
            vmem_limit_bytes=_vmem_limit(HEADS_PER_VREG * tile_rows * MOBA_BLOCK * 6 + 8 * LANES * seq * 4)),
        name="moba_prompt",
    )(q_t, k_t, v_t)


def _zero_after(value):
    return lax.shift_right_logical(pltpu.bitcast(value, jnp.uint32), jnp.uint32(32))


def _score_pages(pages, qb_s, zero):
    head_row = lax.broadcasted_iota(jnp.int32, (MOBA_HEADS, PAGE_SIZE), 0)
    tiles = [jnp.zeros((MOBA_HEADS, PAGE_SIZE), F32) for _ in pages]
    for h in range(MOBA_HEADS):
        rows = slice(h * HEAD_DIM, (h + 1) * HEAD_DIM)
        q_h = (qb_s[rows, :].reshape(HEAD_DIM // SUBLANES, SUBLANES, PAGE_SIZE) + zero).reshape(HEAD_DIM, PAGE_SIZE)
        for n, page in enumerate(pages):
            logit = jnp.sum(page[rows, :] * q_h, axis=0, keepdims=True)
            tiles[n] = jnp.where(head_row == h, logit, tiles[n])
    return tiles


def _select_blocks(lg, q_ref, kn_ref, psel_ref, ids_ref, pself_ref):
    n_pages = lg.shape[0]
    pages_per_block = MOBA_BLOCK // PAGE_SIZE
    n_blocks = n_pages // pages_per_block

    blk = lg.reshape(n_blocks, pages_per_block, MOBA_HEADS, PAGE_SIZE)
    blk_sum = blk[:, 0]
    for r in range(1, pages_per_block):
        blk_sum = blk_sum + blk[:, r]
    score_col = jnp.sum(blk_sum, axis=-1, keepdims=True) * (1.0 / MOBA_BLOCK)
    lane = lax.broadcasted_iota(jnp.int32, (MOBA_HEADS, LANES), 1)
    blk_of_lane = lane & (n_blocks - 1)
    score = jnp.zeros((MOBA_HEADS, LANES), F32)
    for n in range(n_blocks):
        score = jnp.where(blk_of_lane == n, score_col[n], score)
    rank = jnp.zeros(score.shape, jnp.int32)
    for k in range(1, n_blocks):
        other = pltpu.roll(score, k, 1)
        beats = (other > score) | ((other == score) & (blk_of_lane >= k))
        rank = rank + beats.astype(jnp.int32)
    bias = jnp.where(rank < MOBA_TOPK, 0.0, MASKED)

    q = q_ref[0]
    own_head = (lax.broadcasted_iota(jnp.int32, (MOBA_HEADS, MOBA_WIDTH), 1) // HEAD_DIM
                == lax.broadcasted_iota(jnp.int32, (MOBA_HEADS, MOBA_WIDTH), 0))
    lg_self = jnp.sum(jnp.where(own_head, q, 0.0) * kn_ref[0], axis=-1, keepdims=True)

    page_bias = [jnp.broadcast_to(bias[:, n:n + 1], (MOBA_HEADS, PAGE_SIZE)) for n in range(n_blocks)]
    lg = blk + jnp.stack(page_bias)[:, None]
    m = jnp.maximum(jnp.max(jnp.max(jnp.max(lg, axis=0), axis=0), axis=-1, keepdims=True), lg_self)
    p = jnp.exp(lg - m)
    p_self = jnp.exp(lg_self - m)
    denom = jnp.sum(jnp.sum(jnp.sum(p, axis=0), axis=0), axis=-1, keepdims=True) + p_self
    p = p / denom
    pself_ref[0] = jnp.broadcast_to(p_self / denom, (MOBA_HEADS, LANES))

    first_period = lane < n_blocks
    ids = jnp.zeros((MOBA_HEADS, LANES), F32)
    for t in range(MOBA_TOPK):
        is_t = jnp.where(rank == t, 1.0, 0.0)
        for r in range(pages_per_block):
            kept = jnp.zeros((MOBA_HEADS, PAGE_SIZE), F32)
            for n in range(n_blocks):
                kept = kept + is_t[:, n:n + 1] * p[n, r]
            psel_ref[0, t * pages_per_block + r] = kept
        blk_id = jnp.sum(jnp.where(first_period, is_t * blk_of_lane.astype(F32), 0.0), axis=-1, keepdims=True)
        ids = jnp.where(lane == t, blk_id, ids)
    ids_ref[0] = ids.astype(jnp.int32)


def _moba_sample_values_kernel(pt_ref, ids_ref, psel_ref, pself_ref, vn_ref, cv_ref, o_ref, vbuf, sem):
    b = pl.program_id(0)
    n_rows = pl.num_programs(0)
    pages_per_block = MOBA_BLOCK // PAGE_SIZE
    per_head = MOBA_TOPK * pages_per_block
    slot = lax.rem(b, 2)

    def chunk_copy(row, c, s):
        h, t, r = c // per_head, (c % per_head) // pages_per_block, c % pages_per_block
        page = pt_ref[row, ids_ref[row, h * MOBA_TOPK + t] * pages_per_block + r]
        return pltpu.make_async_copy(cv_ref.at[page, pl.ds(h * HEAD_DIM, HEAD_DIM), :], vbuf.at[s, c], sem.at[s])

    n_chunks = MOBA_HEADS * per_head

    @pl.when(b == 0)
    def _():
        for c in range(n_chunks):
            chunk_copy(0, c, 0).start()

    @pl.when(b + 1 < n_rows)
    def _():
        for c in range(n_chunks):
            chunk_copy(b + 1, c, 1 - slot).start()

    for c in range(n_chunks):
        chunk_copy(b, c, slot).wait()

    heads = []
    for h in range(MOBA_HEADS):
        acc = jnp.zeros((HEAD_DIM, PAGE_SIZE), F32)
        for tr in range(per_head):
            acc = acc + vbuf[slot, h * per_head + tr] * psel_ref[0, tr, h:h + 1, :]
        heads.append(acc)
    acc_hi, acc_lo = _split_bf16(jnp.concatenate(heads, axis=0))
    ones = jnp.ones((SUBLANES, PAGE_SIZE), BF16)
    past = (_dot_nt(ones, acc_hi) + _dot_nt(ones, acc_lo))[0:1]
    own_head = (lax.broadcasted_iota(jnp.int32, (MOBA_HEADS, MOBA_WIDTH), 1) // HEAD_DIM
                == lax.broadcasted_iota(jnp.int32, (MOBA_HEADS, MOBA_WIDTH), 0))
    p_self = jnp.sum(jnp.where(own_head, _lane_tile(pself_ref[0], MOBA_WIDTH), 0.0), axis=0, keepdims=True)
    o_ref[0] = (past + p_self * vn_ref[0]).astype(o_ref.dtype)


def _sample_specs(n_pages):
    pages_per_block = MOBA_BLOCK // PAGE_SIZE
    kept_pages = MOBA_TOPK * pages_per_block
    row_spec = pl.BlockSpec((1, 1, MOBA_WIDTH), lambda b, *_: (b, 0, 0))
    psel_spec = pl.BlockSpec((1, kept_pages, MOBA_HEADS, PAGE_SIZE), lambda b, *_: (b, 0, 0, 0))
    tile_spec = pl.BlockSpec((1, MOBA_HEADS, LANES), lambda b, *_: (b, 0, 0))
    return kept_pages, row_spec, psel_spec, tile_spec


def _outproj_ffn2_scores_kernel(pt_ref, x1_ref, yp_ref, ya_ref, ym_ref, wout_ref, n2_ref, wgu_ref, wo_ref,
                                qc_ref, q_ref, kn_ref, ck_ref,
                                y_ref, psel_ref, ids_ref, pself_ref,
                                h_s, act_s, kbuf_a, kbuf_b, qb_s, lg_s, sem):
    b = pl.program_id(0)
    n_rows = pl.num_programs(0)
    n_a, n_b = kbuf_a.shape[0], kbuf_b.shape[0]
    per_chunk = -(-(n_a + n_b) // FF_CHUNKS)
    switch_chunk = n_a // per_chunk
    assert n_a == switch_chunk * per_chunk
    issue_b = -(-n_b // switch_chunk)
    issue_a = -(-n_a // (FF_CHUNKS - switch_chunk))
    next_row = jnp.minimum(b + 1, n_rows - 1)

    def copy_a(row, j, zero=0):
        return pltpu.make_async_copy(ck_ref.at[pt_ref[row, j] + zero], kbuf_a.at[j], sem.at[0])

    def copy_b(row, j, zero=0):
        return pltpu.make_async_copy(ck_ref.at[pt_ref[row, n_a + j] + zero], kbuf_b.at[j], sem.at[1])

    @pl.when(b == 0)
    def _():
        for j in range(n_a):
            copy_a(0, j).start()

    qb_s[...] = jnp.broadcast_to(qc_ref[0], qb_s.shape)
    for j in range(n_a):
        copy_a(b, j).wait()

    def side_work(c, act):
        zero_bits = _zero_after(act[:SUBLANES, :PAGE_SIZE])
        zero_idx = pltpu.bitcast(zero_bits, jnp.int32)[0, 0]
        if c < switch_chunk:
            for j in range(c * issue_b, min((c + 1) * issue_b, n_b)):
                copy_b(b, j, zero_idx).start()
        else:
            if c == switch_chunk:
                for j in range(n_b):
                    copy_b(b, j).wait()
            first = (c - switch_chunk) * issue_a
            for j in range(first, min(first + issue_a, n_a)):
                copy_a(next_row, j, zero_idx).start()
        page_ids = range(c * per_chunk, min((c + 1) * per_chunk, n_a + n_b))
        pages = [kbuf_a.at[j] if j < n_a else kbuf_b.at[j - n_a] for j in page_ids]
        for j, tile in zip(page_ids, _score_pages(pages, qb_s, pltpu.bitcast(zero_bits, F32))):
            lg_s[j] = tile

    c1 = POOL_WIDTH
    c2 = c1 + MOBA_WIDTH
    mix = (_dot(yp_ref[...], wout_ref[:c1, :]) + _dot(ya_ref[...], wout_ref[c1:c2, :])
           + _dot(ym_ref[...], wout_ref[c2:, :]))
    x2 = x1_ref[...] + mix
    h_s[...] = _rmsnorm(x2, n2_ref[...]).astype(BF16)
    y_ref[...] = x2 + 0.5 * _swiglu(h_s, wgu_ref, wo_ref, act_s, side_work)
    _select_blocks(lg_s[...], q_ref, kn_ref, psel_ref, ids_ref, pself_ref)

    @pl.when(b == n_rows - 1)
    def _():
        for j in range(n_a):
            copy_a(next_row, j).wait()


def _outproj_ffn2_with_scores(x1, y_pool, y_moba, y_mem, tm, wout, n2, wgu, wo, page_table, q, k_new, cache_k_t):
    n = x1.shape[0]
    n_rows, n_pages = page_table.shape
    assert n // tm == n_rows
    pages_per_block = MOBA_BLOCK // PAGE_SIZE
    assert n_pages % pages_per_block == 0
    n_blocks = n_pages // pages_per_block
    assert LANES % n_blocks == 0 and n_blocks & (n_blocks - 1) == 0 and n_blocks >= MOBA_TOPK
    per_chunk = -(-n_pages // FF_CHUNKS)
    n_a = per_chunk * (FF_CHUNKS // 2)
    kept_pages, row_spec, psel_spec, tile_spec = _sample_specs(n_pages)
    col_spec = pl.BlockSpec((1, MOBA_WIDTH, 1), lambda b, *_: (b, 0, 0))

    def rows(width):
        return pl.BlockSpec((tm, width), lambda i, *_: (i, 0))

    resident = (2 * (wgu.size + wo.size + wout.size) + tm * (D_MODEL + D_FF) * 2
                + (n_pages + 1) * MOBA_WIDTH * PAGE_SIZE * 4)
    streamed = 2 * tm * (4 * 2 * D_MODEL + 2 * D_MODEL)
    return pl.pallas_call(
        _outproj_ffn2_scores_kernel,
        grid_spec=pltpu.PrefetchScalarGridSpec(
            num_scalar_prefetch=1,
            grid=(n_rows,),
            in_specs=[rows(D_MODEL), rows(POOL_WIDTH), rows(MOBA_WIDTH), rows(MEM_WIDTH),
                      _const_spec(wout.shape), _const_spec((1, D_MODEL)), _const_spec(wgu.shape),
                      _const_spec(wo.shape), col_spec, row_spec, row_spec, pl.BlockSpec(memory_space=pl.ANY)],
            out_specs=[rows(D_MODEL), psel_spec, tile_spec, tile_spec],
            scratch_shapes=[pltpu.VMEM((tm, D_MODEL), BF16), pltpu.VMEM((tm, D_FF), BF16),
                            pltpu.VMEM((n_a, MOBA_WIDTH, PAGE_SIZE), F32),
                            pltpu.VMEM((n_pages - n_a, MOBA_WIDTH, PAGE_SIZE), F32),
                            pltpu.VMEM((MOBA_WIDTH, PAGE_SIZE), F32),
                            pltpu.VMEM((n_pages, MOBA_HEADS, PAGE_SIZE), F32),
                            pltpu.SemaphoreType.DMA((2,))]),
        out_shape=[jax.ShapeDtypeStruct((n, D_MODEL), F32),
                   jax.ShapeDtypeStruct((n_rows, kept_pages, MOBA_HEADS, PAGE_SIZE), F32),
                   jax.ShapeDtypeStruct((n_rows, MOBA_HEADS, LANES), jnp.int32),
                   jax.ShapeDtypeStruct((n_rows, MOBA_HEADS, LANES), F32)],
        compiler_params=pltpu.CompilerParams(dimension_semantics=("arbitrary",),
                                             vmem_limit_bytes=_vmem_limit(resident + streamed)),
        name="outproj_ffn2_scores",
    )(page_table, x1, y_pool, y_moba, y_mem, wout, n2, wgu, wo, q.reshape(n_rows, MOBA_WIDTH, 1),
      q.reshape(n_rows, 1, MOBA_WIDTH), k_new.reshape(n_rows, 1, MOBA_WIDTH), cache_k_t)


def _moba_sample_values(page_table, p_sel, ids, p_self, v_new, cache_v_t):
    n_rows, n_pages = page_table.shape
    kept_pages, row_spec, psel_spec, tile_spec = _sample_specs(n_pages)
    any_spec = pl.BlockSpec(memory_space=pl.ANY)
    kept_ids = ids[:, :, :MOBA_TOPK].reshape(n_rows, MOBA_HEADS * MOBA_TOPK)
    return pl.pallas_call(
        _moba_sample_values_kernel,
        grid_spec=pltpu.PrefetchScalarGridSpec(
            num_scalar_prefetch=2,
            grid=(n_rows,),
            in_specs=[psel_spec, tile_spec, row_spec, any_spec],
            out_specs=row_spec,
            scratch_shapes=[pltpu.VMEM((2, MOBA_HEADS * kept_pages, HEAD_DIM, PAGE_SIZE), F32),
                            pltpu.SemaphoreType.DMA((2,))]),
        out_shape=jax.ShapeDtypeStruct((n_rows, 1, MOBA_WIDTH), BF16),
        compiler_params=pltpu.CompilerParams(dimension_semantics=("arbitrary",)),
        name="moba_sample_values",
    )(page_table, kept_ids, p_sel, p_self, v_new.reshape(n_rows, 1, MOBA_WIDTH), cache_v_t)


def _mem_kv_kernel(mem_ref, g_ref, wt_ref, kg_ref, mk_ref, mv_ref):
    h = _rmsnorm(mem_ref[...], g_ref[...]).astype(BF16)
    kv_t = _dot_nt(wt_ref[...], h)
    mk_ref[...] = _head_rmsnorm_t(kv_t[:MEM_WIDTH], _lane_tile(kg_ref[...], h.shape[0]))
    mv_ref[...] = kv_t[MEM_WIDTH:]


def _mem_kv(mem, batch, mem_len, g, w_t, kg_tab):
    feat = pl.BlockSpec((None, MEM_WIDTH, mem_len), lambda b: (b, 0, 0))
    return pl.pallas_call(
        _mem_kv_kernel,
        grid=(batch,),
        in_specs=[pl.BlockSpec((mem_len, D_MODEL), lambda b: (b, 0)), _const_spec((1, D_MODEL)),
                  _const_spec(w_t.shape), _const_spec(kg_tab.shape)],
        out_specs=[feat, feat],
        out_shape=[jax.ShapeDtypeStruct((batch, MEM_WIDTH, mem_len), F32)] * 2,
        compiler_params=pltpu.CompilerParams(dimension_semantics=("parallel",)),
        name="mem_kv",
    )(mem, g, w_t, kg_tab)


def _pool_select(lane, a2, a4, a8, a16):
    return jnp.where(lane < POOL_GROUP, a2,
                     jnp.where(lane < 2 * POOL_GROUP, a4, jnp.where(lane < 3 * POOL_GROUP, a8, a16)))


def _mix_prompt_kernel(u_ref, halo_ref, qm_ref, mk_ref, mv_ref, wbd_ref, scale_ref, yp_ref, ym_ref):
    t = pl.program_id(1)
    tm = u_ref.shape[0]
    halo_rows = halo_ref.shape[0]
    u = u_ref[...]
    ext = jnp.concatenate([jnp.where(t > 0, halo_ref[...], 0.0), u], axis=0)
    a2 = ext + pltpu.roll(ext, 1, 0)
    a4 = a2 + pltpu.roll(a2, 2, 0)
    a8 = a4 + pltpu.roll(a4, 4, 0)
    a16 = a8 + pltpu.roll(a8, 8, 0)
    lane = lax.broadcasted_iota(jnp.int32, u.shape, 1)
    sums = _pool_select(lane, a2[halo_rows:], a4[halo_rows:], a8[halo_rows:], a16[halo_rows:])
    pos = t * tm + lax.broadcasted_iota(jnp.int32, u.shape, 0)
    window = _pool_select(lane, *POOL_WINDOWS)
    cnt = jnp.minimum(window, pos + 1).astype(F32)
    pooled = sums / cnt - u
    yp_ref[...] = (_dot(pooled.astype(BF16), wbd_ref[...]) * scale_ref[...]).astype(yp_ref.dtype)

    mem_len = mk_ref.shape[1]
    row_v = lax.broadcasted_iota(jnp.int32, (LANES, mem_len), 0)
    lane_q = lax.broadcasted_iota(jnp.int32, (tm, LANES), 1)
    for pair in range(MEM_WIDTH // LANES):
        cols = slice(pair * LANES, (pair + 1) * LANES)
        qp = qm_ref[:, cols]
        mk_t = mk_ref[cols, :].astype(BF16)
        mv_t = mv_ref[cols, :]
        out = None
        for h in range(HEADS_PER_VREG):
            in_head = (lane_q < HEAD_DIM) if h == 0 else (lane_q >= HEAD_DIM)
            in_head_v = (row_v < HEAD_DIM) if h == 0 else (row_v >= HEAD_DIM)
            lg = _dot(jnp.where(in_head, qp, 0.0).astype(BF16), mk_t)
            p = jnp.exp(lg - jnp.max(lg, axis=-1, keepdims=True))
            o = _dot_nt(p.astype(BF16), jnp.where(in_head_v, mv_t, 0.0).astype(BF16))
            o = o / jnp.sum(p, axis=-1, keepdims=True)
            out = o if out is None else out + o
        ym_ref[:, cols] = out.astype(ym_ref.dtype)


def _mix_prompt(u, qm, mk_t, mv_t, wbd, scale, batch, seq, tm):
    halo_rows = 16
    assert halo_rows > POOL_BUF and tm % halo_rows == 0 and seq % tm == 0
    tiles = seq // tm
    mem_len = mk_t.shape[2]

    def rows(width):
        return pl.BlockSpec((tm, width), lambda b, t: (b * tiles + t, 0))

    halo = pl.BlockSpec((halo_rows, POOL_WIDTH),
                        lambda b, t: (jnp.maximum((b * tiles + t) * (tm // halo_rows) - 1, 0), 0))
    mem = pl.BlockSpec((None, MEM_WIDTH, mem_len), lambda b, t: (b, 0, 0))
    return pl.pallas_call(
        _mix_prompt_kernel,
        grid=(batch, tiles),
        in_specs=[rows(POOL_WIDTH), halo, rows(MEM_WIDTH), mem, mem, _const_spec(wbd.shape),
                  _const_spec((1, POOL_WIDTH))],
        out_specs=[rows(POOL_WIDTH), rows(MEM_WIDTH)],
        out_shape=[jax.ShapeDtypeStruct((batch * seq, POOL_WIDTH), BF16),
                   jax.ShapeDtypeStruct((batch * seq, MEM_WIDTH), BF16)],
        compiler_params=pltpu.CompilerParams(dimension_semantics=("parallel", "parallel")),
        name="mix_prompt",
    )(u, u, qm, mk_t, mv_t, wbd, scale)


def _mix_sample_kernel(u_ref, st_ref, qm_ref, mk_ref, mv_ref, wbd_ref, scale_ref, yp_ref, ym_ref, *, pos):
    rows = u_ref.shape[0]
    u = u_ref[...]
    tail = u
    sums = {}
    for back in range(1, POOL_BUF + 1):
        tail = tail + st_ref[POOL_BUF - back]
        if back + 1 in POOL_WINDOWS:
            sums[back + 1] = tail / float(min(back + 1, pos + 1))
    lane = lax.broadcasted_iota(jnp.int32, u.shape, 1)
    pooled = _pool_select(lane, *(sums[w] for w in POOL_WINDOWS)) - u
    yp_ref[...] = (_dot(pooled.astype(BF16), wbd_ref[...]) * scale_ref[...]).astype(yp_ref.dtype)

    row_t = lax.broadcasted_iota(jnp.int32, (SCORE_ROWS, MEM_WIDTH), 0)
    own_head = lax.broadcasted_iota(jnp.int32, (SCORE_ROWS, MEM_WIDTH), 1) // HEAD_DIM == row_t
    for r in range(rows):
        q = jnp.broadcast_to(qm_ref[r:r + 1, :], (SCORE_ROWS, MEM_WIDTH))
        q_t = jnp.where(own_head, q, 0.0).astype(BF16)
        lg = _dot(q_t, mk_ref[r].astype(BF16))
        p = jnp.exp(lg - jnp.max(lg, axis=-1, keepdims=True))
        p = p / jnp.sum(p, axis=-1, keepdims=True)
        o = _dot_nt(p.astype(BF16), mv_ref[r].astype(BF16))
        ym_ref[r:r + 1, :] = jnp.sum(jnp.where(own_head, o, 0.0), axis=0, keepdims=True).astype(ym_ref.dtype)


def _mix_sample(u, state_t, qm, mk_t, mv_t, wbd, scale, rows_per_step, pos):
    n = u.shape[0]
    mem_len = mk_t.shape[2]

    def rows(width):
        return pl.BlockSpec((rows_per_step, width), lambda i: (i, 0))

    mem = pl.BlockSpec((rows_per_step, MEM_WIDTH, mem_len), lambda i: (i, 0, 0))
    return pl.pallas_call(
        functools.partial(_mix_sample_kernel, pos=pos),
        grid=(n // rows_per_step,),
        in_specs=[rows(POOL_WIDTH), pl.BlockSpec((POOL_BUF, rows_per_step, POOL_WIDTH), lambda i: (0, i, 0)),
                  rows(MEM_WIDTH), mem, mem, _const_spec(wbd.shape), _const_spec((1, POOL_WIDTH))],
        out_specs=[rows(POOL_WIDTH), rows(MEM_WIDTH)],
        out_shape=[jax.ShapeDtypeStruct((n, POOL_WIDTH), BF16), jax.ShapeDtypeStruct((n, MEM_WIDTH), BF16)],
        compiler_params=pltpu.CompilerParams(dimension_semantics=("parallel",)),
        name="mix_sample",
    )(u, state_t, qm, mk_t, mv_t, wbd, scale)


def _rope_angles(pos):
    inv = jnp.power(jnp.float32(ROPE_THETA), -jnp.arange(ROT_HALF, dtype=F32) * (2.0 / ROT_DIM))
    return pos.astype(F32)[:, None] * inv[None, :]


def _rope_tables(pos):
    ang = _rope_angles(pos)
    cos, sin = jnp.cos(ang), jnp.sin(ang)
    rest = HEAD_DIM - ROT_DIM
    ones = jnp.ones((pos.shape[0], rest), F32)
    zeros = jnp.zeros((pos.shape[0], rest), F32)
    cos_h = jnp.concatenate([cos, cos, ones], axis=1)
    sin_h = jnp.concatenate([-sin, sin, zeros], axis=1)
    return jnp.tile(cos_h, (1, HEADS_PER_VREG)), jnp.tile(sin_h, (1, HEADS_PER_VREG))


def _block_diag_ones(width):
    head = jnp.arange(width) // HEAD_DIM
    return (head[:, None] == head[None, :]).astype(BF16)


def _tile_gain(g, width):
    return jnp.tile(g, width // HEAD_DIM).reshape(1, width)


def _gain_table(g, width):
    return jnp.broadcast_to(jnp.tile(g, width // HEAD_DIM)[:, None], (width, LANES))


def _feature_major_to_heads(a_t, n_heads):
    batch, _, tokens = a_t.shape
    return jnp.transpose(a_t.reshape(batch, n_heads, HEAD_DIM, tokens), (0, 3, 1, 2))[None]


def kernel(x_prompt, x_sample, mem_prompt, cache_k, cache_v, cache_mem_k, cache_mem_v, state_pool, page_table,
           ffn1_norm, ffn1_w_in, ffn1_w_out, mix_norm, w_in, pool_w, pool_scale, q_norm, k_norm,
           mem_norm, mem_w_kv, mem_q_norm, mem_k_norm, w_out, ffn2_norm, ffn2_w_in, ffn2_w_out):
    depth = ffn1_norm.shape[0]
    assert depth == 1
    n_p, t_p, _ = x_prompt.shape
    n_s, t_s, _ = x_sample.shape
    assert t_s == 1
    past_len = page_table.shape[1] * PAGE_SIZE
    mem_len = mem_prompt.shape[1]
    l = 0

    wgu1, wo1 = ffn1_w_in[l].astype(BF16), ffn1_w_out[l].astype(BF16)
    wgu2, wo2 = ffn2_w_in[l].astype(BF16), ffn2_w_out[l].astype(BF16)
    win, wout = w_in[l].astype(BF16), w_out[l].astype(BF16)
    c1 = POOL_WIDTH
    c4 = c1 + 3 * MOBA_WIDTH
    wnat = jnp.concatenate([win[:, :c1], win[:, c4:]], axis=1)
    wqkv_t = win[:, c1:c4].T
    wkv_t = mem_w_kv[l].astype(BF16).T
    n1, nmix, n2, nmem = (a[l].reshape(1, D_MODEL) for a in (ffn1_norm, mix_norm, ffn2_norm, mem_norm))
    qg, kg = _tile_gain(q_norm[l], MOBA_WIDTH), _tile_gain(k_norm[l], MOBA_WIDTH)
    mqg = _tile_gain(mem_q_norm[l], MEM_WIDTH)
    qg_tab, kg_tab = _gain_table(q_norm[l], MOBA_WIDTH), _gain_table(k_norm[l], MOBA_WIDTH)
    mkg_tab = _gain_table(mem_k_norm[l], MEM_WIDTH)
    ones_bd = _block_diag_ones(MOBA_WIDTH)
    ones_mem = ones_bd[:MEM_WIDTH, :MEM_WIDTH]
    wbd = jax.scipy.linalg.block_diag(*[pool_w[l, g] for g in range(len(POOL_WINDOWS))]).astype(BF16)
    scale = pool_scale[l].reshape(1, POOL_WIDTH)

    xp = x_prompt.reshape(n_p * t_p, D_MODEL)
    ang_p = _rope_angles(jnp.arange(t_p, dtype=jnp.int32)).T
    x1p, up, qmp, qt_p, kt_p, vt_p = _ffn1_inproj_t(xp, n_p, t_p, ROW_TILE, n1, wgu1, wo1, nmix, wnat, wqkv_t,
                                                   qg_tab, kg_tab, mqg, ones_mem, jnp.cos(ang_p), jnp.sin(ang_p))
    mkt_p, mvt_p = _mem_kv(mem_prompt.reshape(n_p * mem_len, D_MODEL), n_p, mem_len, nmem, wkv_t, mkg_tab)
    ya_p = _moba_prompt(qt_p, kt_p, vt_p)
    yp_p, ym_p = _mix_prompt(up, qmp, mkt_p, mvt_p, wbd, scale, n_p, t_p, ROW_TILE)

    xs = x_sample.reshape(n_s, D_MODEL)
    cos_s, sin_s = _rope_tables(jnp.full((n_s,), past_len, jnp.int32))
    x1s, us, qs, ks, vs, qms = _ffn1_inproj(xs, n1, wgu1, wo1, nmix, win, qg, kg, mqg, ones_bd, cos_s, sin_s)

    def feature_major(cache, n_heads):
        rows, tokens = cache.shape[1], cache.shape[2]
        return jnp.transpose(cache[l], (0, 2, 3, 1)).reshape(rows, n_heads * HEAD_DIM, tokens)

    y_prompt, p_sel, kept_ids, p_self = _outproj_ffn2_with_scores(
        x1p, yp_p, ya_p, ym_p, ROW_TILE, wout, n2, wgu2, wo2, page_table, qs, ks, feature_major(cache_k, MOBA_HEADS))
    ya_s = _moba_sample_values(page_table, p_sel, kept_ids, p_self, vs,
                               feature_major(cache_v, MOBA_HEADS)).reshape(n_s, MOBA_WIDTH)
    state = state_pool[l]
    yp_s, ym_s = _mix_sample(us, jnp.swapaxes(state, 0, 1), qms, feature_major(cache_mem_k, MEM_HEADS),
                             feature_major(cache_mem_v, MEM_HEADS), wbd, scale, 8, past_len)
    y_sample = _outproj_ffn2(x1s, yp_s, ya_s, ym_s, n_s, wout, n2, wgu2, wo2)

    pool_state_p = up.reshape(n_p, t_p, POOL_WIDTH)[:, t_p - POOL_BUF:][None]
    pool_state_s = jnp.concatenate([state[:, 1:], us[:, None, :]], axis=1)[None]
    return (y_prompt.reshape(n_p, t_p, D_MODEL), y_sample.reshape(n_s, t_s, D_MODEL),
            _feature_major_to_heads(kt_p, MOBA_HEADS), _feature_major_to_heads(vt_p, MOBA_HEADS),
            _feature_major_to_heads(mkt_p, MEM_HEADS), _feature_major_to_heads(mvt_p, MEM_HEADS),
            pool_state_p,
            ks.reshape(1, n_s, t_s, MOBA_HEADS, HEAD_DIM), vs.reshape(1, n_s, t_s, MOBA_HEADS, HEAD_DIM),
            pool_state_s)
```

```python
import functools

import jax
import jax.numpy as jnp
from jax import lax
from jax.experimental import pallas as pl
from jax.experimental.pallas import tpu as pltpu

F32 = jnp.float32
BF16 = jnp.bfloat16

D_MODEL = 1024
HEAD_DIM = 64
POOL_WIDTH = 256
POOL_WINDOWS = (2, 4, 8, 16)
POOL_GROUP = POOL_WIDTH // len(POOL_WINDOWS)
POOL_BUF = max(POOL_WINDOWS) - 1
MOBA_WIDTH = 512
MOBA_HEADS = MOBA_WIDTH // HEAD_DIM
MOBA_BLOCK = 256
MOBA_TOPK = 3
MEM_WIDTH = 256
MEM_HEADS = MEM_WIDTH // HEAD_DIM
ROT_DIM = HEAD_DIM // 4
ROT_HALF = ROT_DIM // 2
ROPE_THETA = 500000.0
D_FF = 2816
RMS_EPS = 1e-6
PAGE_SIZE = 128
ATTN_SCALE = HEAD_DIM ** -0.5
LOG2_E = 1.4426950408889634

LANES = 128
SUBLANES = 8
V7X_VMEM_BYTES = 64 * 1024 * 1024

HEADS_PER_VREG = LANES // HEAD_DIM
MASKED = -1e30
FF_CHUNK = 256
FF_CHUNKS = D_FF // FF_CHUNK
ROW_TILE = 512
SCORE_ROWS = 16


def _dot(a, b):
    return jnp.dot(a, b, preferred_element_type=F32)


def _dot_nt(a, b):
    return lax.dot_general(a, b, (((1,), (1,)), ((), ())), preferred_element_type=F32)


def _split_bf16(x):
    hi = x.astype(BF16)
    lo = (x - hi.astype(F32)).astype(BF16)
    return hi, lo


def _vmem_limit(resident_bytes):
    return int(min(resident_bytes + 16 * 1024 * 1024, V7X_VMEM_BYTES - 4 * 1024 * 1024))


def _rmsnorm(x, g):
    ms = jnp.mean(x * x, axis=-1, keepdims=True)
    return x * lax.rsqrt(ms + RMS_EPS) * g


def _head_rmsnorm(x, g, ones_bd):
    hi, lo = _split_bf16(x * x)
    ms = (_dot(hi, ones_bd) + _dot(lo, ones_bd)) * (1.0 / HEAD_DIM)
    return x * lax.rsqrt(ms + RMS_EPS) * g


def _head_rmsnorm_t(x, g):
    heads = []
    for r in range(0, x.shape[0], HEAD_DIM):
        xh = x[r:r + HEAD_DIM]
        ms = jnp.mean(xh * xh, axis=0, keepdims=True)
        heads.append(xh * lax.rsqrt(ms + RMS_EPS))
    return jnp.concatenate(heads, axis=0) * g


def _rope(x, cos, sin):
    width = x.shape[1]
    lane = lax.broadcasted_iota(jnp.int32, x.shape, 1)
    first_half = (lane & (HEAD_DIM - 1)) < ROT_HALF
    partner = jnp.where(first_half, pltpu.roll(x, width - ROT_HALF, 1), pltpu.roll(x, ROT_HALF, 1))
    return x * cos + partner * sin


def _rope_t(x, cos, sin):
    pieces = []
    for r in range(0, x.shape[0], HEAD_DIM):
        x1, x2 = x[r:r + ROT_HALF], x[r + ROT_HALF:r + ROT_DIM]
        pieces += [x1 * cos - x2 * sin, x2 * cos + x1 * sin, x[r + ROT_DIM:r + HEAD_DIM]]
    return jnp.concatenate(pieces, axis=0)


def _lane_tile(table, n):
    return jnp.concatenate([table] * (n // LANES), axis=1)


def _swiglu(h_s, wgu_ref, wo_ref, act_s, side_work=None):
    for c in range(FF_CHUNKS):
        for lo in range(c * FF_CHUNK, (c + 1) * FF_CHUNK, LANES):
            gate_up = _dot(h_s[...], wgu_ref[:, 2 * lo:2 * (lo + LANES)])
            act = jax.nn.silu(gate_up[:, :LANES]) * gate_up[:, LANES:]
            act_s[:, lo:lo + LANES] = act.astype(BF16)
        if side_work is not None:
            side_work(c, act)
    return _dot(act_s[...], wo_ref[...])


def _const_spec(shape):
    return pl.BlockSpec(shape, lambda *_: (0,) * len(shape), pipeline_mode=pl.Buffered(1))


def _ffn1_trunk(x_ref, n1_ref, wgu_ref, wo_ref, nmix_ref, x1_ref, h_s, act_s):
    h_s[...] = _rmsnorm(x_ref[...], n1_ref[...]).astype(BF16)
    x1_ref[...] = x_ref[...] + 0.5 * _swiglu(h_s, wgu_ref, wo_ref, act_s)
    h_s[...] = _rmsnorm(x1_ref[...], nmix_ref[...]).astype(BF16)


def _ffn1_inproj_kernel(x_ref, n1_ref, wgu_ref, wo_ref, nmix_ref, win_ref, qg_ref, kg_ref, mg_ref,
                        ones_ref, cos_ref, sin_ref,
                        x1_ref, u_ref, q_ref, k_ref, v_ref, qm_ref, h_s, act_s):
    _ffn1_trunk(x_ref, n1_ref, wgu_ref, wo_ref, nmix_ref, x1_ref, h_s, act_s)
    c1 = POOL_WIDTH
    c2 = c1 + MOBA_WIDTH
    c3 = c2 + MOBA_WIDTH
    c4 = c3 + MOBA_WIDTH
    u_ref[...] = _dot(h_s[...], win_ref[:, :c1])
    q = _dot(h_s[...], win_ref[:, c1:c2])
    k = _dot(h_s[...], win_ref[:, c2:c3])
    v_ref[...] = _dot(h_s[...], win_ref[:, c3:c4])
    qm = _dot(h_s[...], win_ref[:, c4:])
    reps = MOBA_WIDTH // LANES
    cos = jnp.concatenate([cos_ref[...]] * reps, axis=1)
    sin = jnp.concatenate([sin_ref[...]] * reps, axis=1)
    ones = ones_ref[...]
    q_ref[...] = _rope(_head_rmsnorm(q, qg_ref[...], ones), cos, sin) * ATTN_SCALE
    k_ref[...] = _rope(_head_rmsnorm(k, kg_ref[...], ones), cos, sin)
    qm_ref[...] = _head_rmsnorm(qm, mg_ref[...], ones_ref[:MEM_WIDTH, :MEM_WIDTH]) * ATTN_SCALE


def _ffn1_inproj(x, n1, wgu, wo, nmix, win, qg, kg, mg, ones_bd, cos_tab, sin_tab):
    n = x.shape[0]

    def full(width):
        return pl.BlockSpec((n, width), lambda i: (0, 0))

    widths = (D_MODEL, POOL_WIDTH, MOBA_WIDTH, MOBA_WIDTH, MOBA_WIDTH, MEM_WIDTH)
    resident = 2 * (wgu.size + wo.size + win.size + ones_bd.size)
    return pl.pallas_call(
        _ffn1_inproj_kernel,
        grid=(1,),
        in_specs=[full(D_MODEL), _const_spec((1, D_MODEL)), _const_spec(wgu.shape), _const_spec(wo.shape),
                  _const_spec((1, D_MODEL)), _const_spec(win.shape),
                  _const_spec((1, MOBA_WIDTH)), _const_spec((1, MOBA_WIDTH)), _const_spec((1, MEM_WIDTH)),
                  _const_spec(ones_bd.shape), full(LANES), full(LANES)],
        out_specs=[full(w) for w in widths],
        out_shape=[jax.ShapeDtypeStruct((n, w), F32) for w in widths],
        scratch_shapes=[pltpu.VMEM((n, D_MODEL), BF16), pltpu.VMEM((n, D_FF), BF16)],
        compiler_params=pltpu.CompilerParams(dimension_semantics=("arbitrary",),
                                             vmem_limit_bytes=_vmem_limit(resident)),
        name="ffn1_inproj_sample",
    )(x, n1, wgu, wo, nmix, win, qg, kg, mg, ones_bd, cos_tab, sin_tab)


def _ffn1_inproj_t_kernel(x_ref, n1_ref, wgu_ref, wo_ref, nmix_ref, wnat_ref, wqkv_t_ref, qg_ref, kg_ref, mg_ref,
                          ones_ref, cos_ref, sin_ref, wgu2_f32_ref, wo2_f32_ref, wout_f32_ref,
                          x1_ref, u_ref, qm_ref, qt_ref, kt_ref, vt_ref, wgu2_ref, wo2_ref, wout_ref, h_s, act_s):
    for k in range(D_FF // LANES):
        wgu2_ref[:, 2 * k * LANES:(2 * k + 1) * LANES] = wgu2_f32_ref[:, k * LANES:(k + 1) * LANES].astype(BF16)
        wgu2_ref[:, (2 * k + 1) * LANES:(2 * k + 2) * LANES] = (
            wgu2_f32_ref[:, D_FF + k * LANES:D_FF + (k + 1) * LANES].astype(BF16))
    wo2_ref[...] = wo2_f32_ref[...].astype(BF16)
    wout_ref[...] = wout_f32_ref[...].astype(BF16)
    _ffn1_trunk(x_ref, n1_ref, wgu_ref, wo_ref, nmix_ref, x1_ref, h_s, act_s)
    tm = x_ref.shape[0]
    u_ref[...] = _dot(h_s[...], wnat_ref[:, :POOL_WIDTH])
    qm = _dot(h_s[...], wnat_ref[:, POOL_WIDTH:])
    qm_ref[...] = _head_rmsnorm(qm, mg_ref[...], ones_ref[...]) * ATTN_SCALE
    cos, sin = cos_ref[...], sin_ref[...]
    q_t = _dot_nt(wqkv_t_ref[:MOBA_WIDTH, :], h_s[...])
    qt_ref[...] = _rope_t(_head_rmsnorm_t(q_t, _lane_tile(qg_ref[...], tm)), cos, sin) * (ATTN_SCALE * LOG2_E)
    k_t = _dot_nt(wqkv_t_ref[MOBA_WIDTH:2 * MOBA_WIDTH, :], h_s[...])
    kt_ref[...] = _rope_t(_head_rmsnorm_t(k_t, _lane_tile(kg_ref[...], tm)), cos, sin)
    vt_ref[...] = _dot_nt(wqkv_t_ref[2 * MOBA_WIDTH:, :], h_s[...])


def _ffn1_inproj_t(x, batch, seq, tm, n1, wgu, wo, nmix, wnat, wqkv_t, qg_tab, kg_tab, mg, ones_bd, cos_t, sin_t,
                   wgu2_f32, wo2_f32, wout_f32):
    tiles = seq // tm
    steps = batch * tiles

    def rows(width):
        return pl.BlockSpec((tm, width), lambda i: (i, 0))

    def slab(w, bf16_rows=16):
        n_slabs = steps
        while w.shape[0] % n_slabs or (w.shape[0] // n_slabs) % bf16_rows:
            n_slabs //= 2
        return pl.BlockSpec((w.shape[0] // n_slabs, w.shape[1]), lambda i, n=n_slabs: (jnp.minimum(i, n - 1), 0))

    feat = pl.BlockSpec((None, MOBA_WIDTH, tm), lambda i: (i // tiles, 0, i % tiles))
    tab = pl.BlockSpec((ROT_HALF, tm), lambda i: (0, i % tiles))
    casts = (wgu2_f32, wo2_f32, wout_f32)
    resident = 2 * (wgu.size + wo.size + wnat.size + wqkv_t.size + ones_bd.size) + tm * (D_MODEL + D_FF) * 2
    streamed = (2 * 4 * tm * (2 * D_MODEL + POOL_WIDTH + MEM_WIDTH + 3 * MOBA_WIDTH + 2 * ROT_HALF)
                + 2 * 6 * sum(w.size for w in casts) // steps)
    feat_shape = jax.ShapeDtypeStruct((batch, MOBA_WIDTH, seq), F32)
    return pl.pallas_call(
        _ffn1_inproj_t_kernel,
        grid=(steps,),
        in_specs=[rows(D_MODEL), _const_spec((1, D_MODEL)), _const_spec(wgu.shape), _const_spec(wo.shape),
                  _const_spec((1, D_MODEL)), _const_spec(wnat.shape), _const_spec(wqkv_t.shape),
                  _const_spec(qg_tab.shape), _const_spec(kg_tab.shape), _const_spec((1, MEM_WIDTH)),
                  _const_spec(ones_bd.shape), tab, tab] + [slab(w) for w in casts],
        out_specs=[rows(D_MODEL), rows(POOL_WIDTH), rows(MEM_WIDTH), feat, feat, feat] + [slab(w) for w in casts],
        out_shape=[jax.ShapeDtypeStruct((batch * seq, w), F32) for w in (D_MODEL, POOL_WIDTH, MEM_WIDTH)]
        + [feat_shape] * 3 + [jax.ShapeDtypeStruct(w.shape, BF16) for w in casts],
        scratch_shapes=[pltpu.VMEM((tm, D_MODEL), BF16), pltpu.VMEM((tm, D_FF), BF16)],
        compiler_params=pltpu.CompilerParams(dimension_semantics=("arbitrary",),
                                             vmem_limit_bytes=_vmem_limit(resident + streamed)),
        name="ffn1_inproj",
    )(x, n1, wgu, wo, nmix, wnat, wqkv_t, qg_tab, kg_tab, mg, ones_bd, cos_t, sin_t, *casts)


def _outproj_ffn2_kernel(x1_ref, yp_ref, ya_ref, ym_ref, wout_ref, n2_ref, wgu_ref, wo_ref, y_ref, h_s, act_s):
    c1 = POOL_WIDTH
    c2 = c1 + MOBA_WIDTH
    mix = (_dot(yp_ref[...], wout_ref[:c1, :]) + _dot(ya_ref[...], wout_ref[c1:c2, :])
           + _dot(ym_ref[...], wout_ref[c2:, :]))
    x2 = x1_ref[...] + mix
    h_s[...] = _rmsnorm(x2, n2_ref[...]).astype(BF16)
    y_ref[...] = x2 + 0.5 * _swiglu(h_s, wgu_ref, wo_ref, act_s)


def _outproj_ffn2(x1, y_pool, y_moba, y_mem, tm, wout, n2, wgu, wo):
    n = x1.shape[0]

    def rows(width):
        return pl.BlockSpec((tm, width), lambda i: (i, 0))

    resident = 2 * (wgu.size + wo.size + wout.size) + tm * (D_MODEL + D_FF) * 2
    streamed = 2 * tm * (4 * 2 * D_MODEL + 2 * D_MODEL)
    return pl.pallas_call(
        _outproj_ffn2_kernel,
        grid=(n // tm,),
        in_specs=[rows(D_MODEL), rows(POOL_WIDTH), rows(MOBA_WIDTH), rows(MEM_WIDTH),
                  _const_spec(wout.shape), _const_spec((1, D_MODEL)), _const_spec(wgu.shape), _const_spec(wo.shape)],
        out_specs=rows(D_MODEL),
        out_shape=jax.ShapeDtypeStruct((n, D_MODEL), F32),
        scratch_shapes=[pltpu.VMEM((tm, D_MODEL), BF16), pltpu.VMEM((tm, D_FF), BF16)],
        compiler_params=pltpu.CompilerParams(dimension_semantics=("parallel",),
                                             vmem_limit_bytes=_vmem_limit(resident + streamed)),
        name="outproj_ffn2",
    )(x1, y_pool, y_moba, y_mem, wout, n2, wgu, wo)


def _moba_prompt_kernel(qt_ref, kt_ref, vt_ref, o_ref, kb_s, vb_s, km_s, s_s, p_s):
    seq = kt_ref.shape[1]
    n_blocks = seq // MOBA_BLOCK
    blocks = [slice(n * MOBA_BLOCK, (n + 1) * MOBA_BLOCK) for n in range(n_blocks)]
    lane_k = lax.broadcasted_iota(jnp.int32, (MOBA_BLOCK, LANES), 1)
    km_s[...] = jnp.zeros_like(km_s)
    for n in range(n_blocks):
        k_n = kt_ref[:, blocks[n]].T
        km_s[n:n + 1, :] = jnp.mean(k_n, axis=0, keepdims=True)
        for h in range(HEADS_PER_VREG):
            own = (lane_k < HEAD_DIM) if h == 0 else (lane_k >= HEAD_DIM)
            tag_lane = (1 - h) * HEAD_DIM + n
            kb_s[h, blocks[n], :] = jnp.where(own, k_n, jnp.where(lane_k == tag_lane, 1.0, 0.0)).astype(BF16)
    vb_s[...] = vt_ref[...].astype(BF16)

    lane_m = lax.broadcasted_iota(jnp.int32, km_s.shape, 1)
    kmean = km_s[...]
    km_split = [_split_bf16(jnp.where((lane_m < HEAD_DIM) if h == 0 else (lane_m >= HEAD_DIM), kmean, 0.0))
                for h in range(HEADS_PER_VREG)]
    blk = lax.broadcasted_iota(jnp.int32, (SUBLANES, MOBA_BLOCK), 0)
    key_id = lax.broadcasted_iota(jnp.int32, (MOBA_BLOCK, MOBA_BLOCK), 0)
    qry_id = lax.broadcasted_iota(jnp.int32, (MOBA_BLOCK, MOBA_BLOCK), 1)
    spare_rows = HEAD_DIM - SUBLANES

    def tile_rows(i, j):
        start = MOBA_BLOCK * (i * (i + 1) // 2 + j)
        return slice(start, start + MOBA_BLOCK)

    def query_operand(i, h):
        q_t = qt_ref[:, blocks[i]]
        if i == 0:
            bias = jnp.zeros((SUBLANES, MOBA_BLOCK), F32)
        else:
            q_hi, q_lo = _split_bf16(q_t)
            km_hi, km_lo = km_split[h]
            s = (_dot(km_hi, q_hi) + _dot(km_lo, q_hi) + _dot(km_hi, q_lo))[:SUBLANES]
            s = jnp.where(blk < i, s, -jnp.inf)
            rank = jnp.zeros(s.shape, jnp.int32)
            for m in range(i):
                sm = s[m:m + 1, :]
                beats = (sm > s) | ((sm == s) & (blk > m))
                rank = rank + beats.astype(jnp.int32)
            allowed = ((blk < i) & (rank < MOBA_TOPK)) | (blk == i)
            bias = jnp.where(allowed, 0.0, MASKED)
        zeros = jnp.zeros((spare_rows, MOBA_BLOCK), F32)
        if h == 0:
            return jnp.concatenate([q_t[:HEAD_DIM], bias, zeros], axis=0).astype(BF16)
        return jnp.concatenate([bias, zeros, q_t[HEAD_DIM:]], axis=0).astype(BF16)

    def logits_tile(i, h, j, q_aug):
        st = _dot(kb_s[h, blocks[j], :], q_aug)
        if j == i:
            st = jnp.where(key_id <= qry_id, st, MASKED)
        s_s[h, tile_rows(i, j), :] = st
        return jnp.max(st, axis=0, keepdims=True)

    def prob_tile(i, h, j, m_row):
        p = jnp.exp2(s_s[h, tile_rows(i, j), :] - m_row)
        p_s[h, tile_rows(i, j), :] = p.astype(BF16)
        return jnp.sum(p, axis=0, keepdims=True)

    def weighted_values(i, h, l_row):
        keys = slice(tile_rows(i, 0).start, tile_rows(i, i).stop)
        acc = _dot(vb_s[h * HEAD_DIM:(h + 1) * HEAD_DIM, :(i + 1) * MOBA_BLOCK], p_s[h, keys, :])
        return acc / l_row

    groups = [(i, h) for i in range(n_blocks) for h in range(HEADS_PER_VREG)]

    def combine(acc, new, op):
        return new if acc is None else op(acc, new)

    q_next = query_operand(*groups[0])
    m_next = None
    for j in range(groups[0][0] + 1):
        m_next = combine(m_next, logits_tile(*groups[0], j, q_next), jnp.maximum)
    outs = []
    for g, (i, h) in enumerate(groups):
        m_row, m_next, l_row = m_next, None, None
        nxt = groups[g + 1] if g + 1 < len(groups) else None
        if nxt is not None:
            q_next = query_operand(*nxt)
        for j in range(max(i + 1, nxt[0] + 1 if nxt is not None else 0)):
            if j <= i:
                l_row = combine(l_row, prob_tile(i, h, j, m_row), jnp.add)
            if nxt is not None and j <= nxt[0]:
                m_next = combine(m_next, logits_tile(*nxt, j, q_next), jnp.maximum)
        outs.append(weighted_values(i, h, l_row))
        if h == HEADS_PER_VREG - 1:
            o_ref[blocks[i], :] = jnp.concatenate(outs, axis=0).T.astype(o_ref.dtype)
            outs = []


def _moba_prompt(q_t, k_t, v_t):
    batch, _, seq = k_t.shape
    n_blocks = seq // MOBA_BLOCK
    pairs = MOBA_WIDTH // LANES
    assert n_blocks <= SUBLANES and n_blocks <= HEAD_DIM
    tile_rows = MOBA_BLOCK * (n_blocks * (n_blocks + 1) // 2)
    feat = pl.BlockSpec((None, LANES, seq), lambda b, p: (b, p, 0))
    return pl.pallas_call(
        _moba_prompt_kernel,
        grid=(batch, pairs),
        in_specs=[feat, feat, feat],
        out_specs=pl.BlockSpec((seq, LANES), lambda b, p: (b, p)),
        out_shape=jax.ShapeDtypeStruct((batch * seq, MOBA_WIDTH), BF16),
        scratch_shapes=[pltpu.VMEM((HEADS_PER_VREG, seq, LANES), BF16),
                        pltpu.VMEM((LANES, seq), BF16),
                        pltpu.VMEM((SCORE_ROWS, LANES), F32),
                        pltpu.VMEM((HEADS_PER_VREG, tile_rows, MOBA_BLOCK), F32),
                        pltpu.VMEM((HEADS_PER_VREG, tile_rows, MOBA_BLOCK), BF16)],
        compiler_params=pltpu.CompilerParams(
            dimension_semantics=("parallel", "parallel"),
            vmem_limit_bytes=_vmem_limit(HEADS_PER_VREG * tile_rows * MOBA_BLOCK * 6 + 8 * LANES * seq * 4)),
        name="moba_prompt",
    )(q_t, k_t, v_t)


def _zero_after(value):
    return lax.shift_right_logical(pltpu.bitcast(value, jnp.uint32), jnp.uint32(32))


def _score_pages(pages, qb_s, zero):
    head_row = lax.broadcasted_iota(jnp.int32, (MOBA_HEADS, PAGE_SIZE), 0)
    tiles = [jnp.zeros((MOBA_HEADS, PAGE_SIZE), F32) for _ in pages]
    for h in range(MOBA_HEADS):
        rows = slice(h * HEAD_DIM, (h + 1) * HEAD_DIM)
        q_h = (qb_s[rows, :].reshape(HEAD_DIM // SUBLANES, SUBLANES, PAGE_SIZE) + zero).reshape(HEAD_DIM, PAGE_SIZE)
        for n, page in enumerate(pages):
            logit = jnp.sum(page[rows, :] * q_h, axis=0, keepdims=True)
            tiles[n] = jnp.where(head_row == h, logit, tiles[n])
    return tiles


def _select_blocks(lg, q_ref, kn_ref, psel_ref, ids_ref, pself_ref):
    n_pages = lg.shape[0]
    pages_per_block = MOBA_BLOCK // PAGE_SIZE
    n_blocks = n_pages // pages_per_block

    blk = lg.reshape(n_blocks, pages_per_block, MOBA_HEADS, PAGE_SIZE)
    blk_sum = blk[:, 0]
    for r in range(1, pages_per_block):
        blk_sum = blk_sum + blk[:, r]
    score_col = jnp.sum(blk_sum, axis=-1, keepdims=True) * (1.0 / MOBA_BLOCK)
    lane = lax.broadcasted_iota(jnp.int32, (MOBA_HEADS, LANES), 1)
    blk_of_lane = lane & (n_blocks - 1)
    score = jnp.zeros((MOBA_HEADS, LANES), F32)
    for n in range(n_blocks):
        score = jnp.where(blk_of_lane == n, score_col[n], score)
    rank = jnp.zeros(score.shape, jnp.int32)
    for k in range(1, n_blocks):
        other = pltpu.roll(score, k, 1)
        beats = (other > score) | ((other == score) & (blk_of_lane >= k))
        rank = rank + beats.astype(jnp.int32)
    bias = jnp.where(rank < MOBA_TOPK, 0.0, MASKED)

    q = q_ref[0]
    own_head = (lax.broadcasted_iota(jnp.int32, (MOBA_HEADS, MOBA_WIDTH), 1) // HEAD_DIM
                == lax.broadcasted_iota(jnp.int32, (MOBA_HEADS, MOBA_WIDTH), 0))
    lg_self = jnp.sum(jnp.where(own_head, q, 0.0) * kn_ref[0], axis=-1, keepdims=True)

    page_bias = [jnp.broadcast_to(bias[:, n:n + 1], (MOBA_HEADS, PAGE_SIZE)) for n in range(n_blocks)]
    lg = blk + jnp.stack(page_bias)[:, None]
    m = jnp.maximum(jnp.max(jnp.max(jnp.max(lg, axis=0), axis=0), axis=-1, keepdims=True), lg_self)
    p = jnp.exp(lg - m)
    p_self = jnp.exp(lg_self - m)
    denom = jnp.sum(jnp.sum(jnp.sum(p, axis=0), axis=0), axis=-1, keepdims=True) + p_self
    p = p / denom
    pself_ref[0] = jnp.broadcast_to(p_self / denom, (MOBA_HEADS, LANES))

    first_period = lane < n_blocks
    ids = jnp.zeros((MOBA_HEADS, LANES), F32)
    for t in range(MOBA_TOPK):
        is_t = jnp.where(rank == t, 1.0, 0.0)
        for r in range(pages_per_block):
            kept = jnp.zeros((MOBA_HEADS, PAGE_SIZE), F32)
            for n in range(n_blocks):
                kept = kept + is_t[:, n:n + 1] * p[n, r]
            psel_ref[0, t * pages_per_block + r] = kept
        blk_id = jnp.sum(jnp.where(first_period, is_t * blk_of_lane.astype(F32), 0.0), axis=-1, keepdims=True)
        ids = jnp.where(lane == t, blk_id, ids)
    ids_ref[0] = ids.astype(jnp.int32)


def _moba_sample_values_kernel(pt_ref, ids_ref, psel_ref, pself_ref, vn_ref, cv_ref, o_ref, vbuf, sem):
    b = pl.program_id(0)
    n_rows = pl.num_programs(0)
    pages_per_block = MOBA_BLOCK // PAGE_SIZE
    per_head = MOBA_TOPK * pages_per_block
    slot = lax.rem(b, 2)

    def chunk_copy(row, c, s):
        h, t, r = c // per_head, (c % per_head) // pages_per_block, c % pages_per_block
        page = pt_ref[row, ids_ref[row, h * MOBA_TOPK + t] * pages_per_block + r]
        return pltpu.make_async_copy(cv_ref.at[page, pl.ds(h * HEAD_DIM, HEAD_DIM), :], vbuf.at[s, c], sem.at[s])

    n_chunks = MOBA_HEADS * per_head

    @pl.when(b == 0)
    def _():
        for c in range(n_chunks):
            chunk_copy(0, c, 0).start()

    @pl.when(b + 1 < n_rows)
    def _():
        for c in range(n_chunks):
            chunk_copy(b + 1, c, 1 - slot).start()

    for c in range(n_chunks):
        chunk_copy(b, c, slot).wait()

    heads = []
    for h in range(MOBA_HEADS):
        acc = jnp.zeros((HEAD_DIM, PAGE_SIZE), F32)
        for tr in range(per_head):
            acc = acc + vbuf[slot, h * per_head + tr] * psel_ref[0, tr, h:h + 1, :]
        heads.append(acc)
    acc_hi, acc_lo = _split_bf16(jnp.concatenate(heads, axis=0))
    ones = jnp.ones((SUBLANES, PAGE_SIZE), BF16)
    past = (_dot_nt(ones, acc_hi) + _dot_nt(ones, acc_lo))[0:1]
    own_head = (lax.broadcasted_iota(jnp.int32, (MOBA_HEADS, MOBA_WIDTH), 1) // HEAD_DIM
                == lax.broadcasted_iota(jnp.int32, (MOBA_HEADS, MOBA_WIDTH), 0))
    p_self = jnp.sum(jnp.where(own_head, _lane_tile(pself_ref[0], MOBA_WIDTH), 0.0), axis=0, keepdims=True)
    o_ref[0] = (past + p_self * vn_ref[0]).astype(o_ref.dtype)


def _sample_specs(n_pages):
    pages_per_block = MOBA_BLOCK // PAGE_SIZE
    kept_pages = MOBA_TOPK * pages_per_block
    row_spec = pl.BlockSpec((1, 1, MOBA_WIDTH), lambda b, *_: (b, 0, 0))
    psel_spec = pl.BlockSpec((1, kept_pages, MOBA_HEADS, PAGE_SIZE), lambda b, *_: (b, 0, 0, 0))
    tile_spec = pl.BlockSpec((1, MOBA_HEADS, LANES), lambda b, *_: (b, 0, 0))
    return kept_pages, row_spec, psel_spec, tile_spec


def _outproj_ffn2_scores_kernel(pt_ref, x1_ref, yp_ref, ya_ref, ym_ref, wout_ref, n2_ref, wgu_ref, wo_ref,
                                qc_ref, q_ref, kn_ref, ck_ref,
                                y_ref, psel_ref, ids_ref, pself_ref,
                                h_s, act_s, kbuf_a, kbuf_b, qb_s, lg_s, sem):
    b = pl.program_id(0)
    n_rows = pl.num_programs(0)
    n_a, n_b = kbuf_a.shape[0], kbuf_b.shape[0]
    per_chunk = -(-(n_a + n_b) // FF_CHUNKS)
    switch_chunk = n_a // per_chunk
    assert n_a == switch_chunk * per_chunk
    issue_b = -(-n_b // switch_chunk)
    issue_a = -(-n_a // (FF_CHUNKS - switch_chunk))
    next_row = jnp.minimum(b + 1, n_rows - 1)

    def copy_a(row, j, zero=0):
        return pltpu.make_async_copy(ck_ref.at[pt_ref[row, j] + zero], kbuf_a.at[j], sem.at[0])

    def copy_b(row, j, zero=0):
        return pltpu.make_async_copy(ck_ref.at[pt_ref[row, n_a + j] + zero], kbuf_b.at[j], sem.at[1])

    @pl.when(b == 0)
    def _():
        for j in range(n_a):
            copy_a(0, j).start()

    qb_s[...] = jnp.broadcast_to(qc_ref[0], qb_s.shape)
    for j in range(n_a):
        copy_a(b, j).wait()

    def side_work(c, act):
        zero_bits = _zero_after(act[:SUBLANES, :PAGE_SIZE])
        zero_idx = pltpu.bitcast(zero_bits, jnp.int32)[0, 0]
        if c < switch_chunk:
            for j in range(c * issue_b, min((c + 1) * issue_b, n_b)):
                copy_b(b, j, zero_idx).start()
        else:
            if c == switch_chunk:
                for j in range(n_b):
                    copy_b(b, j).wait()
            first = (c - switch_chunk) * issue_a
            for j in range(first, min(first + issue_a, n_a)):
                copy_a(next_row, j, zero_idx).start()
        page_ids = range(c * per_chunk, min((c + 1) * per_chunk, n_a + n_b))
        pages = [kbuf_a.at[j] if j < n_a else kbuf_b.at[j - n_a] for j in page_ids]
        for j, tile in zip(page_ids, _score_pages(pages, qb_s, pltpu.bitcast(zero_bits, F32))):
            lg_s[j] = tile

    c1 = POOL_WIDTH
    c2 = c1 + MOBA_WIDTH
    mix = (_dot(yp_ref[...], wout_ref[:c1, :]) + _dot(ya_ref[...], wout_ref[c1:c2, :])
           + _dot(ym_ref[...], wout_ref[c2:, :]))
    x2 = x1_ref[...] + mix
    h_s[...] = _rmsnorm(x2, n2_ref[...]).astype(BF16)
    y_ref[...] = x2 + 0.5 * _swiglu(h_s, wgu_ref, wo_ref, act_s, side_work)
    _select_blocks(lg_s[...], q_ref, kn_ref, psel_ref, ids_ref, pself_ref)

    @pl.when(b == n_rows - 1)
    def _():
        for j in range(n_a):
            copy_a(next_row, j).wait()


def _outproj_ffn2_with_scores(x1, y_pool, y_moba, y_mem, tm, wout, n2, wgu, wo, page_table, q, k_new, cache_k_t):
    n = x1.shape[0]
    n_rows, n_pages = page_table.shape
    assert n // tm == n_rows
    pages_per_block = MOBA_BLOCK // PAGE_SIZE
    assert n_pages % pages_per_block == 0
    n_blocks = n_pages // pages_per_block
    assert LANES % n_blocks == 0 and n_blocks & (n_blocks - 1) == 0 and n_blocks >= MOBA_TOPK
    per_chunk = -(-n_pages // FF_CHUNKS)
    n_a = per_chunk * (FF_CHUNKS // 2)
    kept_pages, row_spec, psel_spec, tile_spec = _sample_specs(n_pages)
    col_spec = pl.BlockSpec((1, MOBA_WIDTH, 1), lambda b, *_: (b, 0, 0))

    def rows(width):
        return pl.BlockSpec((tm, width), lambda i, *_: (i, 0))

    resident = (2 * (wgu.size + wo.size + wout.size) + tm * (D_MODEL + D_FF) * 2
                + (n_pages + 1) * MOBA_WIDTH * PAGE_SIZE * 4)
    streamed = 2 * tm * (4 * 2 * D_MODEL + 2 * D_MODEL)
    return pl.pallas_call(
        _outproj_ffn2_scores_kernel,
        grid_spec=pltpu.PrefetchScalarGridSpec(
            num_scalar_prefetch=1,
            grid=(n_rows,),
            in_specs=[rows(D_MODEL), rows(POOL_WIDTH), rows(MOBA_WIDTH), rows(MEM_WIDTH),
                      _const_spec(wout.shape), _const_spec((1, D_MODEL)), _const_spec(wgu.shape),
                      _const_spec(wo.shape), col_spec, row_spec, row_spec, pl.BlockSpec(memory_space=pl.ANY)],
            out_specs=[rows(D_MODEL), psel_spec, tile_spec, tile_spec],
            scratch_shapes=[pltpu.VMEM((tm, D_MODEL), BF16), pltpu.VMEM((tm, D_FF), BF16),
                            pltpu.VMEM((n_a, MOBA_WIDTH, PAGE_SIZE), F32),
                            pltpu.VMEM((n_pages - n_a, MOBA_WIDTH, PAGE_SIZE), F32),
                            pltpu.VMEM((MOBA_WIDTH, PAGE_SIZE), F32),
                            pltpu.VMEM((n_pages, MOBA_HEADS, PAGE_SIZE), F32),
                            pltpu.SemaphoreType.DMA((2,))]),
        out_shape=[jax.ShapeDtypeStruct((n, D_MODEL), F32),
                   jax.ShapeDtypeStruct((n_rows, kept_pages, MOBA_HEADS, PAGE_SIZE), F32),
                   jax.ShapeDtypeStruct((n_rows, MOBA_HEADS, LANES), jnp.int32),
                   jax.ShapeDtypeStruct((n_rows, MOBA_HEADS, LANES), F32)],
        compiler_params=pltpu.CompilerParams(dimension_semantics=("arbitrary",),
                                             vmem_limit_bytes=_vmem_limit(resident + streamed)),
        name="outproj_ffn2_scores",
    )(page_table, x1, y_pool, y_moba, y_mem, wout, n2, wgu, wo, q.reshape(n_rows, MOBA_WIDTH, 1),
      q.reshape(n_rows, 1, MOBA_WIDTH), k_new.reshape(n_rows, 1, MOBA_WIDTH), cache_k_t)


def _moba_sample_values(page_table, p_sel, ids, p_self, v_new, cache_v_t):
    n_rows, n_pages = page_table.shape
    kept_pages, row_spec, psel_spec, tile_spec = _sample_specs(n_pages)
    any_spec = pl.BlockSpec(memory_space=pl.ANY)
    kept_ids = ids[:, :, :MOBA_TOPK].reshape(n_rows, MOBA_HEADS * MOBA_TOPK)
    return pl.pallas_call(
        _moba_sample_values_kernel,
        grid_spec=pltpu.PrefetchScalarGridSpec(
            num_scalar_prefetch=2,
            grid=(n_rows,),
            in_specs=[psel_spec, tile_spec, row_spec, any_spec],
            out_specs=row_spec,
            scratch_shapes=[pltpu.VMEM((2, MOBA_HEADS * kept_pages, HEAD_DIM, PAGE_SIZE), F32),
                            pltpu.SemaphoreType.DMA((2,))]),
        out_shape=jax.ShapeDtypeStruct((n_rows, 1, MOBA_WIDTH), BF16),
        compiler_params=pltpu.CompilerParams(dimension_semantics=("arbitrary",)),
        name="moba_sample_values",
    )(page_table, kept_ids, p_sel, p_self, v_new.reshape(n_rows, 1, MOBA_WIDTH), cache_v_t)


def _mem_kv_kernel(mem_ref, g_ref, wt_ref, kg_ref, mk_ref, mv_ref):
    h = _rmsnorm(mem_ref[...], g_ref[...]).astype(BF16)
    kv_t = _dot_nt(wt_ref[...], h)
    mk_ref[...] = _head_rmsnorm_t(kv_t[:MEM_WIDTH], _lane_tile(kg_ref[...], h.shape[0]))
    mv_ref[...] = kv_t[MEM_WIDTH:]


def _mem_kv(mem, batch, mem_len, g, w_t, kg_tab):
    feat = pl.BlockSpec((None, MEM_WIDTH, mem_len), lambda b: (b, 0, 0))
    return pl.pallas_call(
        _mem_kv_kernel,
        grid=(batch,),
        in_specs=[pl.BlockSpec((mem_len, D_MODEL), lambda b: (b, 0)), _const_spec((1, D_MODEL)),
                  _const_spec(w_t.shape), _const_spec(kg_tab.shape)],
        out_specs=[feat, feat],
        out_shape=[jax.ShapeDtypeStruct((batch, MEM_WIDTH, mem_len), F32)] * 2,
        compiler_params=pltpu.CompilerParams(dimension_semantics=("parallel",)),
        name="mem_kv",
    )(mem, g, w_t, kg_tab)


def _pool_select(lane, a2, a4, a8, a16):
    return jnp.where(lane < POOL_GROUP, a2,
                     jnp.where(lane < 2 * POOL_GROUP, a4, jnp.where(lane < 3 * POOL_GROUP, a8, a16)))


def _mix_prompt_kernel(u_ref, halo_ref, qm_ref, mk_ref, mv_ref, wbd_ref, scale_ref, yp_ref, ym_ref):
    t = pl.program_id(1)
    tm = u_ref.shape[0]
    halo_rows = halo_ref.shape[0]
    u = u_ref[...]
    ext = jnp.concatenate([jnp.where(t > 0, halo_ref[...], 0.0), u], axis=0)
    a2 = ext + pltpu.roll(ext, 1, 0)
    a4 = a2 + pltpu.roll(a2, 2, 0)
    a8 = a4 + pltpu.roll(a4, 4, 0)
    a16 = a8 + pltpu.roll(a8, 8, 0)
    lane = lax.broadcasted_iota(jnp.int32, u.shape, 1)
    sums = _pool_select(lane, a2[halo_rows:], a4[halo_rows:], a8[halo_rows:], a16[halo_rows:])
    pos = t * tm + lax.broadcasted_iota(jnp.int32, u.shape, 0)
    window = _pool_select(lane, *POOL_WINDOWS)
    cnt = jnp.minimum(window, pos + 1).astype(F32)
    pooled = sums / cnt - u
    yp_ref[...] = (_dot(pooled.astype(BF16), wbd_ref[...]) * scale_ref[...]).astype(yp_ref.dtype)

    mem_len = mk_ref.shape[1]
    row_v = lax.broadcasted_iota(jnp.int32, (LANES, mem_len), 0)
    lane_q = lax.broadcasted_iota(jnp.int32, (tm, LANES), 1)
    for pair in range(MEM_WIDTH // LANES):
        cols = slice(pair * LANES, (pair + 1) * LANES)
        qp = qm_ref[:, cols]
        mk_t = mk_ref[cols, :].astype(BF16)
        mv_t = mv_ref[cols, :]
        out = None
        for h in range(HEADS_PER_VREG):
            in_head = (lane_q < HEAD_DIM) if h == 0 else (lane_q >= HEAD_DIM)
            in_head_v = (row_v < HEAD_DIM) if h == 0 else (row_v >= HEAD_DIM)
            lg = _dot(jnp.where(in_head, qp, 0.0).astype(BF16), mk_t)
            p = jnp.exp(lg - jnp.max(lg, axis=-1, keepdims=True))
            o = _dot_nt(p.astype(BF16), jnp.where(in_head_v, mv_t, 0.0).astype(BF16))
            o = o / jnp.sum(p, axis=-1, keepdims=True)
            out = o if out is None else out + o
        ym_ref[:, cols] = out.astype(ym_ref.dtype)


def _mix_prompt(u, qm, mk_t, mv_t, wbd, scale, batch, seq, tm):
    halo_rows = 16
    assert halo_rows > POOL_BUF and tm % halo_rows == 0 and seq % tm == 0
    tiles = seq // tm
    mem_len = mk_t.shape[2]

    def rows(width):
        return pl.BlockSpec((tm, width), lambda b, t: (b * tiles + t, 0))

    halo = pl.BlockSpec((halo_rows, POOL_WIDTH),
                        lambda b, t: (jnp.maximum((b * tiles + t) * (tm // halo_rows) - 1, 0), 0))
    mem = pl.BlockSpec((None, MEM_WIDTH, mem_len), lambda b, t: (b, 0, 0))
    return pl.pallas_call(
        _mix_prompt_kernel,
        grid=(batch, tiles),
        in_specs=[rows(POOL_WIDTH), halo, rows(MEM_WIDTH), mem, mem, _const_spec(wbd.shape),
                  _const_spec((1, POOL_WIDTH))],
        out_specs=[rows(POOL_WIDTH), rows(MEM_WIDTH)],
        out_shape=[jax.ShapeDtypeStruct((batch * seq, POOL_WIDTH), BF16),
                   jax.ShapeDtypeStruct((batch * seq, MEM_WIDTH), BF16)],
        compiler_params=pltpu.CompilerParams(dimension_semantics=("parallel", "parallel")),
        name="mix_prompt",
    )(u, u, qm, mk_t, mv_t, wbd, scale)


def _mix_sample_kernel(u_ref, st_ref, qm_ref, mk_ref, mv_ref, wbd_ref, scale_ref, yp_ref, ym_ref, *, pos):
    rows = u_ref.shape[0]
    u = u_ref[...]
    tail = u
    sums = {}
    for back in range(1, POOL_BUF + 1):
        tail = tail + st_ref[POOL_BUF - back]
        if back + 1 in POOL_WINDOWS:
            sums[back + 1] = tail / float(min(back + 1, pos + 1))
    lane = lax.broadcasted_iota(jnp.int32, u.shape, 1)
    pooled = _pool_select(lane, *(sums[w] for w in POOL_WINDOWS)) - u
    yp_ref[...] = (_dot(pooled.astype(BF16), wbd_ref[...]) * scale_ref[...]).astype(yp_ref.dtype)

    row_t = lax.broadcasted_iota(jnp.int32, (SCORE_ROWS, MEM_WIDTH), 0)
    own_head = lax.broadcasted_iota(jnp.int32, (SCORE_ROWS, MEM_WIDTH), 1) // HEAD_DIM == row_t
    for r in range(rows):
        q = jnp.broadcast_to(qm_ref[r:r + 1, :], (SCORE_ROWS, MEM_WIDTH))
        q_t = jnp.where(own_head, q, 0.0).astype(BF16)
        lg = _dot(q_t, mk_ref[r].astype(BF16))
        p = jnp.exp(lg - jnp.max(lg, axis=-1, keepdims=True))
        p = p / jnp.sum(p, axis=-1, keepdims=True)
        o = _dot_nt(p.astype(BF16), mv_ref[r].astype(BF16))
        ym_ref[r:r + 1, :] = jnp.sum(jnp.where(own_head, o, 0.0), axis=0, keepdims=True).astype(ym_ref.dtype)


def _mix_sample(u, state_t, qm, mk_t, mv_t, wbd, scale, rows_per_step, pos):
    n = u.shape[0]
    mem_len = mk_t.shape[2]

    def rows(width):
        return pl.BlockSpec((rows_per_step, width), lambda i: (i, 0))

    mem = pl.BlockSpec((rows_per_step, MEM_WIDTH, mem_len), lambda i: (i, 0, 0))
    return pl.pallas_call(
        functools.partial(_mix_sample_kernel, pos=pos),
        grid=(n // rows_per_step,),
        in_specs=[rows(POOL_WIDTH), pl.BlockSpec((POOL_BUF, rows_per_step, POOL_WIDTH), lambda i: (0, i, 0)),
                  rows(MEM_WIDTH), mem, mem, _const_spec(wbd.shape), _const_spec((1, POOL_WIDTH))],
        out_specs=[rows(POOL_WIDTH), rows(MEM_WIDTH)],
        out_shape=[jax.ShapeDtypeStruct((n, POOL_WIDTH), BF16), jax.ShapeDtypeStruct((n, MEM_WIDTH), BF16)],
        compiler_params=pltpu.CompilerParams(dimension_semantics=("parallel",)),
        name="mix_sample",
    )(u, state_t, qm, mk_t, mv_t, wbd, scale)


def _rope_angles(pos):
    inv = jnp.power(jnp.float32(ROPE_THETA), -jnp.arange(ROT_HALF, dtype=F32) * (2.0 / ROT_DIM))
    return pos.astype(F32)[:, None] * inv[None, :]


def _rope_tables(pos):
    ang = _rope_angles(pos)
    cos, sin = jnp.cos(ang), jnp.sin(ang)
    rest = HEAD_DIM - ROT_DIM
    ones = jnp.ones((pos.shape[0], rest), F32)
    zeros = jnp.zeros((pos.shape[0], rest), F32)
    cos_h = jnp.concatenate([cos, cos, ones], axis=1)
    sin_h = jnp.concatenate([-sin, sin, zeros], axis=1)
    return jnp.tile(cos_h, (1, HEADS_PER_VREG)), jnp.tile(sin_h, (1, HEADS_PER_VREG))


def _block_diag_ones(width):
    head = jnp.arange(width) // HEAD_DIM
    return (head[:, None] == head[None, :]).astype(BF16)


def _tile_gain(g, width):
    return jnp.tile(g, width // HEAD_DIM).reshape(1, width)


def _gain_table(g, width):
    return jnp.broadcast_to(jnp.tile(g, width // HEAD_DIM)[:, None], (width, LANES))


def _feature_major_to_heads(a_t, n_heads):
    batch, _, tokens = a_t.shape
    return jnp.transpose(a_t.reshape(batch, n_heads, HEAD_DIM, tokens), (0, 3, 1, 2))[None]


def kernel(x_prompt, x_sample, mem_prompt, cache_k, cache_v, cache_mem_k, cache_mem_v, state_pool, page_table,
           ffn1_norm, ffn1_w_in, ffn1_w_out, mix_norm, w_in, pool_w, pool_scale, q_norm, k_norm,
           mem_norm, mem_w_kv, mem_q_norm, mem_k_norm, w_out, ffn2_norm, ffn2_w_in, ffn2_w_out):
    depth = ffn1_norm.shape[0]
    assert depth == 1
    n_p, t_p, _ = x_prompt.shape
    n_s, t_s, _ = x_sample.shape
    assert t_s == 1
    past_len = page_table.shape[1] * PAGE_SIZE
    mem_len = mem_prompt.shape[1]
    l = 0

    def interleave_gate_up(w):
        gate_up = w.astype(BF16).reshape(D_MODEL, 2, D_FF // LANES, LANES)
        return jnp.swapaxes(gate_up, 1, 2).reshape(D_MODEL, 2 * D_FF)

    wgu1, wo1 = interleave_gate_up(ffn1_w_in[l]), ffn1_w_out[l].astype(BF16)
    win = w_in[l].astype(BF16)
    c1 = POOL_WIDTH
    c4 = c1 + 3 * MOBA_WIDTH
    wnat = jnp.concatenate([win[:, :c1], win[:, c4:]], axis=1)
    wqkv_t = win[:, c1:c4].T
    wkv_t = mem_w_kv[l].astype(BF16).T
    n1, nmix, n2, nmem = (a[l].reshape(1, D_MODEL) for a in (ffn1_norm, mix_norm, ffn2_norm, mem_norm))
    qg, kg = _tile_gain(q_norm[l], MOBA_WIDTH), _tile_gain(k_norm[l], MOBA_WIDTH)
    mqg = _tile_gain(mem_q_norm[l], MEM_WIDTH)
    qg_tab, kg_tab = _gain_table(q_norm[l], MOBA_WIDTH), _gain_table(k_norm[l], MOBA_WIDTH)
    mkg_tab = _gain_table(mem_k_norm[l], MEM_WIDTH)
    ones_bd = _block_diag_ones(MOBA_WIDTH)
    ones_mem = ones_bd[:MEM_WIDTH, :MEM_WIDTH]
    wbd = jax.scipy.linalg.block_diag(*[pool_w[l, g] for g in range(len(POOL_WINDOWS))]).astype(BF16)
    scale = pool_scale[l].reshape(1, POOL_WIDTH)

    xp = x_prompt.reshape(n_p * t_p, D_MODEL)
    ang_p = _rope_angles(jnp.arange(t_p, dtype=jnp.int32)).T
    x1p, up, qmp, qt_p, kt_p, vt_p, wgu2, wo2, wout = _ffn1_inproj_t(
        xp, n_p, t_p, ROW_TILE, n1, wgu1, wo1, nmix, wnat, wqkv_t, qg_tab, kg_tab, mqg, ones_mem,
        jnp.cos(ang_p), jnp.sin(ang_p), ffn2_w_in[l], ffn2_w_out[l], w_out[l])
    mkt_p, mvt_p = _mem_kv(mem_prompt.reshape(n_p * mem_len, D_MODEL), n_p, mem_len, nmem, wkv_t, mkg_tab)
    ya_p = _moba_prompt(qt_p, kt_p, vt_p)
    yp_p, ym_p = _mix_prompt(up, qmp, mkt_p, mvt_p, wbd, scale, n_p, t_p, ROW_TILE)

    xs = x_sample.reshape(n_s, D_MODEL)
    cos_s, sin_s = _rope_tables(jnp.full((n_s,), past_len, jnp.int32))
    x1s, us, qs, ks, vs, qms = _ffn1_inproj(xs, n1, wgu1, wo1, nmix, win, qg, kg, mqg, ones_bd, cos_s, sin_s)

    def feature_major(cache, n_heads):
        rows, tokens = cache.shape[1], cache.shape[2]
        return jnp.transpose(cache[l], (0, 2, 3, 1)).reshape(rows, n_heads * HEAD_DIM, tokens)

    y_prompt, p_sel, kept_ids, p_self = _outproj_ffn2_with_scores(
        x1p, yp_p, ya_p, ym_p, ROW_TILE, wout, n2, wgu2, wo2, page_table, qs, ks, feature_major(cache_k, MOBA_HEADS))
    ya_s = _moba_sample_values(page_table, p_sel, kept_ids, p_self, vs,
                               feature_major(cache_v, MOBA_HEADS)).reshape(n_s, MOBA_WIDTH)
    state = state_pool[l]
    yp_s, ym_s = _mix_sample(us, jnp.swapaxes(state, 0, 1), qms, feature_major(cache_mem_k, MEM_HEADS),
                             feature_major(cache_mem_v, MEM_HEADS), wbd, scale, 8, past_len)
    y_sample = _outproj_ffn2(x1s, yp_s, ya_s, ym_s, n_s, wout, n2, wgu2, wo2)

    pool_state_p = up.reshape(n_p, t_p, POOL_WIDTH)[:, t_p - POOL_BUF:][None]
    pool_state_s = jnp.concatenate([state[:, 1:], us[:, None, :]], axis=1)[None]
    return (y_prompt.reshape(n_p, t_p, D_MODEL), y_sample.reshape(n_s, t_s, D_MODEL),
            _feature_major_to_heads(kt_p, MOBA_HEADS), _feature_major_to_heads(vt_p, MOBA_HEADS),
            _feature_major_to_heads(mkt_p, MEM_HEADS), _feature_major_to_heads(mvt_p, MEM_HEADS),
            pool_state_p,
            ks.reshape(1, n_s, t_s, MOBA_HEADS, HEAD_DIM), vs.reshape(1, n_s, t_s, MOBA_HEADS, HEAD_DIM),
            pool_state_s)
```

```python
import functools

import jax
import jax.numpy as jnp
from jax import lax
from jax.experimental import pallas as pl
from jax.experimental.pallas import tpu as pltpu

F32 = jnp.float32
BF16 = jnp.bfloat16

D_MODEL = 1024
HEAD_DIM = 64
POOL_WIDTH = 256
POOL_WINDOWS = (2, 4, 8, 16)
POOL_GROUP = POOL_WIDTH // len(POOL_WINDOWS)
POOL_BUF = max(POOL_WINDOWS) - 1
MOBA_WIDTH = 512
MOBA_HEADS = MOBA_WIDTH // HEAD_DIM
MOBA_BLOCK = 256
MOBA_TOPK = 3
MEM_WIDTH = 256
MEM_HEADS = MEM_WIDTH // HEAD_DIM
ROT_DIM = HEAD_DIM // 4
ROT_HALF = ROT_DIM // 2
ROPE_THETA = 500000.0
D_FF = 2816
RMS_EPS = 1e-6
PAGE_SIZE = 128
ATTN_SCALE = HEAD_DIM ** -0.5
LOG2_E = 1.4426950408889634

LANES = 128
SUBLANES = 8
V7X_VMEM_BYTES = 64 * 1024 * 1024

HEADS_PER_VREG = LANES // HEAD_DIM
MASKED = -1e30
FF_CHUNK = 256
FF_CHUNKS = D_FF // FF_CHUNK
ROW_TILE = 512
SCORE_ROWS = 16


def _dot(a, b):
    return jnp.dot(a, b, preferred_element_type=F32)


def _dot_nt(a, b):
    return lax.dot_general(a, b, (((1,), (1,)), ((), ())), preferred_element_type=F32)


def _split_bf16(x):
    hi = x.astype(BF16)
    lo = (x - hi.astype(F32)).astype(BF16)
    return hi, lo


def _vmem_limit(resident_bytes):
    return int(min(resident_bytes + 16 * 1024 * 1024, V7X_VMEM_BYTES - 4 * 1024 * 1024))


def _rmsnorm(x, g):
    ms = jnp.mean(x * x, axis=-1, keepdims=True)
    return x * lax.rsqrt(ms + RMS_EPS) * g


def _head_rmsnorm(x, g, ones_bd):
    hi, lo = _split_bf16(x * x)
    ms = (_dot(hi, ones_bd) + _dot(lo, ones_bd)) * (1.0 / HEAD_DIM)
    return x * lax.rsqrt(ms + RMS_EPS) * g


def _head_rmsnorm_t(x, g):
    heads = []
    for r in range(0, x.shape[0], HEAD_DIM):
        xh = x[r:r + HEAD_DIM]
        ms = jnp.mean(xh * xh, axis=0, keepdims=True)
        heads.append(xh * lax.rsqrt(ms + RMS_EPS))
    return jnp.concatenate(heads, axis=0) * g


def _rope(x, cos, sin):
    width = x.shape[1]
    lane = lax.broadcasted_iota(jnp.int32, x.shape, 1)
    first_half = (lane & (HEAD_DIM - 1)) < ROT_HALF
    partner = jnp.where(first_half, pltpu.roll(x, width - ROT_HALF, 1), pltpu.roll(x, ROT_HALF, 1))
    return x * cos + partner * sin


def _rope_t(x, cos, sin):
    pieces = []
    for r in range(0, x.shape[0], HEAD_DIM):
        x1, x2 = x[r:r + ROT_HALF], x[r + ROT_HALF:r + ROT_DIM]
        pieces += [x1 * cos - x2 * sin, x2 * cos + x1 * sin, x[r + ROT_DIM:r + HEAD_DIM]]
    return jnp.concatenate(pieces, axis=0)


def _lane_tile(table, n):
    return jnp.concatenate([table] * (n // LANES), axis=1)


def _swiglu(h_s, wgu_ref, wo_ref, act_s, side_work=None):
    for c in range(FF_CHUNKS):
        lo = c * FF_CHUNK
        gate = _dot(h_s[...], wgu_ref[:, lo:lo + FF_CHUNK])
        up = _dot(h_s[...], wgu_ref[:, D_FF + lo:D_FF + lo + FF_CHUNK])
        act = jax.nn.silu(gate) * up
        act_s[:, lo:lo + FF_CHUNK] = act.astype(BF16)
        if side_work is not None:
            side_work(c, act)
    return _dot(act_s[...], wo_ref[...])


def _const_spec(shape):
    return pl.BlockSpec(shape, lambda *_: (0,) * len(shape), pipeline_mode=pl.Buffered(1))


def _ffn1_trunk(x_ref, n1_ref, wgu_ref, wo_ref, nmix_ref, x1_ref, h_s, act_s):
    h_s[...] = _rmsnorm(x_ref[...], n1_ref[...]).astype(BF16)
    x1_ref[...] = x_ref[...] + 0.5 * _swiglu(h_s, wgu_ref, wo_ref, act_s)
    h_s[...] = _rmsnorm(x1_ref[...], nmix_ref[...]).astype(BF16)


def _ffn1_inproj_kernel(x_ref, n1_ref, wgu_ref, wo_ref, nmix_ref, win_ref, qg_ref, kg_ref, mg_ref,
                        ones_ref, cos_ref, sin_ref,
                        x1_ref, u_ref, q_ref, k_ref, v_ref, qm_ref, h_s, act_s):
    _ffn1_trunk(x_ref, n1_ref, wgu_ref, wo_ref, nmix_ref, x1_ref, h_s, act_s)
    c1 = POOL_WIDTH
    c2 = c1 + MOBA_WIDTH
    c3 = c2 + MOBA_WIDTH
    c4 = c3 + MOBA_WIDTH
    u_ref[...] = _dot(h_s[...], win_ref[:, :c1])
    q = _dot(h_s[...], win_ref[:, c1:c2])
    k = _dot(h_s[...], win_ref[:, c2:c3])
    v_ref[...] = _dot(h_s[...], win_ref[:, c3:c4])
    qm = _dot(h_s[...], win_ref[:, c4:])
    reps = MOBA_WIDTH // LANES
    cos = jnp.concatenate([cos_ref[...]] * reps, axis=1)
    sin = jnp.concatenate([sin_ref[...]] * reps, axis=1)
    ones = ones_ref[...]
    q_ref[...] = _rope(_head_rmsnorm(q, qg_ref[...], ones), cos, sin) * ATTN_SCALE
    k_ref[...] = _rope(_head_rmsnorm(k, kg_ref[...], ones), cos, sin)
    qm_ref[...] = _head_rmsnorm(qm, mg_ref[...], ones_ref[:MEM_WIDTH, :MEM_WIDTH]) * ATTN_SCALE


def _ffn1_inproj(x, n1, wgu, wo, nmix, win, qg, kg, mg, ones_bd, cos_tab, sin_tab):
    n = x.shape[0]

    def full(width):
        return pl.BlockSpec((n, width), lambda i: (0, 0))

    widths = (D_MODEL, POOL_WIDTH, MOBA_WIDTH, MOBA_WIDTH, MOBA_WIDTH, MEM_WIDTH)
    resident = 2 * (wgu.size + wo.size + win.size + ones_bd.size)
    return pl.pallas_call(
        _ffn1_inproj_kernel,
        grid=(1,),
        in_specs=[full(D_MODEL), _const_spec((1, D_MODEL)), _const_spec(wgu.shape), _const_spec(wo.shape),
                  _const_spec((1, D_MODEL)), _const_spec(win.shape),
                  _const_spec((1, MOBA_WIDTH)), _const_spec((1, MOBA_WIDTH)), _const_spec((1, MEM_WIDTH)),
                  _const_spec(ones_bd.shape), full(LANES), full(LANES)],
        out_specs=[full(w) for w in widths],
        out_shape=[jax.ShapeDtypeStruct((n, w), F32) for w in widths],
        scratch_shapes=[pltpu.VMEM((n, D_MODEL), BF16), pltpu.VMEM((n, D_FF), BF16)],
        compiler_params=pltpu.CompilerParams(dimension_semantics=("arbitrary",),
                                             vmem_limit_bytes=_vmem_limit(resident)),
        name="ffn1_inproj_sample",
    )(x, n1, wgu, wo, nmix, win, qg, kg, mg, ones_bd, cos_tab, sin_tab)


def _ffn1_inproj_t_kernel(x_ref, n1_ref, wgu_ref, wo_ref, nmix_ref, wnat_ref, wqkv_t_ref, qg_ref, kg_ref, mg_ref,
                          ones_ref, cos_ref, sin_ref, wgu2_f32_ref, wo2_f32_ref, wout_f32_ref,
                          x1_ref, u_ref, qm_ref, qt_ref, kt_ref, vt_ref, wgu2_ref, wo2_ref, wout_ref, h_s, act_s):
    wgu2_ref[...] = wgu2_f32_ref[...].astype(BF16)
    wo2_ref[...] = wo2_f32_ref[...].astype(BF16)
    wout_ref[...] = wout_f32_ref[...].astype(BF16)
    _ffn1_trunk(x_ref, n1_ref, wgu_ref, wo_ref, nmix_ref, x1_ref, h_s, act_s)
    tm = x_ref.shape[0]
    u_ref[...] = _dot(h_s[...], wnat_ref[:, :POOL_WIDTH])
    qm = _dot(h_s[...], wnat_ref[:, POOL_WIDTH:])
    qm_ref[...] = _head_rmsnorm(qm, mg_ref[...], ones_ref[...]) * ATTN_SCALE
    cos, sin = cos_ref[...], sin_ref[...]
    q_t = _dot_nt(wqkv_t_ref[:MOBA_WIDTH, :], h_s[...])
    qt_ref[...] = _rope_t(_head_rmsnorm_t(q_t, _lane_tile(qg_ref[...], tm)), cos, sin) * (ATTN_SCALE * LOG2_E)
    k_t = _dot_nt(wqkv_t_ref[MOBA_WIDTH:2 * MOBA_WIDTH, :], h_s[...])
    kt_ref[...] = _rope_t(_head_rmsnorm_t(k_t, _lane_tile(kg_ref[...], tm)), cos, sin)
    vt_ref[...] = _dot_nt(wqkv_t_ref[2 * MOBA_WIDTH:, :], h_s[...])


def _ffn1_inproj_t(x, batch, seq, tm, n1, wgu, wo, nmix, wnat, wqkv_t, qg_tab, kg_tab, mg, ones_bd, cos_t, sin_t,
                   wgu2_f32, wo2_f32, wout_f32):
    tiles = seq // tm
    steps = batch * tiles

    def rows(width):
        return pl.BlockSpec((tm, width), lambda i: (i, 0))

    def slab(w, bf16_rows=16):
        n_slabs = steps
        while w.shape[0] % n_slabs or (w.shape[0] // n_slabs) % bf16_rows:
            n_slabs //= 2
        return pl.BlockSpec((w.shape[0] // n_slabs, w.shape[1]), lambda i, n=n_slabs: (jnp.minimum(i, n - 1), 0))

    feat = pl.BlockSpec((None, MOBA_WIDTH, tm), lambda i: (i // tiles, 0, i % tiles))
    tab = pl.BlockSpec((ROT_HALF, tm), lambda i: (0, i % tiles))
    casts = (wgu2_f32, wo2_f32, wout_f32)
    resident = 2 * (wgu.size + wo.size + wnat.size + wqkv_t.size + ones_bd.size) + tm * (D_MODEL + D_FF) * 2
    streamed = (2 * 4 * tm * (2 * D_MODEL + POOL_WIDTH + MEM_WIDTH + 3 * MOBA_WIDTH + 2 * ROT_HALF)
                + 2 * 6 * sum(w.size for w in casts) // steps)
    feat_shape = jax.ShapeDtypeStruct((batch, MOBA_WIDTH, seq), F32)
    return pl.pallas_call(
        _ffn1_inproj_t_kernel,
        grid=(steps,),
        in_specs=[rows(D_MODEL), _const_spec((1, D_MODEL)), _const_spec(wgu.shape), _const_spec(wo.shape),
                  _const_spec((1, D_MODEL)), _const_spec(wnat.shape), _const_spec(wqkv_t.shape),
                  _const_spec(qg_tab.shape), _const_spec(kg_tab.shape), _const_spec((1, MEM_WIDTH)),
                  _const_spec(ones_bd.shape), tab, tab] + [slab(w) for w in casts],
        out_specs=[rows(D_MODEL), rows(POOL_WIDTH), rows(MEM_WIDTH), feat, feat, feat] + [slab(w) for w in casts],
        out_shape=[jax.ShapeDtypeStruct((batch * seq, w), F32) for w in (D_MODEL, POOL_WIDTH, MEM_WIDTH)]
        + [feat_shape] * 3 + [jax.ShapeDtypeStruct(w.shape, BF16) for w in casts],
        scratch_shapes=[pltpu.VMEM((tm, D_MODEL), BF16), pltpu.VMEM((tm, D_FF), BF16)],
        compiler_params=pltpu.CompilerParams(dimension_semantics=("arbitrary",),
                                             vmem_limit_bytes=_vmem_limit(resident + streamed)),
        name="ffn1_inproj",
    )(x, n1, wgu, wo, nmix, wnat, wqkv_t, qg_tab, kg_tab, mg, ones_bd, cos_t, sin_t, *casts)


def _outproj_ffn2_kernel(x1_ref, yp_ref, ya_ref, ym_ref, wout_ref, n2_ref, wgu_ref, wo_ref, y_ref, h_s, act_s):
    c1 = POOL_WIDTH
    c2 = c1 + MOBA_WIDTH
    mix = (_dot(yp_ref[...], wout_ref[:c1, :]) + _dot(ya_ref[...], wout_ref[c1:c2, :])
           + _dot(ym_ref[...], wout_ref[c2:, :]))
    x2 = x1_ref[...] + mix
    h_s[...] = _rmsnorm(x2, n2_ref[...]).astype(BF16)
    y_ref[...] = x2 + 0.5 * _swiglu(h_s, wgu_ref, wo_ref, act_s)


def _outproj_ffn2(x1, y_pool, y_moba, y_mem, tm, wout, n2, wgu, wo):
    n = x1.shape[0]

    def rows(width):
        return pl.BlockSpec((tm, width), lambda i: (i, 0))

    resident = 2 * (wgu.size + wo.size + wout.size) + tm * (D_MODEL + D_FF) * 2
    streamed = 2 * tm * (4 * 2 * D_MODEL + 2 * D_MODEL)
    return pl.pallas_call(
        _outproj_ffn2_kernel,
        grid=(n // tm,),
        in_specs=[rows(D_MODEL), rows(POOL_WIDTH), rows(MOBA_WIDTH), rows(MEM_WIDTH),
                  _const_spec(wout.shape), _const_spec((1, D_MODEL)), _const_spec(wgu.shape), _const_spec(wo.shape)],
        out_specs=rows(D_MODEL),
        out_shape=jax.ShapeDtypeStruct((n, D_MODEL), F32),
        scratch_shapes=[pltpu.VMEM((tm, D_MODEL), BF16), pltpu.VMEM((tm, D_FF), BF16)],
        compiler_params=pltpu.CompilerParams(dimension_semantics=("parallel",),
                                             vmem_limit_bytes=_vmem_limit(resident + streamed)),
        name="outproj_ffn2",
    )(x1, y_pool, y_moba, y_mem, wout, n2, wgu, wo)


def _moba_prompt_kernel(qt_ref, kt_ref, vt_ref, o_ref, kb_s, vb_s, km_s, s_s, p_s):
    seq = kt_ref.shape[1]
    n_blocks = seq // MOBA_BLOCK
    blocks = [slice(n * MOBA_BLOCK, (n + 1) * MOBA_BLOCK) for n in range(n_blocks)]
    lane_k = lax.broadcasted_iota(jnp.int32, (MOBA_BLOCK, LANES), 1)
    km_s[...] = jnp.zeros_like(km_s)
    for n in range(n_blocks):
        k_n = kt_ref[:, blocks[n]].T
        km_s[n:n + 1, :] = jnp.mean(k_n, axis=0, keepdims=True)
        for h in range(HEADS_PER_VREG):
            own = (lane_k < HEAD_DIM) if h == 0 else (lane_k >= HEAD_DIM)
            tag_lane = (1 - h) * HEAD_DIM + n
            kb_s[h, blocks[n], :] = jnp.where(own, k_n, jnp.where(lane_k == tag_lane, 1.0, 0.0)).astype(BF16)
    vb_s[...] = vt_ref[...].astype(BF16)

    lane_m = lax.broadcasted_iota(jnp.int32, km_s.shape, 1)
    kmean = km_s[...]
    km_split = [_split_bf16(jnp.where((lane_m < HEAD_DIM) if h == 0 else (lane_m >= HEAD_DIM), kmean, 0.0))
                for h in range(HEADS_PER_VREG)]
    blk = lax.broadcasted_iota(jnp.int32, (SUBLANES, MOBA_BLOCK), 0)
    key_id = lax.broadcasted_iota(jnp.int32, (MOBA_BLOCK, MOBA_BLOCK), 0)
    qry_id = lax.broadcasted_iota(jnp.int32, (MOBA_BLOCK, MOBA_BLOCK), 1)
    spare_rows = HEAD_DIM - SUBLANES

    def tile_rows(i, j):
        start = MOBA_BLOCK * (i * (i + 1) // 2 + j)
        return slice(start, start + MOBA_BLOCK)

    def query_operand(i, h):
        q_t = qt_ref[:, blocks[i]]
        if i == 0:
            bias = jnp.zeros((SUBLANES, MOBA_BLOCK), F32)
        else:
            q_hi, q_lo = _split_bf16(q_t)
            km_hi, km_lo = km_split[h]
            s = (_dot(km_hi, q_hi) + _dot(km_lo, q_hi) + _dot(km_hi, q_lo))[:SUBLANES]
            s = jnp.where(blk < i, s, -jnp.inf)
            rank = jnp.zeros(s.shape, jnp.int32)
            for m in range(i):
                sm = s[m:m + 1, :]
                beats = (sm > s) | ((sm == s) & (blk > m))
                rank = rank + beats.astype(jnp.int32)
            allowed = ((blk < i) & (rank < MOBA_TOPK)) | (blk == i)
            bias = jnp.where(allowed, 0.0, MASKED)
        zeros = jnp.zeros((spare_rows, MOBA_BLOCK), F32)
        if h == 0:
            return jnp.concatenate([q_t[:HEAD_DIM], bias, zeros], axis=0).astype(BF16)
        return jnp.concatenate([bias, zeros, q_t[HEAD_DIM:]], axis=0).astype(BF16)

    def logits_tile(i, h, j, q_aug):
        st = _dot(kb_s[h, blocks[j], :], q_aug)
        if j == i:
            st = jnp.where(key_id <= qry_id, st, MASKED)
        s_s[h, tile_rows(i, j), :] = st
        return jnp.max(st, axis=0, keepdims=True)

    def prob_tile(i, h, j, m_row):
        p = jnp.exp2(s_s[h, tile_rows(i, j), :] - m_row)
        p_s[h, tile_rows(i, j), :] = p.astype(BF16)
        return jnp.sum(p, axis=0, keepdims=True)

    def weighted_values(i, h, l_row):
        keys = slice(tile_rows(i, 0).start, tile_rows(i, i).stop)
        acc = _dot(vb_s[h * HEAD_DIM:(h + 1) * HEAD_DIM, :(i + 1) * MOBA_BLOCK], p_s[h, keys, :])
        return acc / l_row

    groups = [(i, h) for i in range(n_blocks) for h in range(HEADS_PER_VREG)]

    def combine(acc, new, op):
        return new if acc is None else op(acc, new)

    q_next = query_operand(*groups[0])
    m_next = None
    for j in range(groups[0][0] + 1):
        m_next = combine(m_next, logits_tile(*groups[0], j, q_next), jnp.maximum)
    outs = []
    for g, (i, h) in enumerate(groups):
        m_row, m_next, l_row = m_next, None, None
        nxt = groups[g + 1] if g + 1 < len(groups) else None
        if nxt is not None:
            q_next = query_operand(*nxt)
        for j in range(max(i + 1, nxt[0] + 1 if nxt is not None else 0)):
            if j <= i:
                l_row = combine(l_row, prob_tile(i, h, j, m_row), jnp.add)
            if nxt is not None and j <= nxt[0]:
                m_next = combine(m_next, logits_tile(*nxt, j, q_next), jnp.maximum)
        outs.append(weighted_values(i, h, l_row))
        if h == HEADS_PER_VREG - 1:
            o_ref[blocks[i], :] = jnp.concatenate(outs, axis=0).T.astype(o_ref.dtype)
            outs = []


def _moba_prompt(q_t, k_t, v_t):
    batch, _, seq = k_t.shape
    n_blocks = seq // MOBA_BLOCK
    pairs = MOBA_WIDTH // LANES
    assert n_blocks <= SUBLANES and n_blocks <= HEAD_DIM
    tile_rows = MOBA_BLOCK * (n_blocks * (n_blocks + 1) // 2)
    feat = pl.BlockSpec((None, LANES, seq), lambda b, p: (b, p, 0))
    return pl.pallas_call(
        _moba_prompt_kernel,
        grid=(batch, pairs),
        in_specs=[feat, feat, feat],
        out_specs=pl.BlockSpec((seq, LANES), lambda b, p: (b, p)),
        out_shape=jax.ShapeDtypeStruct((batch * seq, MOBA_WIDTH), BF16),
        scratch_shapes=[pltpu.VMEM((HEADS_PER_VREG, seq, LANES), BF16),
                        pltpu.VMEM((LANES, seq), BF16),
                        pltpu.VMEM((SCORE_ROWS, LANES), F32),
                        pltpu.VMEM((HEADS_PER_VREG, tile_rows, MOBA_BLOCK), F32),
                        pltpu.VMEM((HEADS_PER_VREG, tile_rows, MOBA_BLOCK), BF16)],
        compiler_params=pltpu.CompilerParams(
            dimension_semantics=("parallel", "parallel"),
            vmem_limit_bytes=_vmem_limit(HEADS_PER_VREG * tile_rows * MOBA_BLOCK * 6 + 8 * LANES * seq * 4)),
        name="moba_prompt",
    )(q_t, k_t, v_t)


def _zero_after(value):
    return lax.shift_right_logical(pltpu.bitcast(value, jnp.uint32), jnp.uint32(32))


def _score_pages(pages, qb_s, zero):
    head_row = lax.broadcasted_iota(jnp.int32, (MOBA_HEADS, PAGE_SIZE), 0)
    tiles = [jnp.zeros((MOBA_HEADS, PAGE_SIZE), F32) for _ in pages]
    for h in range(MOBA_HEADS):
        rows = slice(h * HEAD_DIM, (h + 1) * HEAD_DIM)
        q_h = (qb_s[rows, :].reshape(HEAD_DIM // SUBLANES, SUBLANES, PAGE_SIZE) + zero).reshape(HEAD_DIM, PAGE_SIZE)
        for n, page in enumerate(pages):
            logit = jnp.sum(page[rows, :] * q_h, axis=0, keepdims=True)
            tiles[n] = jnp.where(head_row == h, logit, tiles[n])
    return tiles


def _select_blocks(lg, q_ref, kn_ref, psel_ref, ids_ref, pself_ref):
    n_pages = lg.shape[0]
    pages_per_block = MOBA_BLOCK // PAGE_SIZE
    n_blocks = n_pages // pages_per_block

    blk = lg.reshape(n_blocks, pages_per_block, MOBA_HEADS, PAGE_SIZE)
    blk_sum = blk[:, 0]
    for r in range(1, pages_per_block):
        blk_sum = blk_sum + blk[:, r]
    score_col = jnp.sum(blk_sum, axis=-1, keepdims=True) * (1.0 / MOBA_BLOCK)
    lane = lax.broadcasted_iota(jnp.int32, (MOBA_HEADS, LANES), 1)
    blk_of_lane = lane & (n_blocks - 1)
    score = jnp.zeros((MOBA_HEADS, LANES), F32)
    for n in range(n_blocks):
        score = jnp.where(blk_of_lane == n, score_col[n], score)
    rank = jnp.zeros(score.shape, jnp.int32)
    for k in range(1, n_blocks):
        other = pltpu.roll(score, k, 1)
        beats = (other > score) | ((other == score) & (blk_of_lane >= k))
        rank = rank + beats.astype(jnp.int32)
    bias = jnp.where(rank < MOBA_TOPK, 0.0, MASKED)

    q = q_ref[0]
    own_head = (lax.broadcasted_iota(jnp.int32, (MOBA_HEADS, MOBA_WIDTH), 1) // HEAD_DIM
                == lax.broadcasted_iota(jnp.int32, (MOBA_HEADS, MOBA_WIDTH), 0))
    lg_self = jnp.sum(jnp.where(own_head, q, 0.0) * kn_ref[0], axis=-1, keepdims=True)

    page_bias = [jnp.broadcast_to(bias[:, n:n + 1], (MOBA_HEADS, PAGE_SIZE)) for n in range(n_blocks)]
    lg = blk + jnp.stack(page_bias)[:, None]
    m = jnp.maximum(jnp.max(jnp.max(jnp.max(lg, axis=0), axis=0), axis=-1, keepdims=True), lg_self)
    p = jnp.exp(lg - m)
    p_self = jnp.exp(lg_self - m)
    denom = jnp.sum(jnp.sum(jnp.sum(p, axis=0), axis=0), axis=-1, keepdims=True) + p_self
    p = p / denom
    pself_ref[0] = jnp.broadcast_to(p_self / denom, (MOBA_HEADS, LANES))

    first_period = lane < n_blocks
    ids = jnp.zeros((MOBA_HEADS, LANES), F32)
    for t in range(MOBA_TOPK):
        is_t = jnp.where(rank == t, 1.0, 0.0)
        for r in range(pages_per_block):
            kept = jnp.zeros((MOBA_HEADS, PAGE_SIZE), F32)
            for n in range(n_blocks):
                kept = kept + is_t[:, n:n + 1] * p[n, r]
            psel_ref[0, t * pages_per_block + r] = kept
        blk_id = jnp.sum(jnp.where(first_period, is_t * blk_of_lane.astype(F32), 0.0), axis=-1, keepdims=True)
        ids = jnp.where(lane == t, blk_id, ids)
    ids_ref[0] = ids.astype(jnp.int32)


def _moba_sample_values_kernel(pt_ref, ids_ref, psel_ref, pself_ref, vn_ref, cv_ref, o_ref, vbuf, sem):
    b = pl.program_id(0)
    n_rows = pl.num_programs(0)
    pages_per_block = MOBA_BLOCK // PAGE_SIZE
    per_head = MOBA_TOPK * pages_per_block
    slot = lax.rem(b, 2)

    def chunk_copy(row, c, s):
        h, t, r = c // per_head, (c % per_head) // pages_per_block, c % pages_per_block
        page = pt_ref[row, ids_ref[row, h * MOBA_TOPK + t] * pages_per_block + r]
        return pltpu.make_async_copy(cv_ref.at[page, pl.ds(h * HEAD_DIM, HEAD_DIM), :], vbuf.at[s, c], sem.at[s])

    n_chunks = MOBA_HEADS * per_head

    @pl.when(b == 0)
    def _():
        for c in range(n_chunks):
            chunk_copy(0, c, 0).start()

    @pl.when(b + 1 < n_rows)
    def _():
        for c in range(n_chunks):
            chunk_copy(b + 1, c, 1 - slot).start()

    for c in range(n_chunks):
        chunk_copy(b, c, slot).wait()

    heads = []
    for h in range(MOBA_HEADS):
        acc = jnp.zeros((HEAD_DIM, PAGE_SIZE), F32)
        for tr in range(per_head):
            acc = acc + vbuf[slot, h * per_head + tr] * psel_ref[0, tr, h:h + 1, :]
        heads.append(acc)
    acc_hi, acc_lo = _split_bf16(jnp.concatenate(heads, axis=0))
    ones = jnp.ones((SUBLANES, PAGE_SIZE), BF16)
    past = (_dot_nt(ones, acc_hi) + _dot_nt(ones, acc_lo))[0:1]
    own_head = (lax.broadcasted_iota(jnp.int32, (MOBA_HEADS, MOBA_WIDTH), 1) // HEAD_DIM
                == lax.broadcasted_iota(jnp.int32, (MOBA_HEADS, MOBA_WIDTH), 0))
    p_self = jnp.sum(jnp.where(own_head, _lane_tile(pself_ref[0], MOBA_WIDTH), 0.0), axis=0, keepdims=True)
    o_ref[0] = (past + p_self * vn_ref[0]).astype(o_ref.dtype)


def _sample_specs(n_pages):
    pages_per_block = MOBA_BLOCK // PAGE_SIZE
    kept_pages = MOBA_TOPK * pages_per_block
    row_spec = pl.BlockSpec((1, 1, MOBA_WIDTH), lambda b, *_: (b, 0, 0))
    psel_spec = pl.BlockSpec((1, kept_pages, MOBA_HEADS, PAGE_SIZE), lambda b, *_: (b, 0, 0, 0))
    tile_spec = pl.BlockSpec((1, MOBA_HEADS, LANES), lambda b, *_: (b, 0, 0))
    return kept_pages, row_spec, psel_spec, tile_spec


def _outproj_ffn2_scores_kernel(pt_ref, x1_ref, yp_ref, ya_ref, ym_ref, wout_ref, n2_ref, wgu_ref, wo_ref,
                                qc_ref, q_ref, kn_ref, ck_ref,
                                y_ref, psel_ref, ids_ref, pself_ref,
                                h_s, act_s, kbuf_a, kbuf_b, qb_s, lg_s, sem):
    b = pl.program_id(0)
    n_rows = pl.num_programs(0)
    n_a, n_b = kbuf_a.shape[0], kbuf_b.shape[0]
    per_chunk = -(-(n_a + n_b) // FF_CHUNKS)
    switch_chunk = n_a // per_chunk
    assert n_a == switch_chunk * per_chunk
    issue_b = -(-n_b // switch_chunk)
    issue_a = -(-n_a // (FF_CHUNKS - switch_chunk))
    next_row = jnp.minimum(b + 1, n_rows - 1)

    def copy_a(row, j, zero=0):
        return pltpu.make_async_copy(ck_ref.at[pt_ref[row, j] + zero], kbuf_a.at[j], sem.at[0])

    def copy_b(row, j, zero=0):
        return pltpu.make_async_copy(ck_ref.at[pt_ref[row, n_a + j] + zero], kbuf_b.at[j], sem.at[1])

    @pl.when(b == 0)
    def _():
        for j in range(n_a):
            copy_a(0, j).start()

    qb_s[...] = jnp.broadcast_to(qc_ref[0], qb_s.shape)
    for j in range(n_a):
        copy_a(b, j).wait()

    def side_work(c, act):
        zero_bits = _zero_after(act[:SUBLANES, :PAGE_SIZE])
        zero_idx = pltpu.bitcast(zero_bits, jnp.int32)[0, 0]
        if c < switch_chunk:
            for j in range(c * issue_b, min((c + 1) * issue_b, n_b)):
                copy_b(b, j, zero_idx).start()
        else:
            if c == switch_chunk:
                for j in range(n_b):
                    copy_b(b, j).wait()
            first = (c - switch_chunk) * issue_a
            for j in range(first, min(first + issue_a, n_a)):
                copy_a(next_row, j, zero_idx).start()
        page_ids = range(c * per_chunk, min((c + 1) * per_chunk, n_a + n_b))
        pages = [kbuf_a.at[j] if j < n_a else kbuf_b.at[j - n_a] for j in page_ids]
        for j, tile in zip(page_ids, _score_pages(pages, qb_s, pltpu.bitcast(zero_bits, F32))):
            lg_s[j] = tile

    c1 = POOL_WIDTH
    c2 = c1 + MOBA_WIDTH
    mix = (_dot(yp_ref[...], wout_ref[:c1, :]) + _dot(ya_ref[...], wout_ref[c1:c2, :])
           + _dot(ym_ref[...], wout_ref[c2:, :]))
    x2 = x1_ref[...] + mix
    h_s[...] = _rmsnorm(x2, n2_ref[...]).astype(BF16)
    y_ref[...] = x2 + 0.5 * _swiglu(h_s, wgu_ref, wo_ref, act_s, side_work)
    _select_blocks(lg_s[...], q_ref, kn_ref, psel_ref, ids_ref, pself_ref)

    @pl.when(b == n_rows - 1)
    def _():
        for j in range(n_a):
            copy_a(next_row, j).wait()


def _outproj_ffn2_with_scores(x1, y_pool, y_moba, y_mem, tm, wout, n2, wgu, wo, page_table, q, k_new, cache_k_t):
    n = x1.shape[0]
    n_rows, n_pages = page_table.shape
    assert n // tm == n_rows
    pages_per_block = MOBA_BLOCK // PAGE_SIZE
    assert n_pages % pages_per_block == 0
    n_blocks = n_pages // pages_per_block
    assert LANES % n_blocks == 0 and n_blocks & (n_blocks - 1) == 0 and n_blocks >= MOBA_TOPK
    per_chunk = -(-n_pages // FF_CHUNKS)
    n_a = per_chunk * (FF_CHUNKS // 2)
    kept_pages, row_spec, psel_spec, tile_spec = _sample_specs(n_pages)
    col_spec = pl.BlockSpec((1, MOBA_WIDTH, 1), lambda b, *_: (b, 0, 0))

    def rows(width):
        return pl.BlockSpec((tm, width), lambda i, *_: (i, 0))

    resident = (2 * (wgu.size + wo.size + wout.size) + tm * (D_MODEL + D_FF) * 2
                + (n_pages + 1) * MOBA_WIDTH * PAGE_SIZE * 4)
    streamed = 2 * tm * (4 * 2 * D_MODEL + 2 * D_MODEL)
    return pl.pallas_call(
        _outproj_ffn2_scores_kernel,
        grid_spec=pltpu.PrefetchScalarGridSpec(
            num_scalar_prefetch=1,
            grid=(n_rows,),
            in_specs=[rows(D_MODEL), rows(POOL_WIDTH), rows(MOBA_WIDTH), rows(MEM_WIDTH),
                      _const_spec(wout.shape), _const_spec((1, D_MODEL)), _const_spec(wgu.shape),
                      _const_spec(wo.shape), col_spec, row_spec, row_spec, pl.BlockSpec(memory_space=pl.ANY)],
            out_specs=[rows(D_MODEL), psel_spec, tile_spec, tile_spec],
            scratch_shapes=[pltpu.VMEM((tm, D_MODEL), BF16), pltpu.VMEM((tm, D_FF), BF16),
                            pltpu.VMEM((n_a, MOBA_WIDTH, PAGE_SIZE), F32),
                            pltpu.VMEM((n_pages - n_a, MOBA_WIDTH, PAGE_SIZE), F32),
                            pltpu.VMEM((MOBA_WIDTH, PAGE_SIZE), F32),
                            pltpu.VMEM((n_pages, MOBA_HEADS, PAGE_SIZE), F32),
                            pltpu.SemaphoreType.DMA((2,))]),
        out_shape=[jax.ShapeDtypeStruct((n, D_MODEL), F32),
                   jax.ShapeDtypeStruct((n_rows, kept_pages, MOBA_HEADS, PAGE_SIZE), F32),
                   jax.ShapeDtypeStruct((n_rows, MOBA_HEADS, LANES), jnp.int32),
                   jax.ShapeDtypeStruct((n_rows, MOBA_HEADS, LANES), F32)],
        compiler_params=pltpu.CompilerParams(dimension_semantics=("arbitrary",),
                                             vmem_limit_bytes=_vmem_limit(resident + streamed)),
        name="outproj_ffn2_scores",
    )(page_table, x1, y_pool, y_moba, y_mem, wout, n2, wgu, wo, q.reshape(n_rows, MOBA_WIDTH, 1),
      q.reshape(n_rows, 1, MOBA_WIDTH), k_new.reshape(n_rows, 1, MOBA_WIDTH), cache_k_t)


def _moba_sample_values(page_table, p_sel, ids, p_self, v_new, cache_v_t):
    n_rows, n_pages = page_table.shape
    kept_pages, row_spec, psel_spec, tile_spec = _sample_specs(n_pages)
    any_spec = pl.BlockSpec(memory_space=pl.ANY)
    kept_ids = ids[:, :, :MOBA_TOPK].reshape(n_rows, MOBA_HEADS * MOBA_TOPK)
    return pl.pallas_call(
        _moba_sample_values_kernel,
        grid_spec=pltpu.PrefetchScalarGridSpec(
            num_scalar_prefetch=2,
            grid=(n_rows,),
            in_specs=[psel_spec, tile_spec, row_spec, any_spec],
            out_specs=row_spec,
            scratch_shapes=[pltpu.VMEM((2, MOBA_HEADS * kept_pages, HEAD_DIM, PAGE_SIZE), F32),
                            pltpu.SemaphoreType.DMA((2,))]),
        out_shape=jax.ShapeDtypeStruct((n_rows, 1, MOBA_WIDTH), BF16),
        compiler_params=pltpu.CompilerParams(dimension_semantics=("arbitrary",)),
        name="moba_sample_values",
    )(page_table, kept_ids, p_sel, p_self, v_new.reshape(n_rows, 1, MOBA_WIDTH), cache_v_t)


def _mem_kv_kernel(mem_ref, g_ref, wt_ref, kg_ref, mk_ref, mv_ref):
    h = _rmsnorm(mem_ref[...], g_ref[...]).astype(BF16)
    kv_t = _dot_nt(wt_ref[...], h)
    mk_ref[...] = _head_rmsnorm_t(kv_t[:MEM_WIDTH], _lane_tile(kg_ref[...], h.shape[0]))
    mv_ref[...] = kv_t[MEM_WIDTH:]


def _mem_kv(mem, batch, mem_len, g, w_t, kg_tab):
    feat = pl.BlockSpec((None, MEM_WIDTH, mem_len), lambda b: (b, 0, 0))
    return pl.pallas_call(
        _mem_kv_kernel,
        grid=(batch,),
        in_specs=[pl.BlockSpec((mem_len, D_MODEL), lambda b: (b, 0)), _const_spec((1, D_MODEL)),
                  _const_spec(w_t.shape), _const_spec(kg_tab.shape)],
        out_specs=[feat, feat],
        out_shape=[jax.ShapeDtypeStruct((batch, MEM_WIDTH, mem_len), F32)] * 2,
        compiler_params=pltpu.CompilerParams(dimension_semantics=("parallel",)),
        name="mem_kv",
    )(mem, g, w_t, kg_tab)


def _pool_select(lane, a2, a4, a8, a16):
    return jnp.where(lane < POOL_GROUP, a2,
                     jnp.where(lane < 2 * POOL_GROUP, a4, jnp.where(lane < 3 * POOL_GROUP, a8, a16)))


def _mix_prompt_kernel(u_ref, halo_ref, qm_ref, mk_ref, mv_ref, wbd_ref, scale_ref, yp_ref, ym_ref):
    t = pl.program_id(1)
    tm = u_ref.shape[0]
    halo_rows = halo_ref.shape[0]
    u = u_ref[...]
    ext = jnp.concatenate([jnp.where(t > 0, halo_ref[...], 0.0), u], axis=0)
    a2 = ext + pltpu.roll(ext, 1, 0)
    a4 = a2 + pltpu.roll(a2, 2, 0)
    a8 = a4 + pltpu.roll(a4, 4, 0)
    a16 = a8 + pltpu.roll(a8, 8, 0)
    lane = lax.broadcasted_iota(jnp.int32, u.shape, 1)
    sums = _pool_select(lane, a2[halo_rows:], a4[halo_rows:], a8[halo_rows:], a16[halo_rows:])
    pos = t * tm + lax.broadcasted_iota(jnp.int32, u.shape, 0)
    window = _pool_select(lane, *POOL_WINDOWS)
    cnt = jnp.minimum(window, pos + 1).astype(F32)
    pooled = sums / cnt - u
    yp_ref[...] = (_dot(pooled.astype(BF16), wbd_ref[...]) * scale_ref[...]).astype(yp_ref.dtype)

    mem_len = mk_ref.shape[1]
    row_v = lax.broadcasted_iota(jnp.int32, (LANES, mem_len), 0)
    lane_q = lax.broadcasted_iota(jnp.int32, (tm, LANES), 1)
    for pair in range(MEM_WIDTH // LANES):
        cols = slice(pair * LANES, (pair + 1) * LANES)
        qp = qm_ref[:, cols]
        mk_t = mk_ref[cols, :].astype(BF16)
        mv_t = mv_ref[cols, :]
        out = None
        for h in range(HEADS_PER_VREG):
            in_head = (lane_q < HEAD_DIM) if h == 0 else (lane_q >= HEAD_DIM)
            in_head_v = (row_v < HEAD_DIM) if h == 0 else (row_v >= HEAD_DIM)
            lg = _dot(jnp.where(in_head, qp, 0.0).astype(BF16), mk_t)
            p = jnp.exp(lg - jnp.max(lg, axis=-1, keepdims=True))
            o = _dot_nt(p.astype(BF16), jnp.where(in_head_v, mv_t, 0.0).astype(BF16))
            o = o / jnp.sum(p, axis=-1, keepdims=True)
            out = o if out is None else out + o
        ym_ref[:, cols] = out.astype(ym_ref.dtype)


def _mix_prompt(u, qm, mk_t, mv_t, wbd, scale, batch, seq, tm):
    halo_rows = 16
    assert halo_rows > POOL_BUF and tm % halo_rows == 0 and seq % tm == 0
    tiles = seq // tm
    mem_len = mk_t.shape[2]

    def rows(width):
        return pl.BlockSpec((tm, width), lambda b, t: (b * tiles + t, 0))

    halo = pl.BlockSpec((halo_rows, POOL_WIDTH),
                        lambda b, t: (jnp.maximum((b * tiles + t) * (tm // halo_rows) - 1, 0), 0))
    mem = pl.BlockSpec((None, MEM_WIDTH, mem_len), lambda b, t: (b, 0, 0))
    return pl.pallas_call(
        _mix_prompt_kernel,
        grid=(batch, tiles),
        in_specs=[rows(POOL_WIDTH), halo, rows(MEM_WIDTH), mem, mem, _const_spec(wbd.shape),
                  _const_spec((1, POOL_WIDTH))],
        out_specs=[rows(POOL_WIDTH), rows(MEM_WIDTH)],
        out_shape=[jax.ShapeDtypeStruct((batch * seq, POOL_WIDTH), BF16),
                   jax.ShapeDtypeStruct((batch * seq, MEM_WIDTH), BF16)],
        compiler_params=pltpu.CompilerParams(dimension_semantics=("parallel", "parallel")),
        name="mix_prompt",
    )(u, u, qm, mk_t, mv_t, wbd, scale)


def _mix_sample_kernel(u_ref, st_ref, qm_ref, mk_ref, mv_ref, wbd_ref, scale_ref, yp_ref, ym_ref, *, pos):
    rows = u_ref.shape[0]
    u = u_ref[...]
    tail = u
    sums = {}
    for back in range(1, POOL_BUF + 1):
        tail = tail + st_ref[POOL_BUF - back]
        if back + 1 in POOL_WINDOWS:
            sums[back + 1] = tail / float(min(back + 1, pos + 1))
    lane = lax.broadcasted_iota(jnp.int32, u.shape, 1)
    pooled = _pool_select(lane, *(sums[w] for w in POOL_WINDOWS)) - u
    yp_ref[...] = (_dot(pooled.astype(BF16), wbd_ref[...]) * scale_ref[...]).astype(yp_ref.dtype)

    row_t = lax.broadcasted_iota(jnp.int32, (SCORE_ROWS, MEM_WIDTH), 0)
    own_head = lax.broadcasted_iota(jnp.int32, (SCORE_ROWS, MEM_WIDTH), 1) // HEAD_DIM == row_t
    for r in range(rows):
        q = jnp.broadcast_to(qm_ref[r:r + 1, :], (SCORE_ROWS, MEM_WIDTH))
        q_t = jnp.where(own_head, q, 0.0).astype(BF16)
        lg = _dot(q_t, mk_ref[r].astype(BF16))
        p = jnp.exp(lg - jnp.max(lg, axis=-1, keepdims=True))
        p = p / jnp.sum(p, axis=-1, keepdims=True)
        o = _dot_nt(p.astype(BF16), mv_ref[r].astype(BF16))
        ym_ref[r:r + 1, :] = jnp.sum(jnp.where(own_head, o, 0.0), axis=0, keepdims=True).astype(ym_ref.dtype)


def _mix_sample(u, state_t, qm, mk_t, mv_t, wbd, scale, rows_per_step, pos):
    n = u.shape[0]
    mem_len = mk_t.shape[2]

    def rows(width):
        return pl.BlockSpec((rows_per_step, width), lambda i: (i, 0))

    mem = pl.BlockSpec((rows_per_step, MEM_WIDTH, mem_len), lambda i: (i, 0, 0))
    return pl.pallas_call(
        functools.partial(_mix_sample_kernel, pos=pos),
        grid=(n // rows_per_step,),
        in_specs=[rows(POOL_WIDTH), pl.BlockSpec((POOL_BUF, rows_per_step, POOL_WIDTH), lambda i: (0, i, 0)),
                  rows(MEM_WIDTH), mem, mem, _const_spec(wbd.shape), _const_spec((1, POOL_WIDTH))],
        out_specs=[rows(POOL_WIDTH), rows(MEM_WIDTH)],
        out_shape=[jax.ShapeDtypeStruct((n, POOL_WIDTH), BF16), jax.ShapeDtypeStruct((n, MEM_WIDTH), BF16)],
        compiler_params=pltpu.CompilerParams(dimension_semantics=("parallel",)),
        name="mix_sample",
    )(u, state_t, qm, mk_t, mv_t, wbd, scale)


def _rope_angles(pos):
    inv = jnp.power(jnp.float32(ROPE_THETA), -jnp.arange(ROT_HALF, dtype=F32) * (2.0 / ROT_DIM))
    return pos.astype(F32)[:, None] * inv[None, :]


def _rope_tables(pos):
    ang = _rope_angles(pos)
    cos, sin = jnp.cos(ang), jnp.sin(ang)
    rest = HEAD_DIM - ROT_DIM
    ones = jnp.ones((pos.shape[0], rest), F32)
    zeros = jnp.zeros((pos.shape[0], rest), F32)
    cos_h = jnp.concatenate([cos, cos, ones], axis=1)
    sin_h = jnp.concatenate([-sin, sin, zeros], axis=1)
    return jnp.tile(cos_h, (1, HEADS_PER_VREG)), jnp.tile(sin_h, (1, HEADS_PER_VREG))


def _block_diag_ones(width):
    head = jnp.arange(width) // HEAD_DIM
    return (head[:, None] == head[None, :]).astype(BF16)


def _tile_gain(g, width):
    return jnp.tile(g, width // HEAD_DIM).reshape(1, width)


def _gain_table(g, width):
    return jnp.broadcast_to(jnp.tile(g, width // HEAD_DIM)[:, None], (width, LANES))


def _feature_major_to_heads(a_t, n_heads):
    batch, _, tokens = a_t.shape
    return jnp.transpose(a_t.reshape(batch, n_heads, HEAD_DIM, tokens), (0, 3, 1, 2))[None]


def kernel(x_prompt, x_sample, mem_prompt, cache_k, cache_v, cache_mem_k, cache_mem_v, state_pool, page_table,
           ffn1_norm, ffn1_w_in, ffn1_w_out, mix_norm, w_in, pool_w, pool_scale, q_norm, k_norm,
           mem_norm, mem_w_kv, mem_q_norm, mem_k_norm, w_out, ffn2_norm, ffn2_w_in, ffn2_w_out):
    depth = ffn1_norm.shape[0]
    assert depth == 1
    n_p, t_p, _ = x_prompt.shape
    n_s, t_s, _ = x_sample.shape
    assert t_s == 1
    past_len = page_table.shape[1] * PAGE_SIZE
    mem_len = mem_prompt.shape[1]
    l = 0

    wgu1, wo1 = ffn1_w_in[l].astype(BF16), ffn1_w_out[l].astype(BF16)
    win = w_in[l].astype(BF16)
    c1 = POOL_WIDTH
    c4 = c1 + 3 * MOBA_WIDTH
    wnat = jnp.concatenate([win[:, :c1], win[:, c4:]], axis=1)
    wqkv_t = win[:, c1:c4].T
    wkv_t = mem_w_kv[l].astype(BF16).T
    n1, nmix, n2, nmem = (a[l].reshape(1, D_MODEL) for a in (ffn1_norm, mix_norm, ffn2_norm, mem_norm))
    qg, kg = _tile_gain(q_norm[l], MOBA_WIDTH), _tile_gain(k_norm[l], MOBA_WIDTH)
    mqg = _tile_gain(mem_q_norm[l], MEM_WIDTH)
    qg_tab, kg_tab = _gain_table(q_norm[l], MOBA_WIDTH), _gain_table(k_norm[l], MOBA_WIDTH)
    mkg_tab = _gain_table(mem_k_norm[l], MEM_WIDTH)
    ones_bd = _block_diag_ones(MOBA_WIDTH)
    ones_mem = ones_bd[:MEM_WIDTH, :MEM_WIDTH]
    wbd = jax.scipy.linalg.block_diag(*[pool_w[l, g] for g in range(len(POOL_WINDOWS))]).astype(BF16)
    scale = pool_scale[l].reshape(1, POOL_WIDTH)

    xp = x_prompt.reshape(n_p * t_p, D_MODEL)
    ang_p = _rope_angles(jnp.arange(t_p, dtype=jnp.int32)).T
    x1p, up, qmp, qt_p, kt_p, vt_p, wgu2, wo2, wout = _ffn1_inproj_t(
        xp, n_p, t_p, ROW_TILE, n1, wgu1, wo1, nmix, wnat, wqkv_t, qg_tab, kg_tab, mqg, ones_mem,
        jnp.cos(ang_p), jnp.sin(ang_p), ffn2_w_in[l], ffn2_w_out[l], w_out[l])
    mkt_p, mvt_p = _mem_kv(mem_prompt.reshape(n_p * mem_len, D_MODEL), n_p, mem_len, nmem, wkv_t, mkg_tab)
    ya_p = _moba_prompt(qt_p, kt_p, vt_p)
    yp_p, ym_p = _mix_prompt(up, qmp, mkt_p, mvt_p, wbd, scale, n_p, t_p, ROW_TILE)

    xs = x_sample.reshape(n_s, D_MODEL)
    cos_s, sin_s = _rope_tables(jnp.full((n_s,), past_len, jnp.int32))
    x1s, us, qs, ks, vs, qms = _ffn1_inproj(xs, n1, wgu1, wo1, nmix, win, qg, kg, mqg, ones_bd, cos_s, sin_s)

    def feature_major(cache, n_heads):
        rows, tokens = cache.shape[1], cache.shape[2]
        return jnp.transpose(cache[l], (0, 2, 3, 1)).reshape(rows, n_heads * HEAD_DIM, tokens)

    y_prompt, p_sel, kept_ids, p_self = _outproj_ffn2_with_scores(
        x1p, yp_p, ya_p, ym_p, ROW_TILE, wout, n2, wgu2, wo2, page_table, qs, ks, feature_major(cache_k, MOBA_HEADS))
    ya_s = _moba_sample_values(page_table, p_sel, kept_ids, p_self, vs,
                               feature_major(cache_v, MOBA_HEADS)).reshape(n_s, MOBA_WIDTH)
    state = state_pool[l]
    yp_s, ym_s = _mix_sample(us, jnp.swapaxes(state, 0, 1), qms, feature_major(cache_mem_k, MEM_HEADS),
                             feature_major(cache_mem_v, MEM_HEADS), wbd, scale, 8, past_len)
    y_sample = _outproj_ffn2(x1s, yp_s, ya_s, ym_s, n_s, wout, n2, wgu2, wo2)

    pool_state_p = up.reshape(n_p, t_p, POOL_WIDTH)[:, t_p - POOL_BUF:][None]
    pool_state_s = jnp.concatenate([state[:, 1:], us[:, None, :]], axis=1)[None]
    return (y_prompt.reshape(n_p, t_p, D_MODEL), y_sample.reshape(n_s, t_s, D_MODEL),
            _feature_major_to_heads(kt_p, MOBA_HEADS), _feature_major_to_heads(vt_p, MOBA_HEADS),
            _feature_major_to_heads(mkt_p, MEM_HEADS), _feature_major_to_heads(mvt_p, MEM_HEADS),
            pool_state_p,
            ks.reshape(1, n_s, t_s, MOBA_HEADS, HEAD_DIM), vs.reshape(1, n_s, t_s, MOBA_HEADS, HEAD_DIM),
            pool_state_s)
```

```python
import functools

import jax
import jax.numpy as jnp
from jax import lax
from jax.experimental import pallas as pl
from jax.experimental.pallas import tpu as pltpu

F32 = jnp.float32
BF16 = jnp.bfloat16

D_MODEL = 1024
HEAD_DIM = 64
POOL_WIDTH = 256
POOL_WINDOWS = (2, 4, 8, 16)
POOL_GROUP = POOL_WIDTH // len(POOL_WINDOWS)
POOL_BUF = max(POOL_WINDOWS) - 1
MOBA_WIDTH = 512
MOBA_HEADS = MOBA_WIDTH // HEAD_DIM
MOBA_BLOCK = 256
MOBA_TOPK = 3
MEM_WIDTH = 256
MEM_HEADS = MEM_WIDTH // HEAD_DIM
ROT_DIM = HEAD_DIM // 4
ROT_HALF = ROT_DIM // 2
ROPE_THETA = 500000.0
D_FF = 2816
RMS_EPS = 1e-6
PAGE_SIZE = 128
ATTN_SCALE = HEAD_DIM ** -0.5
LOG2_E = 1.4426950408889634

LANES = 128
SUBLANES = 8
V7X_VMEM_BYTES = 64 * 1024 * 1024

HEADS_PER_VREG = LANES // HEAD_DIM
MASKED = -1e30
FF_CHUNK = 256
FF_CHUNKS = D_FF // FF_CHUNK
ROW_TILE = 512
SCORE_ROWS = 16


def _dot(a, b):
    return jnp.dot(a, b, preferred_element_type=F32)


def _dot_nt(a, b):
    return lax.dot_general(a, b, (((1,), (1,)), ((), ())), preferred_element_type=F32)


def _split_bf16(x):
    hi = x.astype(BF16)
    lo = (x - hi.astype(F32)).astype(BF16)
    return hi, lo


def _vmem_limit(resident_bytes):
    return int(min(resident_bytes + 16 * 1024 * 1024, V7X_VMEM_BYTES - 4 * 1024 * 1024))


def _rmsnorm(x, g):
    ms = jnp.mean(x * x, axis=-1, keepdims=True)
    return x * lax.rsqrt(ms + RMS_EPS) * g


def _head_rmsnorm(x, g, ones_bd):
    hi, lo = _split_bf16(x * x)
    ms = (_dot(hi, ones_bd) + _dot(lo, ones_bd)) * (1.0 / HEAD_DIM)
    return x * lax.rsqrt(ms + RMS_EPS) * g


def _head_rmsnorm_t(x, g):
    heads = []
    for r in range(0, x.shape[0], HEAD_DIM):
        xh = x[r:r + HEAD_DIM]
        ms = jnp.mean(xh * xh, axis=0, keepdims=True)
        heads.append(xh * lax.rsqrt(ms + RMS_EPS))
    return jnp.concatenate(heads, axis=0) * g


def _rope(x, cos, sin):
    width = x.shape[1]
    lane = lax.broadcasted_iota(jnp.int32, x.shape, 1)
    first_half = (lane & (HEAD_DIM - 1)) < ROT_HALF
    partner = jnp.where(first_half, pltpu.roll(x, width - ROT_HALF, 1), pltpu.roll(x, ROT_HALF, 1))
    return x * cos + partner * sin


def _rope_t(x, cos, sin):
    pieces = []
    for r in range(0, x.shape[0], HEAD_DIM):
        x1, x2 = x[r:r + ROT_HALF], x[r + ROT_HALF:r + ROT_DIM]
        pieces += [x1 * cos - x2 * sin, x2 * cos + x1 * sin, x[r + ROT_DIM:r + HEAD_DIM]]
    return jnp.concatenate(pieces, axis=0)


def _lane_tile(table, n):
    return jnp.concatenate([table] * (n // LANES), axis=1)


def _swiglu(h_s, wgu_ref, wo_ref, act_s, side_work=None):
    for c in range(FF_CHUNKS):
        lo = c * FF_CHUNK
        gate = _dot(h_s[...], wgu_ref[:, lo:lo + FF_CHUNK])
        up = _dot(h_s[...], wgu_ref[:, D_FF + lo:D_FF + lo + FF_CHUNK])
        act = jax.nn.silu(gate) * up
        act_s[:, lo:lo + FF_CHUNK] = act.astype(BF16)
        if side_work is not None:
            side_work(c, act)
    return _dot(act_s[...], wo_ref[...])


def _const_spec(shape):
    return pl.BlockSpec(shape, lambda *_: (0,) * len(shape), pipeline_mode=pl.Buffered(1))


def _ffn1_trunk(x_ref, n1_ref, wgu_ref, wo_ref, nmix_ref, x1_ref, h_s, act_s):
    h_s[...] = _rmsnorm(x_ref[...], n1_ref[...]).astype(BF16)
    x1_ref[...] = x_ref[...] + 0.5 * _swiglu(h_s, wgu_ref, wo_ref, act_s)
    h_s[...] = _rmsnorm(x1_ref[...], nmix_ref[...]).astype(BF16)


def _ffn1_inproj_kernel(x_ref, n1_ref, wgu_ref, wo_ref, nmix_ref, win_ref, qg_ref, kg_ref, mg_ref,
                        ones_ref, cos_ref, sin_ref,
                        x1_ref, u_ref, q_ref, k_ref, v_ref, qm_ref, h_s, act_s):
    _ffn1_trunk(x_ref, n1_ref, wgu_ref, wo_ref, nmix_ref, x1_ref, h_s, act_s)
    c1 = POOL_WIDTH
    c2 = c1 + MOBA_WIDTH
    c3 = c2 + MOBA_WIDTH
    c4 = c3 + MOBA_WIDTH
    u_ref[...] = _dot(h_s[...], win_ref[:, :c1])
    q = _dot(h_s[...], win_ref[:, c1:c2])
    k = _dot(h_s[...], win_ref[:, c2:c3])
    v_ref[...] = _dot(h_s[...], win_ref[:, c3:c4])
    qm = _dot(h_s[...], win_ref[:, c4:])
    reps = MOBA_WIDTH // LANES
    cos = jnp.concatenate([cos_ref[...]] * reps, axis=1)
    sin = jnp.concatenate([sin_ref[...]] * reps, axis=1)
    ones = ones_ref[...]
    q_ref[...] = _rope(_head_rmsnorm(q, qg_ref[...], ones), cos, sin) * ATTN_SCALE
    k_ref[...] = _rope(_head_rmsnorm(k, kg_ref[...], ones), cos, sin)
    qm_ref[...] = _head_rmsnorm(qm, mg_ref[...], ones_ref[:MEM_WIDTH, :MEM_WIDTH]) * ATTN_SCALE


def _ffn1_inproj(x, n1, wgu, wo, nmix, win, qg, kg, mg, ones_bd, cos_tab, sin_tab):
    n = x.shape[0]

    def full(width):
        return pl.BlockSpec((n, width), lambda i: (0, 0))

    widths = (D_MODEL, POOL_WIDTH, MOBA_WIDTH, MOBA_WIDTH, MOBA_WIDTH, MEM_WIDTH)
    resident = 2 * (wgu.size + wo.size + win.size + ones_bd.size)
    return pl.pallas_call(
        _ffn1_inproj_kernel,
        grid=(1,),
        in_specs=[full(D_MODEL), _const_spec((1, D_MODEL)), _const_spec(wgu.shape), _const_spec(wo.shape),
                  _const_spec((1, D_MODEL)), _const_spec(win.shape),
                  _const_spec((1, MOBA_WIDTH)), _const_spec((1, MOBA_WIDTH)), _const_spec((1, MEM_WIDTH)),
                  _const_spec(ones_bd.shape), full(LANES), full(LANES)],
        out_specs=[full(w) for w in widths],
        out_shape=[jax.ShapeDtypeStruct((n, w), F32) for w in widths],
        scratch_shapes=[pltpu.VMEM((n, D_MODEL), BF16), pltpu.VMEM((n, D_FF), BF16)],
        compiler_params=pltpu.CompilerParams(dimension_semantics=("arbitrary",),
                                             vmem_limit_bytes=_vmem_limit(resident)),
        name="ffn1_inproj_sample",
    )(x, n1, wgu, wo, nmix, win, qg, kg, mg, ones_bd, cos_tab, sin_tab)


def _ffn1_inproj_t_kernel(x_ref, n1_ref, wgu_ref, wo_ref, nmix_ref, wnat_ref, wqkv_t_ref, qg_ref, kg_ref, mg_ref,
                          ones_ref, cos_ref, sin_ref, mk_ref, mv_ref, wbd_ref, scale_ref,
                          wgu2_f32_ref, wo2_f32_ref, wout_f32_ref,
                          x1_ref, u_ref, yp_ref, ym_ref, qt_ref, kt_ref, vt_ref, wgu2_ref, wo2_ref, wout_ref,
                          h_s, act_s, halo_s, *, tiles):
    t = lax.rem(pl.program_id(0), tiles)
    wgu2_ref[...] = wgu2_f32_ref[...].astype(BF16)
    wo2_ref[...] = wo2_f32_ref[...].astype(BF16)
    wout_ref[...] = wout_f32_ref[...].astype(BF16)

    @pl.when(pl.program_id(0) == 0)
    def _():
        halo_s[...] = jnp.zeros_like(halo_s)

    _ffn1_trunk(x_ref, n1_ref, wgu_ref, wo_ref, nmix_ref, x1_ref, h_s, act_s)
    tm = x_ref.shape[0]
    u = _dot(h_s[...], wnat_ref[:, :POOL_WIDTH])
    u_ref[...] = u
    qm = _head_rmsnorm(_dot(h_s[...], wnat_ref[:, POOL_WIDTH:]), mg_ref[...], ones_ref[...]) * ATTN_SCALE
    q_t = _dot_nt(wqkv_t_ref[:MOBA_WIDTH, :], h_s[...])
    halo = jnp.where(t > 0, halo_s[...], 0.0)
    halo_s[...] = u[tm - halo_s.shape[0]:]
    yp_ref[...] = _pool_mixer_tile(u, halo, t * tm, wbd_ref, scale_ref).astype(yp_ref.dtype)
    k_t = _dot_nt(wqkv_t_ref[MOBA_WIDTH:2 * MOBA_WIDTH, :], h_s[...])
    _mem_attention_pair(qm, mk_ref, mv_ref, ym_ref, 0)
    vt_ref[...] = _dot_nt(wqkv_t_ref[2 * MOBA_WIDTH:, :], h_s[...])
    _mem_attention_pair(qm, mk_ref, mv_ref, ym_ref, 1)
    cos, sin = cos_ref[...], sin_ref[...]
    qt_ref[...] = _rope_t(_head_rmsnorm_t(q_t, _lane_tile(qg_ref[...], tm)), cos, sin) * (ATTN_SCALE * LOG2_E)
    kt_ref[...] = _rope_t(_head_rmsnorm_t(k_t, _lane_tile(kg_ref[...], tm)), cos, sin)


def _ffn1_inproj_t(x, batch, seq, tm, n1, wgu, wo, nmix, wnat, wqkv_t, qg_tab, kg_tab, mg, ones_bd, cos_t, sin_t,
                   mk_t, mv_t, wbd, scale, wgu2_f32, wo2_f32, wout_f32):
    tiles = seq // tm
    steps = batch * tiles
    halo_rows = 16
    assert halo_rows > POOL_BUF and tm % halo_rows == 0 and seq % tm == 0
    mem_len = mk_t.shape[2]
    mem = pl.BlockSpec((None, MEM_WIDTH, mem_len), lambda i: (i // tiles, 0, 0))

    def rows(width):
        return pl.BlockSpec((tm, width), lambda i: (i, 0))

    def slab(w, bf16_rows=16):
        n_slabs = steps
        while w.shape[0] % n_slabs or (w.shape[0] // n_slabs) % bf16_rows:
            n_slabs //= 2
        return pl.BlockSpec((w.shape[0] // n_slabs, w.shape[1]), lambda i, n=n_slabs: (jnp.minimum(i, n - 1), 0))

    feat = pl.BlockSpec((None, MOBA_WIDTH, tm), lambda i: (i // tiles, 0, i % tiles))
    tab = pl.BlockSpec((ROT_HALF, tm), lambda i: (0, i % tiles))
    casts = (wgu2_f32, wo2_f32, wout_f32)
    resident = 2 * (wgu.size + wo.size + wnat.size + wqkv_t.size + ones_bd.size) + tm * (D_MODEL + D_FF) * 2
    streamed = (2 * 4 * tm * (2 * D_MODEL + POOL_WIDTH + MEM_WIDTH + 3 * MOBA_WIDTH + 2 * ROT_HALF)
                + 2 * 6 * sum(w.size for w in casts) // steps)
    feat_shape = jax.ShapeDtypeStruct((batch, MOBA_WIDTH, seq), F32)
    return pl.pallas_call(
        functools.partial(_ffn1_inproj_t_kernel, tiles=tiles),
        grid=(steps,),
        in_specs=[rows(D_MODEL), _const_spec((1, D_MODEL)), _const_spec(wgu.shape), _const_spec(wo.shape),
                  _const_spec((1, D_MODEL)), _const_spec(wnat.shape), _const_spec(wqkv_t.shape),
                  _const_spec(qg_tab.shape), _const_spec(kg_tab.shape), _const_spec((1, MEM_WIDTH)),
                  _const_spec(ones_bd.shape), tab, tab, mem, mem, _const_spec(wbd.shape),
                  _const_spec((1, POOL_WIDTH))] + [slab(w) for w in casts],
        out_specs=[rows(D_MODEL), rows(POOL_WIDTH), rows(POOL_WIDTH), rows(MEM_WIDTH), feat, feat, feat]
        + [slab(w) for w in casts],
        out_shape=[jax.ShapeDtypeStruct((batch * seq, D_MODEL), F32),
                   jax.ShapeDtypeStruct((batch * seq, POOL_WIDTH), F32),
                   jax.ShapeDtypeStruct((batch * seq, POOL_WIDTH), BF16),
                   jax.ShapeDtypeStruct((batch * seq, MEM_WIDTH), BF16)]
        + [feat_shape] * 3 + [jax.ShapeDtypeStruct(w.shape, BF16) for w in casts],
        scratch_shapes=[pltpu.VMEM((tm, D_MODEL), BF16), pltpu.VMEM((tm, D_FF), BF16),
                        pltpu.VMEM((halo_rows, POOL_WIDTH), F32)],
        compiler_params=pltpu.CompilerParams(dimension_semantics=("arbitrary",),
                                             vmem_limit_bytes=_vmem_limit(resident + streamed)),
        name="ffn1_inproj",
    )(x, n1, wgu, wo, nmix, wnat, wqkv_t, qg_tab, kg_tab, mg, ones_bd, cos_t, sin_t, mk_t, mv_t, wbd, scale, *casts)


def _outproj_ffn2_kernel(x1_ref, yp_ref, ya_ref, ym_ref, wout_ref, n2_ref, wgu_ref, wo_ref, y_ref, h_s, act_s):
    c1 = POOL_WIDTH
    c2 = c1 + MOBA_WIDTH
    mix = (_dot(yp_ref[...], wout_ref[:c1, :]) + _dot(ya_ref[...], wout_ref[c1:c2, :])
           + _dot(ym_ref[...], wout_ref[c2:, :]))
    x2 = x1_ref[...] + mix
    h_s[...] = _rmsnorm(x2, n2_ref[...]).astype(BF16)
    y_ref[...] = x2 + 0.5 * _swiglu(h_s, wgu_ref, wo_ref, act_s)


def _outproj_ffn2(x1, y_pool, y_moba, y_mem, tm, wout, n2, wgu, wo):
    n = x1.shape[0]

    def rows(width):
        return pl.BlockSpec((tm, width), lambda i: (i, 0))

    resident = 2 * (wgu.size + wo.size + wout.size) + tm * (D_MODEL + D_FF) * 2
    streamed = 2 * tm * (4 * 2 * D_MODEL + 2 * D_MODEL)
    return pl.pallas_call(
        _outproj_ffn2_kernel,
        grid=(n // tm,),
        in_specs=[rows(D_MODEL), rows(POOL_WIDTH), rows(MOBA_WIDTH), rows(MEM_WIDTH),
                  _const_spec(wout.shape), _const_spec((1, D_MODEL)), _const_spec(wgu.shape), _const_spec(wo.shape)],
        out_specs=rows(D_MODEL),
        out_shape=jax.ShapeDtypeStruct((n, D_MODEL), F32),
        scratch_shapes=[pltpu.VMEM((tm, D_MODEL), BF16), pltpu.VMEM((tm, D_FF), BF16)],
        compiler_params=pltpu.CompilerParams(dimension_semantics=("parallel",),
                                             vmem_limit_bytes=_vmem_limit(resident + streamed)),
        name="outproj_ffn2",
    )(x1, y_pool, y_moba, y_mem, wout, n2, wgu, wo)


def _moba_prompt_kernel(qt_ref, kt_ref, vt_ref, o_ref, kb_s, vb_s, km_s, s_s, p_s):
    seq = kt_ref.shape[1]
    n_blocks = seq // MOBA_BLOCK
    blocks = [slice(n * MOBA_BLOCK, (n + 1) * MOBA_BLOCK) for n in range(n_blocks)]
    lane_k = lax.broadcasted_iota(jnp.int32, (MOBA_BLOCK, LANES), 1)
    km_s[...] = jnp.zeros_like(km_s)
    for n in range(n_blocks):
        k_n = kt_ref[:, blocks[n]].T
        km_s[n:n + 1, :] = jnp.mean(k_n, axis=0, keepdims=True)
        for h in range(HEADS_PER_VREG):
            own = (lane_k < HEAD_DIM) if h == 0 else (lane_k >= HEAD_DIM)
            tag_lane = (1 - h) * HEAD_DIM + n
            kb_s[h, blocks[n], :] = jnp.where(own, k_n, jnp.where(lane_k == tag_lane, 1.0, 0.0)).astype(BF16)
    vb_s[...] = vt_ref[...].astype(BF16)

    lane_m = lax.broadcasted_iota(jnp.int32, km_s.shape, 1)
    kmean = km_s[...]
    km_split = [_split_bf16(jnp.where((lane_m < HEAD_DIM) if h == 0 else (lane_m >= HEAD_DIM), kmean, 0.0))
                for h in range(HEADS_PER_VREG)]
    blk = lax.broadcasted_iota(jnp.int32, (SUBLANES, MOBA_BLOCK), 0)
    key_id = lax.broadcasted_iota(jnp.int32, (MOBA_BLOCK, MOBA_BLOCK), 0)
    qry_id = lax.broadcasted_iota(jnp.int32, (MOBA_BLOCK, MOBA_BLOCK), 1)
    spare_rows = HEAD_DIM - SUBLANES

    def tile_rows(i, j):
        start = MOBA_BLOCK * (i * (i + 1) // 2 + j)
        return slice(start, start + MOBA_BLOCK)

    def query_operand(i, h):
        q_t = qt_ref[:, blocks[i]]
        if i == 0:
            bias = jnp.zeros((SUBLANES, MOBA_BLOCK), F32)
        else:
            q_hi, q_lo = _split_bf16(q_t)
            km_hi, km_lo = km_split[h]
            s = (_dot(km_hi, q_hi) + _dot(km_lo, q_hi) + _dot(km_hi, q_lo))[:SUBLANES]
            s = jnp.where(blk < i, s, -jnp.inf)
            rank = jnp.zeros(s.shape, jnp.int32)
            for m in range(i):
                sm = s[m:m + 1, :]
                beats = (sm > s) | ((sm == s) & (blk > m))
                rank = rank + beats.astype(jnp.int32)
            allowed = ((blk < i) & (rank < MOBA_TOPK)) | (blk == i)
            bias = jnp.where(allowed, 0.0, MASKED)
        zeros = jnp.zeros((spare_rows, MOBA_BLOCK), F32)
        if h == 0:
            return jnp.concatenate([q_t[:HEAD_DIM], bias, zeros], axis=0).astype(BF16)
        return jnp.concatenate([bias, zeros, q_t[HEAD_DIM:]], axis=0).astype(BF16)

    def logits_tile(i, h, j, q_aug):
        st = _dot(kb_s[h, blocks[j], :], q_aug)
        if j == i:
            st = jnp.where(key_id <= qry_id, st, MASKED)
        s_s[h, tile_rows(i, j), :] = st
        return jnp.max(st, axis=0, keepdims=True)

    def prob_tile(i, h, j, m_row):
        p = jnp.exp2(s_s[h, tile_rows(i, j), :] - m_row)
        p_s[h, tile_rows(i, j), :] = p.astype(BF16)
        return jnp.sum(p, axis=0, keepdims=True)

    def weighted_values(i, h, l_row):
        keys = slice(tile_rows(i, 0).start, tile_rows(i, i).stop)
        acc = _dot(vb_s[h * HEAD_DIM:(h + 1) * HEAD_DIM, :(i + 1) * MOBA_BLOCK], p_s[h, keys, :])
        return acc / l_row

    groups = [(i, h) for i in range(n_blocks) for h in range(HEADS_PER_VREG)]

    def combine(acc, new, op):
        return new if acc is None else op(acc, new)

    q_next = query_operand(*groups[0])
    m_next = None
    for j in range(groups[0][0] + 1):
        m_next = combine(m_next, logits_tile(*groups[0], j, q_next), jnp.maximum)
    outs = []
    for g, (i, h) in enumerate(groups):
        m_row, m_next, l_row = m_next, None, None
        nxt = groups[g + 1] if g + 1 < len(groups) else None
        if nxt is not None:
            q_next = query_operand(*nxt)
        for j in range(max(i + 1, nxt[0] + 1 if nxt is not None else 0)):
            if j <= i:
                l_row = combine(l_row, prob_tile(i, h, j, m_row), jnp.add)
            if nxt is not None and j <= nxt[0]:
                m_next = combine(m_next, logits_tile(*nxt, j, q_next), jnp.maximum)
        outs.append(weighted_values(i, h, l_row))
        if h == HEADS_PER_VREG - 1:
            o_ref[blocks[i], :] = jnp.concatenate(outs, axis=0).T.astype(o_ref.dtype)
            outs = []


def _moba_prompt(q_t, k_t, v_t):
    batch, _, seq = k_t.shape
    n_blocks = seq // MOBA_BLOCK
    pairs = MOBA_WIDTH // LANES
    assert n_blocks <= SUBLANES and n_blocks <= HEAD_DIM
    tile_rows = MOBA_BLOCK * (n_blocks * (n_blocks + 1) // 2)
    feat = pl.BlockSpec((None, LANES, seq), lambda b, p: (b, p, 0))
    return pl.pallas_call(
        _moba_prompt_kernel,
        grid=(batch, pairs),
        in_specs=[feat, feat, feat],
        out_specs=pl.BlockSpec((seq, LANES), lambda b, p: (b, p)),
        out_shape=jax.ShapeDtypeStruct((batch * seq, MOBA_WIDTH), BF16),
        scratch_shapes=[pltpu.VMEM((HEADS_PER_VREG, seq, LANES), BF16),
                        pltpu.VMEM((LANES, seq), BF16),
                        pltpu.VMEM((SCORE_ROWS, LANES), F32),
                        pltpu.VMEM((HEADS_PER_VREG, tile_rows, MOBA_BLOCK), F32),
                        pltpu.VMEM((HEADS_PER_VREG, tile_rows, MOBA_BLOCK), BF16)],
        compiler_params=pltpu.CompilerParams(
            dimension_semantics=("parallel", "parallel"),
            vmem_limit_bytes=_vmem_limit(HEADS_PER_VREG * tile_rows * MOBA_BLOCK * 6 + 8 * LANES * seq * 4)),
        name="moba_prompt",
    )(q_t, k_t, v_t)


def _zero_after(value):
    return lax.shift_right_logical(pltpu.bitcast(value, jnp.uint32), jnp.uint32(32))


def _score_pages(pages, qb_s, zero):
    head_row = lax.broadcasted_iota(jnp.int32, (MOBA_HEADS, PAGE_SIZE), 0)
    tiles = [jnp.zeros((MOBA_HEADS, PAGE_SIZE), F32) for _ in pages]
    for h in range(MOBA_HEADS):
        rows = slice(h * HEAD_DIM, (h + 1) * HEAD_DIM)
        q_h = (qb_s[rows, :].reshape(HEAD_DIM // SUBLANES, SUBLANES, PAGE_SIZE) + zero).reshape(HEAD_DIM, PAGE_SIZE)
        for n, page in enumerate(pages):
            logit = jnp.sum(page[rows, :] * q_h, axis=0, keepdims=True)
            tiles[n] = jnp.where(head_row == h, logit, tiles[n])
    return tiles


def _select_blocks(lg, q_ref, kn_ref, psel_ref, ids_ref, pself_ref):
    n_pages = lg.shape[0]
    pages_per_block = MOBA_BLOCK // PAGE_SIZE
    n_blocks = n_pages // pages_per_block

    blk = lg.reshape(n_blocks, pages_per_block, MOBA_HEADS, PAGE_SIZE)
    blk_sum = blk[:, 0]
    for r in range(1, pages_per_block):
        blk_sum = blk_sum + blk[:, r]
    score_col = jnp.sum(blk_sum, axis=-1, keepdims=True) * (1.0 / MOBA_BLOCK)
    lane = lax.broadcasted_iota(jnp.int32, (MOBA_HEADS, LANES), 1)
    blk_of_lane = lane & (n_blocks - 1)
    score = jnp.zeros((MOBA_HEADS, LANES), F32)
    for n in range(n_blocks):
        score = jnp.where(blk_of_lane == n, score_col[n], score)
    rank = jnp.zeros(score.shape, jnp.int32)
    for k in range(1, n_blocks):
        other = pltpu.roll(score, k, 1)
        beats = (other > score) | ((other == score) & (blk_of_lane >= k))
        rank = rank + beats.astype(jnp.int32)
    bias = jnp.where(rank < MOBA_TOPK, 0.0, MASKED)

    q = q_ref[0]
    own_head = (lax.broadcasted_iota(jnp.int32, (MOBA_HEADS, MOBA_WIDTH), 1) // HEAD_DIM
                == lax.broadcasted_iota(jnp.int32, (MOBA_HEADS, MOBA_WIDTH), 0))
    lg_self = jnp.sum(jnp.where(own_head, q, 0.0) * kn_ref[0], axis=-1, keepdims=True)

    page_bias = [jnp.broadcast_to(bias[:, n:n + 1], (MOBA_HEADS, PAGE_SIZE)) for n in range(n_blocks)]
    lg = blk + jnp.stack(page_bias)[:, None]
    m = jnp.maximum(jnp.max(jnp.max(jnp.max(lg, axis=0), axis=0), axis=-1, keepdims=True), lg_self)
    p = jnp.exp(lg - m)
    p_self = jnp.exp(lg_self - m)
    denom = jnp.sum(jnp.sum(jnp.sum(p, axis=0), axis=0), axis=-1, keepdims=True) + p_self
    p = p / denom
    pself_ref[0] = jnp.broadcast_to(p_self / denom, (MOBA_HEADS, LANES))

    first_period = lane < n_blocks
    ids = jnp.zeros((MOBA_HEADS, LANES), F32)
    for t in range(MOBA_TOPK):
        is_t = jnp.where(rank == t, 1.0, 0.0)
        for r in range(pages_per_block):
            kept = jnp.zeros((MOBA_HEADS, PAGE_SIZE), F32)
            for n in range(n_blocks):
                kept = kept + is_t[:, n:n + 1] * p[n, r]
            psel_ref[0, t * pages_per_block + r] = kept
        blk_id = jnp.sum(jnp.where(first_period, is_t * blk_of_lane.astype(F32), 0.0), axis=-1, keepdims=True)
        ids = jnp.where(lane == t, blk_id, ids)
    ids_ref[0] = ids.astype(jnp.int32)


def _moba_sample_values_kernel(pt_ref, ids_ref, psel_ref, pself_ref, vn_ref, cv_ref, o_ref, vbuf, sem):
    b = pl.program_id(0)
    n_rows = pl.num_programs(0)
    pages_per_block = MOBA_BLOCK // PAGE_SIZE
    per_head = MOBA_TOPK * pages_per_block
    slot = lax.rem(b, 2)

    def chunk_copy(row, c, s):
        h, t, r = c // per_head, (c % per_head) // pages_per_block, c % pages_per_block
        page = pt_ref[row, ids_ref[row, h * MOBA_TOPK + t] * pages_per_block + r]
        return pltpu.make_async_copy(cv_ref.at[page, pl.ds(h * HEAD_DIM, HEAD_DIM), :], vbuf.at[s, c], sem.at[s])

    n_chunks = MOBA_HEADS * per_head

    @pl.when(b == 0)
    def _():
        for c in range(n_chunks):
            chunk_copy(0, c, 0).start()

    @pl.when(b + 1 < n_rows)
    def _():
        for c in range(n_chunks):
            chunk_copy(b + 1, c, 1 - slot).start()

    for c in range(n_chunks):
        chunk_copy(b, c, slot).wait()

    heads = []
    for h in range(MOBA_HEADS):
        acc = jnp.zeros((HEAD_DIM, PAGE_SIZE), F32)
        for tr in range(per_head):
            acc = acc + vbuf[slot, h * per_head + tr] * psel_ref[0, tr, h:h + 1, :]
        heads.append(acc)
    acc_hi, acc_lo = _split_bf16(jnp.concatenate(heads, axis=0))
    ones = jnp.ones((SUBLANES, PAGE_SIZE), BF16)
    past = (_dot_nt(ones, acc_hi) + _dot_nt(ones, acc_lo))[0:1]
    own_head = (lax.broadcasted_iota(jnp.int32, (MOBA_HEADS, MOBA_WIDTH), 1) // HEAD_DIM
                == lax.broadcasted_iota(jnp.int32, (MOBA_HEADS, MOBA_WIDTH), 0))
    p_self = jnp.sum(jnp.where(own_head, _lane_tile(pself_ref[0], MOBA_WIDTH), 0.0), axis=0, keepdims=True)
    o_ref[0] = (past + p_self * vn_ref[0]).astype(o_ref.dtype)


def _sample_specs(n_pages):
    pages_per_block = MOBA_BLOCK // PAGE_SIZE
    kept_pages = MOBA_TOPK * pages_per_block
    row_spec = pl.BlockSpec((1, 1, MOBA_WIDTH), lambda b, *_: (b, 0, 0))
    psel_spec = pl.BlockSpec((1, kept_pages, MOBA_HEADS, PAGE_SIZE), lambda b, *_: (b, 0, 0, 0))
    tile_spec = pl.BlockSpec((1, MOBA_HEADS, LANES), lambda b, *_: (b, 0, 0))
    return kept_pages, row_spec, psel_spec, tile_spec


def _outproj_ffn2_scores_kernel(pt_ref, x1_ref, yp_ref, ya_ref, ym_ref, wout_ref, n2_ref, wgu_ref, wo_ref,
                                qc_ref, q_ref, kn_ref, ck_ref,
                                y_ref, psel_ref, ids_ref, pself_ref,
                                h_s, act_s, kbuf_a, kbuf_b, qb_s, lg_s, sem):
    b = pl.program_id(0)
    n_rows = pl.num_programs(0)
    n_a, n_b = kbuf_a.shape[0], kbuf_b.shape[0]
    per_chunk = -(-(n_a + n_b) // FF_CHUNKS)
    switch_chunk = n_a // per_chunk
    assert n_a == switch_chunk * per_chunk
    issue_b = -(-n_b // switch_chunk)
    issue_a = -(-n_a // (FF_CHUNKS - switch_chunk))
    next_row = jnp.minimum(b + 1, n_rows - 1)

    def copy_a(row, j, zero=0):
        return pltpu.make_async_copy(ck_ref.at[pt_ref[row, j] + zero], kbuf_a.at[j], sem.at[0])

    def copy_b(row, j, zero=0):
        return pltpu.make_async_copy(ck_ref.at[pt_ref[row, n_a + j] + zero], kbuf_b.at[j], sem.at[1])

    @pl.when(b == 0)
    def _():
        for j in range(n_a):
            copy_a(0, j).start()

    qb_s[...] = jnp.broadcast_to(qc_ref[0], qb_s.shape)
    for j in range(n_a):
        copy_a(b, j).wait()

    def side_work(c, act):
        zero_bits = _zero_after(act[:SUBLANES, :PAGE_SIZE])
        zero_idx = pltpu.bitcast(zero_bits, jnp.int32)[0, 0]
        if c < switch_chunk:
            for j in range(c * issue_b, min((c + 1) * issue_b, n_b)):
                copy_b(b, j, zero_idx).start()
        else:
            if c == switch_chunk:
                for j in range(n_b):
                    copy_b(b, j).wait()
            first = (c - switch_chunk) * issue_a
            for j in range(first, min(first + issue_a, n_a)):
                copy_a(next_row, j, zero_idx).start()
        page_ids = range(c * per_chunk, min((c + 1) * per_chunk, n_a + n_b))
        pages = [kbuf_a.at[j] if j < n_a else kbuf_b.at[j - n_a] for j in page_ids]
        for j, tile in zip(page_ids, _score_pages(pages, qb_s, pltpu.bitcast(zero_bits, F32))):
            lg_s[j] = tile

    c1 = POOL_WIDTH
    c2 = c1 + MOBA_WIDTH
    mix = (_dot(yp_ref[...], wout_ref[:c1, :]) + _dot(ya_ref[...], wout_ref[c1:c2, :])
           + _dot(ym_ref[...], wout_ref[c2:, :]))
    x2 = x1_ref[...] + mix
    h_s[...] = _rmsnorm(x2, n2_ref[...]).astype(BF16)
    y_ref[...] = x2 + 0.5 * _swiglu(h_s, wgu_ref, wo_ref, act_s, side_work)
    _select_blocks(lg_s[...], q_ref, kn_ref, psel_ref, ids_ref, pself_ref)

    @pl.when(b == n_rows - 1)
    def _():
        for j in range(n_a):
            copy_a(next_row, j).wait()


def _outproj_ffn2_with_scores(x1, y_pool, y_moba, y_mem, tm, wout, n2, wgu, wo, page_table, q, k_new, cache_k_t):
    n = x1.shape[0]
    n_rows, n_pages = page_table.shape
    assert n // tm == n_rows
    pages_per_block = MOBA_BLOCK // PAGE_SIZE
    assert n_pages % pages_per_block == 0
    n_blocks = n_pages // pages_per_block
    assert LANES % n_blocks == 0 and n_blocks & (n_blocks - 1) == 0 and n_blocks >= MOBA_TOPK
    per_chunk = -(-n_pages // FF_CHUNKS)
    n_a = per_chunk * (FF_CHUNKS // 2)
    kept_pages, row_spec, psel_spec, tile_spec = _sample_specs(n_pages)
    col_spec = pl.BlockSpec((1, MOBA_WIDTH, 1), lambda b, *_: (b, 0, 0))

    def rows(width):
        return pl.BlockSpec((tm, width), lambda i, *_: (i, 0))

    resident = (2 * (wgu.size + wo.size + wout.size) + tm * (D_MODEL + D_FF) * 2
                + (n_pages + 1) * MOBA_WIDTH * PAGE_SIZE * 4)
    streamed = 2 * tm * (4 * 2 * D_MODEL + 2 * D_MODEL)
    return pl.pallas_call(
        _outproj_ffn2_scores_kernel,
        grid_spec=pltpu.PrefetchScalarGridSpec(
            num_scalar_prefetch=1,
            grid=(n_rows,),
            in_specs=[rows(D_MODEL), rows(POOL_WIDTH), rows(MOBA_WIDTH), rows(MEM_WIDTH),
                      _const_spec(wout.shape), _const_spec((1, D_MODEL)), _const_spec(wgu.shape),
                      _const_spec(wo.shape), col_spec, row_spec, row_spec, pl.BlockSpec(memory_space=pl.ANY)],
            out_specs=[rows(D_MODEL), psel_spec, tile_spec, tile_spec],
            scratch_shapes=[pltpu.VMEM((tm, D_MODEL), BF16), pltpu.VMEM((tm, D_FF), BF16),
                            pltpu.VMEM((n_a, MOBA_WIDTH, PAGE_SIZE), F32),
                            pltpu.VMEM((n_pages - n_a, MOBA_WIDTH, PAGE_SIZE), F32),
                            pltpu.VMEM((MOBA_WIDTH, PAGE_SIZE), F32),
                            pltpu.VMEM((n_pages, MOBA_HEADS, PAGE_SIZE), F32),
                            pltpu.SemaphoreType.DMA((2,))]),
        out_shape=[jax.ShapeDtypeStruct((n, D_MODEL), F32),
                   jax.ShapeDtypeStruct((n_rows, kept_pages, MOBA_HEADS, PAGE_SIZE), F32),
                   jax.ShapeDtypeStruct((n_rows, MOBA_HEADS, LANES), jnp.int32),
                   jax.ShapeDtypeStruct((n_rows, MOBA_HEADS, LANES), F32)],
        compiler_params=pltpu.CompilerParams(dimension_semantics=("arbitrary",),
                                             vmem_limit_bytes=_vmem_limit(resident + streamed)),
        name="outproj_ffn2_scores",
    )(page_table, x1, y_pool, y_moba, y_mem, wout, n2, wgu, wo, q.reshape(n_rows, MOBA_WIDTH, 1),
      q.reshape(n_rows, 1, MOBA_WIDTH), k_new.reshape(n_rows, 1, MOBA_WIDTH), cache_k_t)


def _moba_sample_values(page_table, p_sel, ids, p_self, v_new, cache_v_t):
    n_rows, n_pages = page_table.shape
    kept_pages, row_spec, psel_spec, tile_spec = _sample_specs(n_pages)
    any_spec = pl.BlockSpec(memory_space=pl.ANY)
    kept_ids = ids[:, :, :MOBA_TOPK].reshape(n_rows, MOBA_HEADS * MOBA_TOPK)
    return pl.pallas_call(
        _moba_sample_values_kernel,
        grid_spec=pltpu.PrefetchScalarGridSpec(
            num_scalar_prefetch=2,
            grid=(n_rows,),
            in_specs=[psel_spec, tile_spec, row_spec, any_spec],
            out_specs=row_spec,
            scratch_shapes=[pltpu.VMEM((2, MOBA_HEADS * kept_pages, HEAD_DIM, PAGE_SIZE), F32),
                            pltpu.SemaphoreType.DMA((2,))]),
        out_shape=jax.ShapeDtypeStruct((n_rows, 1, MOBA_WIDTH), BF16),
        compiler_params=pltpu.CompilerParams(dimension_semantics=("arbitrary",)),
        name="moba_sample_values",
    )(page_table, kept_ids, p_sel, p_self, v_new.reshape(n_rows, 1, MOBA_WIDTH), cache_v_t)


def _mem_kv_kernel(mem_ref, g_ref, wt_ref, kg_ref, mk_ref, mv_ref):
    h = _rmsnorm(mem_ref[...], g_ref[...]).astype(BF16)
    kv_t = _dot_nt(wt_ref[...], h)
    mk_ref[...] = _head_rmsnorm_t(kv_t[:MEM_WIDTH], _lane_tile(kg_ref[...], h.shape[0]))
    mv_ref[...] = kv_t[MEM_WIDTH:]


def _mem_kv(mem, batch, mem_len, g, w_t, kg_tab):
    feat = pl.BlockSpec((None, MEM_WIDTH, mem_len), lambda b: (b, 0, 0))
    return pl.pallas_call(
        _mem_kv_kernel,
        grid=(batch,),
        in_specs=[pl.BlockSpec((mem_len, D_MODEL), lambda b: (b, 0)), _const_spec((1, D_MODEL)),
                  _const_spec(w_t.shape), _const_spec(kg_tab.shape)],
        out_specs=[feat, feat],
        out_shape=[jax.ShapeDtypeStruct((batch, MEM_WIDTH, mem_len), F32)] * 2,
        compiler_params=pltpu.CompilerParams(dimension_semantics=("parallel",)),
        name="mem_kv",
    )(mem, g, w_t, kg_tab)


def _pool_select(lane, a2, a4, a8, a16):
    return jnp.where(lane < POOL_GROUP, a2,
                     jnp.where(lane < 2 * POOL_GROUP, a4, jnp.where(lane < 3 * POOL_GROUP, a8, a16)))


def _pool_mixer_tile(u, halo, first_pos, wbd_ref, scale_ref):
    halo_rows = halo.shape[0]
    ext = jnp.concatenate([halo, u], axis=0)
    a2 = ext + pltpu.roll(ext, 1, 0)
    a4 = a2 + pltpu.roll(a2, 2, 0)
    a8 = a4 + pltpu.roll(a4, 4, 0)
    a16 = a8 + pltpu.roll(a8, 8, 0)
    lane = lax.broadcasted_iota(jnp.int32, u.shape, 1)
    sums = _pool_select(lane, a2[halo_rows:], a4[halo_rows:], a8[halo_rows:], a16[halo_rows:])
    pos = first_pos + lax.broadcasted_iota(jnp.int32, u.shape, 0)
    window = _pool_select(lane, *POOL_WINDOWS)
    cnt = jnp.minimum(window, pos + 1).astype(F32)
    pooled = sums / cnt - u
    return _dot(pooled.astype(BF16), wbd_ref[...]) * scale_ref[...]


def _mem_attention_pair(qm, mk_ref, mv_ref, ym_ref, pair):
    tm = qm.shape[0]
    mem_len = mk_ref.shape[1]
    row_v = lax.broadcasted_iota(jnp.int32, (LANES, mem_len), 0)
    lane_q = lax.broadcasted_iota(jnp.int32, (tm, LANES), 1)
    cols = slice(pair * LANES, (pair + 1) * LANES)
    qp = qm[:, cols]
    mk_t = mk_ref[cols, :].astype(BF16)
    mv_t = mv_ref[cols, :]
    out = None
    for h in range(HEADS_PER_VREG):
        in_head = (lane_q < HEAD_DIM) if h == 0 else (lane_q >= HEAD_DIM)
        in_head_v = (row_v < HEAD_DIM) if h == 0 else (row_v >= HEAD_DIM)
        lg = _dot(jnp.where(in_head, qp, 0.0).astype(BF16), mk_t)
        p = jnp.exp(lg - jnp.max(lg, axis=-1, keepdims=True))
        o = _dot_nt(p.astype(BF16), jnp.where(in_head_v, mv_t, 0.0).astype(BF16))
        o = o / jnp.sum(p, axis=-1, keepdims=True)
        out = o if out is None else out + o
    ym_ref[:, cols] = out.astype(ym_ref.dtype)


def _mix_sample_kernel(u_ref, st_ref, qm_ref, mk_ref, mv_ref, wbd_ref, scale_ref, yp_ref, ym_ref, *, pos):
    rows = u_ref.shape[0]
    u = u_ref[...]
    tail = u
    sums = {}
    for back in range(1, POOL_BUF + 1):
        tail = tail + st_ref[POOL_BUF - back]
        if back + 1 in POOL_WINDOWS:
            sums[back + 1] = tail / float(min(back + 1, pos + 1))
    lane = lax.broadcasted_iota(jnp.int32, u.shape, 1)
    pooled = _pool_select(lane, *(sums[w] for w in POOL_WINDOWS)) - u
    yp_ref[...] = (_dot(pooled.astype(BF16), wbd_ref[...]) * scale_ref[...]).astype(yp_ref.dtype)

    row_t = lax.broadcasted_iota(jnp.int32, (SCORE_ROWS, MEM_WIDTH), 0)
    own_head = lax.broadcasted_iota(jnp.int32, (SCORE_ROWS, MEM_WIDTH), 1) // HEAD_DIM == row_t
    for r in range(rows):
        q = jnp.broadcast_to(qm_ref[r:r + 1, :], (SCORE_ROWS, MEM_WIDTH))
        q_t = jnp.where(own_head, q, 0.0).astype(BF16)
        lg = _dot(q_t, mk_ref[r].astype(BF16))
        p = jnp.exp(lg - jnp.max(lg, axis=-1, keepdims=True))
        p = p / jnp.sum(p, axis=-1, keepdims=True)
        o = _dot_nt(p.astype(BF16), mv_ref[r].astype(BF16))
        ym_ref[r:r + 1, :] = jnp.sum(jnp.where(own_head, o, 0.0), axis=0, keepdims=True).astype(ym_ref.dtype)


def _mix_sample(u, state_t, qm, mk_t, mv_t, wbd, scale, rows_per_step, pos):
    n = u.shape[0]
    mem_len = mk_t.shape[2]

    def rows(width):
        return pl.BlockSpec((rows_per_step, width), lambda i: (i, 0))

    mem = pl.BlockSpec((rows_per_step, MEM_WIDTH, mem_len), lambda i: (i, 0, 0))
    return pl.pallas_call(
        functools.partial(_mix_sample_kernel, pos=pos),
        grid=(n // rows_per_step,),
        in_specs=[rows(POOL_WIDTH), pl.BlockSpec((POOL_BUF, rows_per_step, POOL_WIDTH), lambda i: (0, i, 0)),
                  rows(MEM_WIDTH), mem, mem, _const_spec(wbd.shape), _const_spec((1, POOL_WIDTH))],
        out_specs=[rows(POOL_WIDTH), rows(MEM_WIDTH)],
        out_shape=[jax.ShapeDtypeStruct((n, POOL_WIDTH), BF16), jax.ShapeDtypeStruct((n, MEM_WIDTH), BF16)],
        compiler_params=pltpu.CompilerParams(dimension_semantics=("parallel",)),
        name="mix_sample",
    )(u, state_t, qm, mk_t, mv_t, wbd, scale)


def _rope_angles(pos):
    inv = jnp.power(jnp.float32(ROPE_THETA), -jnp.arange(ROT_HALF, dtype=F32) * (2.0 / ROT_DIM))
    return pos.astype(F32)[:, None] * inv[None, :]


def _rope_tables(pos):
    ang = _rope_angles(pos)
    cos, sin = jnp.cos(ang), jnp.sin(ang)
    rest = HEAD_DIM - ROT_DIM
    ones = jnp.ones((pos.shape[0], rest), F32)
    zeros = jnp.zeros((pos.shape[0], rest), F32)
    cos_h = jnp.concatenate([cos, cos, ones], axis=1)
    sin_h = jnp.concatenate([-sin, sin, zeros], axis=1)
    return jnp.tile(cos_h, (1, HEADS_PER_VREG)), jnp.tile(sin_h, (1, HEADS_PER_VREG))


def _block_diag_ones(width):
    head = jnp.arange(width) // HEAD_DIM
    return (head[:, None] == head[None, :]).astype(BF16)


def _tile_gain(g, width):
    return jnp.tile(g, width // HEAD_DIM).reshape(1, width)


def _gain_table(g, width):
    return jnp.broadcast_to(jnp.tile(g, width // HEAD_DIM)[:, None], (width, LANES))


def _feature_major_to_heads(a_t, n_heads):
    batch, _, tokens = a_t.shape
    return jnp.transpose(a_t.reshape(batch, n_heads, HEAD_DIM, tokens), (0, 3, 1, 2))[None]


def kernel(x_prompt, x_sample, mem_prompt, cache_k, cache_v, cache_mem_k, cache_mem_v, state_pool, page_table,
           ffn1_norm, ffn1_w_in, ffn1_w_out, mix_norm, w_in, pool_w, pool_scale, q_norm, k_norm,
           mem_norm, mem_w_kv, mem_q_norm, mem_k_norm, w_out, ffn2_norm, ffn2_w_in, ffn2_w_out):
    depth = ffn1_norm.shape[0]
    assert depth == 1
    n_p, t_p, _ = x_prompt.shape
    n_s, t_s, _ = x_sample.shape
    assert t_s == 1
    past_len = page_table.shape[1] * PAGE_SIZE
    mem_len = mem_prompt.shape[1]
    l = 0

    wgu1, wo1 = ffn1_w_in[l].astype(BF16), ffn1_w_out[l].astype(BF16)
    win = w_in[l].astype(BF16)
    c1 = POOL_WIDTH
    c4 = c1 + 3 * MOBA_WIDTH
    wnat = jnp.concatenate([win[:, :c1], win[:, c4:]], axis=1)
    wqkv_t = win[:, c1:c4].T
    wkv_t = mem_w_kv[l].astype(BF16).T
    n1, nmix, n2, nmem = (a[l].reshape(1, D_MODEL) for a in (ffn1_norm, mix_norm, ffn2_norm, mem_norm))
    qg, kg = _tile_gain(q_norm[l], MOBA_WIDTH), _tile_gain(k_norm[l], MOBA_WIDTH)
    mqg = _tile_gain(mem_q_norm[l], MEM_WIDTH)
    qg_tab, kg_tab = _gain_table(q_norm[l], MOBA_WIDTH), _gain_table(k_norm[l], MOBA_WIDTH)
    mkg_tab = _gain_table(mem_k_norm[l], MEM_WIDTH)
    ones_bd = _block_diag_ones(MOBA_WIDTH)
    ones_mem = ones_bd[:MEM_WIDTH, :MEM_WIDTH]
    wbd = jax.scipy.linalg.block_diag(*[pool_w[l, g] for g in range(len(POOL_WINDOWS))]).astype(BF16)
    scale = pool_scale[l].reshape(1, POOL_WIDTH)

    xp = x_prompt.reshape(n_p * t_p, D_MODEL)
    ang_p = _rope_angles(jnp.arange(t_p, dtype=jnp.int32)).T
    mkt_p, mvt_p = _mem_kv(mem_prompt.reshape(n_p * mem_len, D_MODEL), n_p, mem_len, nmem, wkv_t, mkg_tab)
    x1p, up, yp_p, ym_p, qt_p, kt_p, vt_p, wgu2, wo2, wout = _ffn1_inproj_t(
        xp, n_p, t_p, ROW_TILE, n1, wgu1, wo1, nmix, wnat, wqkv_t, qg_tab, kg_tab, mqg, ones_mem,
        jnp.cos(ang_p), jnp.sin(ang_p), mkt_p, mvt_p, wbd, scale, ffn2_w_in[l], ffn2_w_out[l], w_out[l])
    ya_p = _moba_prompt(qt_p, kt_p, vt_p)

    xs = x_sample.reshape(n_s, D_MODEL)
    cos_s, sin_s = _rope_tables(jnp.full((n_s,), past_len, jnp.int32))
    x1s, us, qs, ks, vs, qms = _ffn1_inproj(xs, n1, wgu1, wo1, nmix, win, qg, kg, mqg, ones_bd, cos_s, sin_s)

    def feature_major(cache, n_heads):
        rows, tokens = cache.shape[1], cache.shape[2]
        return jnp.transpose(cache[l], (0, 2, 3, 1)).reshape(rows, n_heads * HEAD_DIM, tokens)

    y_prompt, p_sel, kept_ids, p_self = _outproj_ffn2_with_scores(
        x1p, yp_p, ya_p, ym_p, ROW_TILE, wout, n2, wgu2, wo2, page_table, qs, ks, feature_major(cache_k, MOBA_HEADS))
    ya_s = _moba_sample_values(page_table, p_sel, kept_ids, p_self, vs,
                               feature_major(cache_v, MOBA_HEADS)).reshape(n_s, MOBA_WIDTH)
    state = state_pool[l]
    yp_s, ym_s = _mix_sample(us, jnp.swapaxes(state, 0, 1), qms, feature_major(cache_mem_k, MEM_HEADS),
                             feature_major(cache_mem_v, MEM_HEADS), wbd, scale, 8, past_len)
    y_sample = _outproj_ffn2(x1s, yp_s, ya_s, ym_s, n_s, wout, n2, wgu2, wo2)

    pool_state_p = up.reshape(n_p, t_p, POOL_WIDTH)[:, t_p - POOL_BUF:][None]
    pool_state_s = jnp.concatenate([state[:, 1:], us[:, None, :]], axis=1)[None]
    return (y_prompt.reshape(n_p, t_p, D_MODEL), y_sample.reshape(n_s, t_s, D_MODEL),
            _feature_major_to_heads(kt_p, MOBA_HEADS), _feature_major_to_heads(vt_p, MOBA_HEADS),
            _feature_major_to_heads(mkt_p, MEM_HEADS), _feature_major_to_heads(mvt_p, MEM_HEADS),
            pool_state_p,
            ks.reshape(1, n_s, t_s, MOBA_HEADS, HEAD_DIM), vs.reshape(1, n_s, t_s, MOBA_HEADS, HEAD_DIM),
            pool_state_s)
```

```python
import functools

import jax
import jax.numpy as jnp
from jax import lax
from jax.experimental import pallas as pl
from jax.experimental.pallas import tpu as pltpu

F32 = jnp.float32
BF16 = jnp.bfloat16

D_MODEL = 1024
HEAD_DIM = 64
POOL_WIDTH = 256
POOL_WINDOWS = (2, 4, 8, 16)
POOL_GROUP = POOL_WIDTH // len(POOL_WINDOWS)
POOL_BUF = max(POOL_WINDOWS) - 1
MOBA_WIDTH = 512
MOBA_HEADS = MOBA_WIDTH // HEAD_DIM
MOBA_BLOCK = 256
MOBA_TOPK = 3
MEM_WIDTH = 256
MEM_HEADS = MEM_WIDTH // HEAD_DIM
ROT_DIM = HEAD_DIM // 4
ROT_HALF = ROT_DIM // 2
ROPE_THETA = 500000.0
D_FF = 2816
RMS_EPS = 1e-6
PAGE_SIZE = 128
ATTN_SCALE = HEAD_DIM ** -0.5
LOG2_E = 1.4426950408889634

LANES = 128
SUBLANES = 8
V7X_VMEM_BYTES = 64 * 1024 * 1024

HEADS_PER_VREG = LANES // HEAD_DIM
MASKED = -1e30
FF_CHUNK = 256
FF_CHUNKS = D_FF // FF_CHUNK
ROW_TILE = 512
SCORE_ROWS = 16


def _dot(a, b):
    return jnp.dot(a, b, preferred_element_type=F32)


def _dot_nt(a, b):
    return lax.dot_general(a, b, (((1,), (1,)), ((), ())), preferred_element_type=F32)


def _split_bf16(x):
    hi = x.astype(BF16)
    lo = (x - hi.astype(F32)).astype(BF16)
    return hi, lo


def _vmem_limit(resident_bytes):
    return int(min(resident_bytes + 16 * 1024 * 1024, V7X_VMEM_BYTES - 4 * 1024 * 1024))


def _rmsnorm(x, g):
    ms = jnp.mean(x * x, axis=-1, keepdims=True)
    return x * lax.rsqrt(ms + RMS_EPS) * g


def _head_rmsnorm(x, g, ones_bd):
    hi, lo = _split_bf16(x * x)
    ms = (_dot(hi, ones_bd) + _dot(lo, ones_bd)) * (1.0 / HEAD_DIM)
    return x * lax.rsqrt(ms + RMS_EPS) * g


def _head_rmsnorm_t(x, g):
    heads = []
    for r in range(0, x.shape[0], HEAD_DIM):
        xh = x[r:r + HEAD_DIM]
        ms = jnp.mean(xh * xh, axis=0, keepdims=True)
        heads.append(xh * lax.rsqrt(ms + RMS_EPS))
    return jnp.concatenate(heads, axis=0) * g


def _rope(x, cos, sin):
    width = x.shape[1]
    lane = lax.broadcasted_iota(jnp.int32, x.shape, 1)
    first_half = (lane & (HEAD_DIM - 1)) < ROT_HALF
    partner = jnp.where(first_half, pltpu.roll(x, width - ROT_HALF, 1), pltpu.roll(x, ROT_HALF, 1))
    return x * cos + partner * sin


def _rope_t(x, cos, sin):
    pieces = []
    for r in range(0, x.shape[0], HEAD_DIM):
        x1, x2 = x[r:r + ROT_HALF], x[r + ROT_HALF:r + ROT_DIM]
        pieces += [x1 * cos - x2 * sin, x2 * cos + x1 * sin, x[r + ROT_DIM:r + HEAD_DIM]]
    return jnp.concatenate(pieces, axis=0)


def _lane_tile(table, n):
    return jnp.concatenate([table] * (n // LANES), axis=1)


def _swiglu(h_s, wgu_ref, wo_ref, act_s, side_work=None):
    for c in range(FF_CHUNKS):
        lo = c * FF_CHUNK
        gate = _dot(h_s[...], wgu_ref[:, lo:lo + FF_CHUNK])
        up = _dot(h_s[...], wgu_ref[:, D_FF + lo:D_FF + lo + FF_CHUNK])
        act = jax.nn.silu(gate) * up
        act_s[:, lo:lo + FF_CHUNK] = act.astype(BF16)
        if side_work is not None:
            side_work(c, act)
    return _dot(act_s[...], wo_ref[...])


def _const_spec(shape):
    return pl.BlockSpec(shape, lambda *_: (0,) * len(shape), pipeline_mode=pl.Buffered(1))


def _ffn1_trunk(x_ref, n1_ref, wgu_ref, wo_ref, nmix_ref, x1_ref, h_s, act_s):
    h_s[...] = _rmsnorm(x_ref[...], n1_ref[...]).astype(BF16)
    x1_ref[...] = x_ref[...] + 0.5 * _swiglu(h_s, wgu_ref, wo_ref, act_s)
    h_s[...] = _rmsnorm(x1_ref[...], nmix_ref[...]).astype(BF16)


def _ffn1_inproj_kernel(x_ref, n1_ref, wgu_ref, wo_ref, nmix_ref, win_ref, qg_ref, kg_ref, mg_ref,
                        ones_ref, cos_ref, sin_ref,
                        x1_ref, u_ref, q_ref, k_ref, v_ref, qm_ref, h_s, act_s):
    _ffn1_trunk(x_ref, n1_ref, wgu_ref, wo_ref, nmix_ref, x1_ref, h_s, act_s)
    c1 = POOL_WIDTH
    c2 = c1 + MOBA_WIDTH
    c3 = c2 + MOBA_WIDTH
    c4 = c3 + MOBA_WIDTH
    u_ref[...] = _dot(h_s[...], win_ref[:, :c1])
    q = _dot(h_s[...], win_ref[:, c1:c2])
    k = _dot(h_s[...], win_ref[:, c2:c3])
    v_ref[...] = _dot(h_s[...], win_ref[:, c3:c4])
    qm = _dot(h_s[...], win_ref[:, c4:])
    reps = MOBA_WIDTH // LANES
    cos = jnp.concatenate([cos_ref[...]] * reps, axis=1)
    sin = jnp.concatenate([sin_ref[...]] * reps, axis=1)
    ones = ones_ref[...]
    q_ref[...] = _rope(_head_rmsnorm(q, qg_ref[...], ones), cos, sin) * ATTN_SCALE
    k_ref[...] = _rope(_head_rmsnorm(k, kg_ref[...], ones), cos, sin)
    qm_ref[...] = _head_rmsnorm(qm, mg_ref[...], ones_ref[:MEM_WIDTH, :MEM_WIDTH]) * ATTN_SCALE


def _ffn1_inproj(x, n1, wgu, wo, nmix, win, qg, kg, mg, ones_bd, cos_tab, sin_tab):
    n = x.shape[0]

    def full(width):
        return pl.BlockSpec((n, width), lambda i: (0, 0))

    widths = (D_MODEL, POOL_WIDTH, MOBA_WIDTH, MOBA_WIDTH, MOBA_WIDTH, MEM_WIDTH)
    resident = 2 * (wgu.size + wo.size + win.size + ones_bd.size)
    return pl.pallas_call(
        _ffn1_inproj_kernel,
        grid=(1,),
        in_specs=[full(D_MODEL), _const_spec((1, D_MODEL)), _const_spec(wgu.shape), _const_spec(wo.shape),
                  _const_spec((1, D_MODEL)), _const_spec(win.shape),
                  _const_spec((1, MOBA_WIDTH)), _const_spec((1, MOBA_WIDTH)), _const_spec((1, MEM_WIDTH)),
                  _const_spec(ones_bd.shape), full(LANES), full(LANES)],
        out_specs=[full(w) for w in widths],
        out_shape=[jax.ShapeDtypeStruct((n, w), F32) for w in widths],
        scratch_shapes=[pltpu.VMEM((n, D_MODEL), BF16), pltpu.VMEM((n, D_FF), BF16)],
        compiler_params=pltpu.CompilerParams(dimension_semantics=("arbitrary",),
                                             vmem_limit_bytes=_vmem_limit(resident)),
        name="ffn1_inproj_sample",
    )(x, n1, wgu, wo, nmix, win, qg, kg, mg, ones_bd, cos_tab, sin_tab)


def _ffn1_inproj_t_kernel(x_ref, n1_ref, wgu_ref, wo_ref, nmix_ref, wnat_ref, wqkv_t_ref, qg_ref, kg_ref, mg_ref,
                          ones_ref, cos_ref, sin_ref, mk_ref, mv_ref, wbd_ref, scale_ref,
                          wgu2_f32_ref, wo2_f32_ref, wout_f32_ref,
                          x1_ref, u_ref, yp_ref, ym_ref, qt_ref, kt_ref, vt_ref, wgu2_ref, wo2_ref, wout_ref,
                          h_s, act_s, halo_s, *, tiles):
    t = lax.rem(pl.program_id(0), tiles)
    wgu2_ref[...] = wgu2_f32_ref[...].astype(BF16)
    wo2_ref[...] = wo2_f32_ref[...].astype(BF16)
    wout_ref[...] = wout_f32_ref[...].astype(BF16)

    @pl.when(pl.program_id(0) == 0)
    def _():
        halo_s[...] = jnp.zeros_like(halo_s)

    _ffn1_trunk(x_ref, n1_ref, wgu_ref, wo_ref, nmix_ref, x1_ref, h_s, act_s)
    tm = x_ref.shape[0]
    u = _dot(h_s[...], wnat_ref[:, :POOL_WIDTH])
    u_ref[...] = u
    qm = _head_rmsnorm(_dot(h_s[...], wnat_ref[:, POOL_WIDTH:]), mg_ref[...], ones_ref[...]) * ATTN_SCALE
    q_t = _dot_nt(wqkv_t_ref[:MOBA_WIDTH, :], h_s[...])
    halo = jnp.where(t > 0, halo_s[...], 0.0)
    halo_s[...] = u[tm - halo_s.shape[0]:]
    yp_ref[...] = _pool_mixer_tile(u, halo, t * tm, wbd_ref, scale_ref).astype(yp_ref.dtype)
    k_t = _dot_nt(wqkv_t_ref[MOBA_WIDTH:2 * MOBA_WIDTH, :], h_s[...])
    _mem_attention_pair(qm, mk_ref, mv_ref, ym_ref, 0)
    vt_ref[...] = _dot_nt(wqkv_t_ref[2 * MOBA_WIDTH:, :], h_s[...])
    _mem_attention_pair(qm, mk_ref, mv_ref, ym_ref, 1)
    cos, sin = cos_ref[...], sin_ref[...]
    qt_ref[...] = _rope_t(_head_rmsnorm_t(q_t, _lane_tile(qg_ref[...], tm)), cos, sin) * (ATTN_SCALE * LOG2_E)
    kt_ref[...] = _rope_t(_head_rmsnorm_t(k_t, _lane_tile(kg_ref[...], tm)), cos, sin)


def _ffn1_inproj_t(x, batch, seq, tm, n1, wgu, wo, nmix, wnat, wqkv_t, qg_tab, kg_tab, mg, ones_bd, cos_t, sin_t,
                   mk_t, mv_t, wbd, scale, wgu2_f32, wo2_f32, wout_f32):
    tiles = seq // tm
    steps = batch * tiles
    halo_rows = 16
    assert halo_rows > POOL_BUF and tm % halo_rows == 0 and seq % tm == 0
    mem_len = mk_t.shape[2]
    mem = pl.BlockSpec((None, MEM_WIDTH, mem_len), lambda i: (i // tiles, 0, 0))

    def rows(width):
        return pl.BlockSpec((tm, width), lambda i: (i, 0))

    def slab(w, bf16_rows=16):
        n_slabs = steps
        while w.shape[0] % n_slabs or (w.shape[0] // n_slabs) % bf16_rows:
            n_slabs //= 2
        return pl.BlockSpec((w.shape[0] // n_slabs, w.shape[1]), lambda i, n=n_slabs: (jnp.minimum(i, n - 1), 0))

    feat = pl.BlockSpec((None, MOBA_WIDTH, tm), lambda i: (i // tiles, 0, i % tiles))
    tab = pl.BlockSpec((ROT_HALF, tm), lambda i: (0, i % tiles))
    casts = (wgu2_f32, wo2_f32, wout_f32)
    resident = 2 * (wgu.size + wo.size + wnat.size + wqkv_t.size + ones_bd.size) + tm * (D_MODEL + D_FF) * 2
    streamed = (2 * 4 * tm * (2 * D_MODEL + POOL_WIDTH + MEM_WIDTH + 3 * MOBA_WIDTH + 2 * ROT_HALF)
                + 2 * 6 * sum(w.size for w in casts) // steps)
    feat_shape = jax.ShapeDtypeStruct((batch, MOBA_WIDTH, seq), F32)
    return pl.pallas_call(
        functools.partial(_ffn1_inproj_t_kernel, tiles=tiles),
        grid=(steps,),
        in_specs=[rows(D_MODEL), _const_spec((1, D_MODEL)), _const_spec(wgu.shape), _const_spec(wo.shape),
                  _const_spec((1, D_MODEL)), _const_spec(wnat.shape), _const_spec(wqkv_t.shape),
                  _const_spec(qg_tab.shape), _const_spec(kg_tab.shape), _const_spec((1, MEM_WIDTH)),
                  _const_spec(ones_bd.shape), tab, tab, mem, mem, _const_spec(wbd.shape),
                  _const_spec((1, POOL_WIDTH))] + [slab(w) for w in casts],
        out_specs=[rows(D_MODEL), rows(POOL_WIDTH), rows(POOL_WIDTH), rows(MEM_WIDTH), feat, feat, feat]
        + [slab(w) for w in casts],
        out_shape=[jax.ShapeDtypeStruct((batch * seq, D_MODEL), F32),
                   jax.ShapeDtypeStruct((batch * seq, POOL_WIDTH), F32),
                   jax.ShapeDtypeStruct((batch * seq, POOL_WIDTH), BF16),
                   jax.ShapeDtypeStruct((batch * seq, MEM_WIDTH), BF16)]
        + [feat_shape] * 3 + [jax.ShapeDtypeStruct(w.shape, BF16) for w in casts],
        scratch_shapes=[pltpu.VMEM((tm, D_MODEL), BF16), pltpu.VMEM((tm, D_FF), BF16),
                        pltpu.VMEM((halo_rows, POOL_WIDTH), F32)],
        compiler_params=pltpu.CompilerParams(dimension_semantics=("arbitrary",),
                                             vmem_limit_bytes=_vmem_limit(resident + streamed)),
        name="ffn1_inproj",
    )(x, n1, wgu, wo, nmix, wnat, wqkv_t, qg_tab, kg_tab, mg, ones_bd, cos_t, sin_t, mk_t, mv_t, wbd, scale, *casts)


def _outproj_ffn2_kernel(x1_ref, yp_ref, ya_ref, ym_ref, wout_ref, n2_ref, wgu_ref, wo_ref, y_ref, h_s, act_s):
    c1 = POOL_WIDTH
    c2 = c1 + MOBA_WIDTH
    mix = (_dot(yp_ref[...], wout_ref[:c1, :]) + _dot(ya_ref[...], wout_ref[c1:c2, :])
           + _dot(ym_ref[...], wout_ref[c2:, :]))
    x2 = x1_ref[...] + mix
    h_s[...] = _rmsnorm(x2, n2_ref[...]).astype(BF16)
    y_ref[...] = x2 + 0.5 * _swiglu(h_s, wgu_ref, wo_ref, act_s)


def _outproj_ffn2(x1, y_pool, y_moba, y_mem, tm, wout, n2, wgu, wo):
    n = x1.shape[0]

    def rows(width):
        return pl.BlockSpec((tm, width), lambda i: (i, 0))

    resident = 2 * (wgu.size + wo.size + wout.size) + tm * (D_MODEL + D_FF) * 2
    streamed = 2 * tm * (4 * 2 * D_MODEL + 2 * D_MODEL)
    return pl.pallas_call(
        _outproj_ffn2_kernel,
        grid=(n // tm,),
        in_specs=[rows(D_MODEL), rows(POOL_WIDTH), rows(MOBA_WIDTH), rows(MEM_WIDTH),
                  _const_spec(wout.shape), _const_spec((1, D_MODEL)), _const_spec(wgu.shape), _const_spec(wo.shape)],
        out_specs=rows(D_MODEL),
        out_shape=jax.ShapeDtypeStruct((n, D_MODEL), F32),
        scratch_shapes=[pltpu.VMEM((tm, D_MODEL), BF16), pltpu.VMEM((tm, D_FF), BF16)],
        compiler_params=pltpu.CompilerParams(dimension_semantics=("parallel",),
                                             vmem_limit_bytes=_vmem_limit(resident + streamed)),
        name="outproj_ffn2",
    )(x1, y_pool, y_moba, y_mem, wout, n2, wgu, wo)


def _moba_prompt_kernel(qt_ref, kt_ref, vt_ref, o_ref, kb_s, vb_s, km_s, s_s, p_s):
    seq = kt_ref.shape[1]
    n_blocks = seq // MOBA_BLOCK
    blocks = [slice(n * MOBA_BLOCK, (n + 1) * MOBA_BLOCK) for n in range(n_blocks)]
    lane_k = lax.broadcasted_iota(jnp.int32, (MOBA_BLOCK, LANES), 1)
    km_s[...] = jnp.zeros_like(km_s)
    for n in range(n_blocks):
        k_n = kt_ref[:, blocks[n]].T
        km_s[n:n + 1, :] = jnp.mean(k_n, axis=0, keepdims=True)
        for h in range(HEADS_PER_VREG):
            own = (lane_k < HEAD_DIM) if h == 0 else (lane_k >= HEAD_DIM)
            tag_lane = (1 - h) * HEAD_DIM + n
            kb_s[h, blocks[n], :] = jnp.where(own, k_n, jnp.where(lane_k == tag_lane, 1.0, 0.0)).astype(BF16)
    vb_s[...] = vt_ref[...].astype(BF16)

    lane_m = lax.broadcasted_iota(jnp.int32, km_s.shape, 1)
    kmean = km_s[...]
    km_split = [_split_bf16(jnp.where((lane_m < HEAD_DIM) if h == 0 else (lane_m >= HEAD_DIM), kmean, 0.0))
                for h in range(HEADS_PER_VREG)]
    blk = lax.broadcasted_iota(jnp.int32, (SUBLANES, MOBA_BLOCK), 0)
    key_id = lax.broadcasted_iota(jnp.int32, (MOBA_BLOCK, MOBA_BLOCK), 0)
    qry_id = lax.broadcasted_iota(jnp.int32, (MOBA_BLOCK, MOBA_BLOCK), 1)
    spare_rows = HEAD_DIM - SUBLANES

    def tile_rows(i, j):
        start = MOBA_BLOCK * (i * (i + 1) // 2 + j)
        return slice(start, start + MOBA_BLOCK)

    def query_operand(i, h):
        q_t = qt_ref[:, blocks[i]]
        if i == 0:
            bias = jnp.zeros((SUBLANES, MOBA_BLOCK), F32)
        else:
            q_hi, q_lo = _split_bf16(q_t)
            km_hi, km_lo = km_split[h]
            s = (_dot(km_hi, q_hi) + _dot(km_lo, q_hi) + _dot(km_hi, q_lo))[:SUBLANES]
            s = jnp.where(blk < i, s, -jnp.inf)
            rank = jnp.zeros(s.shape, jnp.int32)
            for m in range(i):
                sm = s[m:m + 1, :]
                beats = (sm > s) | ((sm == s) & (blk > m))
                rank = rank + beats.astype(jnp.int32)
            allowed = ((blk < i) & (rank < MOBA_TOPK)) | (blk == i)
            bias = jnp.where(allowed, 0.0, MASKED)
        zeros = jnp.zeros((spare_rows, MOBA_BLOCK), F32)
        if h == 0:
            return jnp.concatenate([q_t[:HEAD_DIM], bias, zeros], axis=0).astype(BF16)
        return jnp.concatenate([bias, zeros, q_t[HEAD_DIM:]], axis=0).astype(BF16)

    def logits_tile(i, h, j, q_aug):
        st = _dot(kb_s[h, blocks[j], :], q_aug)
        if j == i:
            st = jnp.where(key_id <= qry_id, st, MASKED)
        s_s[h, tile_rows(i, j), :] = st
        return jnp.max(st, axis=0, keepdims=True)

    def prob_tile(i, h, j, m_row):
        p = jnp.exp2(s_s[h, tile_rows(i, j), :] - m_row)
        p_s[h, tile_rows(i, j), :] = p.astype(BF16)
        return jnp.sum(p, axis=0, keepdims=True)

    def weighted_values(i, h, l_row):
        keys = slice(tile_rows(i, 0).start, tile_rows(i, i).stop)
        acc = _dot(vb_s[h * HEAD_DIM:(h + 1) * HEAD_DIM, :(i + 1) * MOBA_BLOCK], p_s[h, keys, :])
        return acc / l_row

    groups = [(i, h) for i in range(n_blocks) for h in range(HEADS_PER_VREG)]

    def combine(acc, new, op):
        return new if acc is None else op(acc, new)

    q_next = query_operand(*groups[0])
    m_next = None
    for j in range(groups[0][0] + 1):
        m_next = combine(m_next, logits_tile(*groups[0], j, q_next), jnp.maximum)
    outs = []
    for g, (i, h) in enumerate(groups):
        m_row, m_next, l_row = m_next, None, None
        nxt = groups[g + 1] if g + 1 < len(groups) else None
        if nxt is not None:
            q_next = query_operand(*nxt)
        for j in range(max(i + 1, nxt[0] + 1 if nxt is not None else 0)):
            if j <= i:
                l_row = combine(l_row, prob_tile(i, h, j, m_row), jnp.add)
            if nxt is not None and j <= nxt[0]:
                m_next = combine(m_next, logits_tile(*nxt, j, q_next), jnp.maximum)
        outs.append(weighted_values(i, h, l_row))
        if h == HEADS_PER_VREG - 1:
            o_ref[blocks[i], :] = jnp.concatenate(outs, axis=0).T.astype(o_ref.dtype)
            outs = []


def _moba_prompt(q_t, k_t, v_t):
    batch, _, seq = k_t.shape
    n_blocks = seq // MOBA_BLOCK
    pairs = MOBA_WIDTH // LANES
    assert n_blocks <= SUBLANES and n_blocks <= HEAD_DIM
    tile_rows = MOBA_BLOCK * (n_blocks * (n_blocks + 1) // 2)
    feat = pl.BlockSpec((None, LANES, seq), lambda b, p: (b, p, 0))
    return pl.pallas_call(
        _moba_prompt_kernel,
        grid=(batch, pairs),
        in_specs=[feat, feat, feat],
        out_specs=pl.BlockSpec((seq, LANES), lambda b, p: (b, p)),
        out_shape=jax.ShapeDtypeStruct((batch * seq, MOBA_WIDTH), BF16),
        scratch_shapes=[pltpu.VMEM((HEADS_PER_VREG, seq, LANES), BF16),
                        pltpu.VMEM((LANES, seq), BF16),
                        pltpu.VMEM((SCORE_ROWS, LANES), F32),
                        pltpu.VMEM((HEADS_PER_VREG, tile_rows, MOBA_BLOCK), F32),
                        pltpu.VMEM((HEADS_PER_VREG, tile_rows, MOBA_BLOCK), BF16)],
        compiler_params=pltpu.CompilerParams(
            dimension_semantics=("parallel", "parallel"),
            vmem_limit_bytes=_vmem_limit(HEADS_PER_VREG * tile_rows * MOBA_BLOCK * 6 + 8 * LANES * seq * 4)),
        name="moba_prompt",
    )(q_t, k_t, v_t)


def _zero_after(value):
    return lax.shift_right_logical(pltpu.bitcast(value, jnp.uint32), jnp.uint32(32))


def _score_pages(pages, qb_s, zero):
    head_row = lax.broadcasted_iota(jnp.int32, (MOBA_HEADS, PAGE_SIZE), 0)
    tiles = [jnp.zeros((MOBA_HEADS, PAGE_SIZE), F32) for _ in pages]
    for h in range(MOBA_HEADS):
        rows = slice(h * HEAD_DIM, (h + 1) * HEAD_DIM)
        q_h = (qb_s[rows, :].reshape(HEAD_DIM // SUBLANES, SUBLANES, PAGE_SIZE) + zero).reshape(HEAD_DIM, PAGE_SIZE)
        for n, page in enumerate(pages):
            logit = jnp.sum(page[rows, :] * q_h, axis=0, keepdims=True)
            tiles[n] = jnp.where(head_row == h, logit, tiles[n])
    return tiles


def _select_blocks(lg, q_ref, kn_ref, psel_ref, ids_ref, pself_ref):
    n_pages = lg.shape[0]
    pages_per_block = MOBA_BLOCK // PAGE_SIZE
    n_blocks = n_pages // pages_per_block

    blk = lg.reshape(n_blocks, pages_per_block, MOBA_HEADS, PAGE_SIZE)
    blk_sum = blk[:, 0]
    for r in range(1, pages_per_block):
        blk_sum = blk_sum + blk[:, r]
    score_col = jnp.sum(blk_sum, axis=-1, keepdims=True) * (1.0 / MOBA_BLOCK)
    lane = lax.broadcasted_iota(jnp.int32, (MOBA_HEADS, LANES), 1)
    blk_of_lane = lane & (n_blocks - 1)
    score = jnp.zeros((MOBA_HEADS, LANES), F32)
    for n in range(n_blocks):
        score = jnp.where(blk_of_lane == n, score_col[n], score)
    rank = jnp.zeros(score.shape, jnp.int32)
    for k in range(1, n_blocks):
        other = pltpu.roll(score, k, 1)
        beats = (other > score) | ((other == score) & (blk_of_lane >= k))
        rank = rank + beats.astype(jnp.int32)
    bias = jnp.where(rank < MOBA_TOPK, 0.0, MASKED)

    q = q_ref[0]
    own_head = (lax.broadcasted_iota(jnp.int32, (MOBA_HEADS, MOBA_WIDTH), 1) // HEAD_DIM
                == lax.broadcasted_iota(jnp.int32, (MOBA_HEADS, MOBA_WIDTH), 0))
    lg_self = jnp.sum(jnp.where(own_head, q, 0.0) * kn_ref[0], axis=-1, keepdims=True)

    page_bias = [jnp.broadcast_to(bias[:, n:n + 1], (MOBA_HEADS, PAGE_SIZE)) for n in range(n_blocks)]
    lg = blk + jnp.stack(page_bias)[:, None]
    m = jnp.maximum(jnp.max(jnp.max(jnp.max(lg, axis=0), axis=0), axis=-1, keepdims=True), lg_self)
    p = jnp.exp(lg - m)
    p_self = jnp.exp(lg_self - m)
    denom = jnp.sum(jnp.sum(jnp.sum(p, axis=0), axis=0), axis=-1, keepdims=True) + p_self
    p = p / denom
    pself_ref[0] = jnp.broadcast_to(p_self / denom, (MOBA_HEADS, LANES))

    first_period = lane < n_blocks
    ids = jnp.zeros((MOBA_HEADS, LANES), F32)
    for t in range(MOBA_TOPK):
        is_t = jnp.where(rank == t, 1.0, 0.0)
        for r in range(pages_per_block):
            kept = jnp.zeros((MOBA_HEADS, PAGE_SIZE), F32)
            for n in range(n_blocks):
                kept = kept + is_t[:, n:n + 1] * p[n, r]
            psel_ref[0, t * pages_per_block + r] = kept
        blk_id = jnp.sum(jnp.where(first_period, is_t * blk_of_lane.astype(F32), 0.0), axis=-1, keepdims=True)
        ids = jnp.where(lane == t, blk_id, ids)
    ids_ref[0] = ids.astype(jnp.int32)


def _moba_sample_values_kernel(pt_ref, ids_ref, psel_ref, pself_ref, vn_ref, cv_ref, o_ref, vbuf, sem):
    b = pl.program_id(0)
    n_rows = pl.num_programs(0)
    pages_per_block = MOBA_BLOCK // PAGE_SIZE
    per_head = MOBA_TOPK * pages_per_block
    slot = lax.rem(b, 2)

    def chunk_copy(row, c, s):
        h, t, r = c // per_head, (c % per_head) // pages_per_block, c % pages_per_block
        page = pt_ref[row, ids_ref[row, h * MOBA_TOPK + t] * pages_per_block + r]
        return pltpu.make_async_copy(cv_ref.at[page, pl.ds(h * HEAD_DIM, HEAD_DIM), :], vbuf.at[s, c], sem.at[s])

    n_chunks = MOBA_HEADS * per_head

    @pl.when(b == 0)
    def _():
        for c in range(n_chunks):
            chunk_copy(0, c, 0).start()

    @pl.when(b + 1 < n_rows)
    def _():
        for c in range(n_chunks):
            chunk_copy(b + 1, c, 1 - slot).start()

    for c in range(n_chunks):
        chunk_copy(b, c, slot).wait()

    heads = []
    for h in range(MOBA_HEADS):
        acc = jnp.zeros((HEAD_DIM, PAGE_SIZE), F32)
        for tr in range(per_head):
            acc = acc + vbuf[slot, h * per_head + tr] * psel_ref[0, tr, h:h + 1, :]
        heads.append(acc)
    acc_hi, acc_lo = _split_bf16(jnp.concatenate(heads, axis=0))
    ones = jnp.ones((SUBLANES, PAGE_SIZE), BF16)
    past = (_dot_nt(ones, acc_hi) + _dot_nt(ones, acc_lo))[0:1]
    own_head = (lax.broadcasted_iota(jnp.int32, (MOBA_HEADS, MOBA_WIDTH), 1) // HEAD_DIM
                == lax.broadcasted_iota(jnp.int32, (MOBA_HEADS, MOBA_WIDTH), 0))
    p_self = jnp.sum(jnp.where(own_head, _lane_tile(pself_ref[0], MOBA_WIDTH), 0.0), axis=0, keepdims=True)
    o_ref[0] = (past + p_self * vn_ref[0]).astype(o_ref.dtype)


def _sample_specs(n_pages):
    pages_per_block = MOBA_BLOCK // PAGE_SIZE
    kept_pages = MOBA_TOPK * pages_per_block
    row_spec = pl.BlockSpec((1, 1, MOBA_WIDTH), lambda b, *_: (b, 0, 0))
    psel_spec = pl.BlockSpec((1, kept_pages, MOBA_HEADS, PAGE_SIZE), lambda b, *_: (b, 0, 0, 0))
    tile_spec = pl.BlockSpec((1, MOBA_HEADS, LANES), lambda b, *_: (b, 0, 0))
    return kept_pages, row_spec, psel_spec, tile_spec


def _outproj_ffn2_scores_kernel(pt_ref, x1_ref, yp_ref, ya_ref, ym_ref, wout_ref, n2_ref, wgu_ref, wo_ref,
                                qc_ref, q_ref, kn_ref, ck_ref,
                                y_ref, psel_ref, ids_ref, pself_ref,
                                h_s, act_s, kbuf_a, kbuf_b, qb_s, lg_s, sem):
    b = pl.program_id(0)
    n_rows = pl.num_programs(0)
    n_a, n_b = kbuf_a.shape[0], kbuf_b.shape[0]
    per_chunk = -(-(n_a + n_b) // FF_CHUNKS)
    switch_chunk = n_a // per_chunk
    assert n_a == switch_chunk * per_chunk
    issue_b = -(-n_b // switch_chunk)
    issue_a = -(-n_a // (FF_CHUNKS - switch_chunk))
    next_row = jnp.minimum(b + 1, n_rows - 1)

    def copy_a(row, j, zero=0):
        return pltpu.make_async_copy(ck_ref.at[pt_ref[row, j] + zero], kbuf_a.at[j], sem.at[0])

    def copy_b(row, j, zero=0):
        return pltpu.make_async_copy(ck_ref.at[pt_ref[row, n_a + j] + zero], kbuf_b.at[j], sem.at[1])

    @pl.when(b == 0)
    def _():
        for j in range(n_a):
            copy_a(0, j).start()

    qb_s[...] = jnp.broadcast_to(qc_ref[0], qb_s.shape)
    for j in range(n_a):
        copy_a(b, j).wait()

    def side_work(c, act):
        zero_bits = _zero_after(act[:SUBLANES, :PAGE_SIZE])
        zero_idx = pltpu.bitcast(zero_bits, jnp.int32)[0, 0]
        if c < switch_chunk:
            for j in range(c * issue_b, min((c + 1) * issue_b, n_b)):
                copy_b(b, j, zero_idx).start()
        else:
            if c == switch_chunk:
                for j in range(n_b):
                    copy_b(b, j).wait()
            first = (c - switch_chunk) * issue_a
            for j in range(first, min(first + issue_a, n_a)):
                copy_a(next_row, j, zero_idx).start()
        page_ids = range(c * per_chunk, min((c + 1) * per_chunk, n_a + n_b))
        pages = [kbuf_a.at[j] if j < n_a else kbuf_b.at[j - n_a] for j in page_ids]
        for j, tile in zip(page_ids, _score_pages(pages, qb_s, pltpu.bitcast(zero_bits, F32))):
            lg_s[j] = tile

    c1 = POOL_WIDTH
    c2 = c1 + MOBA_WIDTH
    mix = (_dot(yp_ref[...], wout_ref[:c1, :]) + _dot(ya_ref[...], wout_ref[c1:c2, :])
           + _dot(ym_ref[...], wout_ref[c2:, :]))
    x2 = x1_ref[...] + mix
    h_s[...] = _rmsnorm(x2, n2_ref[...]).astype(BF16)
    y_ref[...] = x2 + 0.5 * _swiglu(h_s, wgu_ref, wo_ref, act_s, side_work)
    _select_blocks(lg_s[...], q_ref, kn_ref, psel_ref, ids_ref, pself_ref)

    @pl.when(b == n_rows - 1)
    def _():
        for j in range(n_a):
            copy_a(next_row, j).wait()


def _outproj_ffn2_with_scores(x1, y_pool, y_moba, y_mem, tm, wout, n2, wgu, wo, page_table, q, k_new, cache_k_t):
    n = x1.shape[0]
    n_rows, n_pages = page_table.shape
    assert n // tm == n_rows
    pages_per_block = MOBA_BLOCK // PAGE_SIZE
    assert n_pages % pages_per_block == 0
    n_blocks = n_pages // pages_per_block
    assert LANES % n_blocks == 0 and n_blocks & (n_blocks - 1) == 0 and n_blocks >= MOBA_TOPK
    per_chunk = -(-n_pages // FF_CHUNKS)
    n_a = per_chunk * (FF_CHUNKS // 2)
    kept_pages, row_spec, psel_spec, tile_spec = _sample_specs(n_pages)
    col_spec = pl.BlockSpec((1, MOBA_WIDTH, 1), lambda b, *_: (b, 0, 0))

    def rows(width):
        return pl.BlockSpec((tm, width), lambda i, *_: (i, 0))

    resident = (2 * (wgu.size + wo.size + wout.size) + tm * (D_MODEL + D_FF) * 2
                + (n_pages + 1) * MOBA_WIDTH * PAGE_SIZE * 4)
    streamed = 2 * tm * (4 * 2 * D_MODEL + 2 * D_MODEL)
    return pl.pallas_call(
        _outproj_ffn2_scores_kernel,
        grid_spec=pltpu.PrefetchScalarGridSpec(
            num_scalar_prefetch=1,
            grid=(n_rows,),
            in_specs=[rows(D_MODEL), rows(POOL_WIDTH), rows(MOBA_WIDTH), rows(MEM_WIDTH),
                      _const_spec(wout.shape), _const_spec((1, D_MODEL)), _const_spec(wgu.shape),
                      _const_spec(wo.shape), col_spec, row_spec, row_spec, pl.BlockSpec(memory_space=pl.ANY)],
            out_specs=[rows(D_MODEL), psel_spec, tile_spec, tile_spec],
            scratch_shapes=[pltpu.VMEM((tm, D_MODEL), BF16), pltpu.VMEM((tm, D_FF), BF16),
                            pltpu.VMEM((n_a, MOBA_WIDTH, PAGE_SIZE), F32),
                            pltpu.VMEM((n_pages - n_a, MOBA_WIDTH, PAGE_SIZE), F32),
                            pltpu.VMEM((MOBA_WIDTH, PAGE_SIZE), F32),
                            pltpu.VMEM((n_pages, MOBA_HEADS, PAGE_SIZE), F32),
                            pltpu.SemaphoreType.DMA((2,))]),
        out_shape=[jax.ShapeDtypeStruct((n, D_MODEL), F32),
                   jax.ShapeDtypeStruct((n_rows, kept_pages, MOBA_HEADS, PAGE_SIZE), F32),
                   jax.ShapeDtypeStruct((n_rows, MOBA_HEADS, LANES), jnp.int32),
                   jax.ShapeDtypeStruct((n_rows, MOBA_HEADS, LANES), F32)],
        compiler_params=pltpu.CompilerParams(dimension_semantics=("arbitrary",),
                                             vmem_limit_bytes=_vmem_limit(resident + streamed)),
        name="outproj_ffn2_scores",
    )(page_table, x1, y_pool, y_moba, y_mem, wout, n2, wgu, wo, q.reshape(n_rows, MOBA_WIDTH, 1),
      q.reshape(n_rows, 1, MOBA_WIDTH), k_new.reshape(n_rows, 1, MOBA_WIDTH), cache_k_t)


def _moba_sample_values(page_table, p_sel, ids, p_self, v_new, cache_v_t):
    n_rows, n_pages = page_table.shape
    kept_pages, row_spec, psel_spec, tile_spec = _sample_specs(n_pages)
    any_spec = pl.BlockSpec(memory_space=pl.ANY)
    kept_ids = ids[:, :, :MOBA_TOPK].reshape(n_rows, MOBA_HEADS * MOBA_TOPK)
    return pl.pallas_call(
        _moba_sample_values_kernel,
        grid_spec=pltpu.PrefetchScalarGridSpec(
            num_scalar_prefetch=2,
            grid=(n_rows,),
            in_specs=[psel_spec, tile_spec, row_spec, any_spec],
            out_specs=row_spec,
            scratch_shapes=[pltpu.VMEM((2, MOBA_HEADS * kept_pages, HEAD_DIM, PAGE_SIZE), F32),
                            pltpu.SemaphoreType.DMA((2,))]),
        out_shape=jax.ShapeDtypeStruct((n_rows, 1, MOBA_WIDTH), BF16),
        compiler_params=pltpu.CompilerParams(dimension_semantics=("arbitrary",)),
        name="moba_sample_values",
    )(page_table, kept_ids, p_sel, p_self, v_new.reshape(n_rows, 1, MOBA_WIDTH), cache_v_t)


def _mem_kv_kernel(mem_ref, g_ref, wt_ref, kg_ref, wgu_f32_ref, wo_f32_ref, win_f32_ref,
                   mk_ref, mv_ref, wgu_ref, wo_ref, win_ref, wnat_ref, wqkv_t_ref):
    h = _rmsnorm(mem_ref[...], g_ref[...]).astype(BF16)
    kv_t = _dot_nt(wt_ref[...], h)
    mk_ref[...] = _head_rmsnorm_t(kv_t[:MEM_WIDTH], _lane_tile(kg_ref[...], h.shape[0]))
    mv_ref[...] = kv_t[MEM_WIDTH:]
    wgu_ref[...] = wgu_f32_ref[...].astype(BF16)
    wo_ref[...] = wo_f32_ref[...].astype(BF16)
    win = win_f32_ref[...]
    c1 = POOL_WIDTH
    c4 = c1 + 3 * MOBA_WIDTH
    win_ref[...] = win.astype(BF16)
    wnat_ref[...] = jnp.concatenate([win[:, :c1], win[:, c4:]], axis=1).astype(BF16)
    wqkv_t_ref[...] = win[:, c1:c4].T.astype(BF16)


def _mem_kv(mem, batch, mem_len, g, w_t, kg_tab, wgu_f32, wo_f32, win_f32):
    feat = pl.BlockSpec((None, MEM_WIDTH, mem_len), lambda b: (b, 0, 0))

    def slab(w):
        assert w.shape[0] % (batch * 16) == 0
        return pl.BlockSpec((w.shape[0] // batch, w.shape[1]), lambda b: (b, 0))

    rows_in = win_f32.shape[0] // batch
    assert rows_in % LANES == 0
    nat_cols = POOL_WIDTH + MEM_WIDTH
    qkv_cols = 3 * MOBA_WIDTH
    casts = (wgu_f32, wo_f32, win_f32)
    return pl.pallas_call(
        _mem_kv_kernel,
        grid=(batch,),
        in_specs=[pl.BlockSpec((mem_len, D_MODEL), lambda b: (b, 0)), _const_spec((1, D_MODEL)),
                  _const_spec(w_t.shape), _const_spec(kg_tab.shape)] + [slab(w) for w in casts],
        out_specs=[feat, feat] + [slab(w) for w in casts]
        + [pl.BlockSpec((rows_in, nat_cols), lambda b: (b, 0)), pl.BlockSpec((qkv_cols, rows_in), lambda b: (0, b))],
        out_shape=[jax.ShapeDtypeStruct((batch, MEM_WIDTH, mem_len), F32)] * 2
        + [jax.ShapeDtypeStruct(w.shape, BF16) for w in casts]
        + [jax.ShapeDtypeStruct((win_f32.shape[0], nat_cols), BF16),
           jax.ShapeDtypeStruct((qkv_cols, win_f32.shape[0]), BF16)],
        compiler_params=pltpu.CompilerParams(
            dimension_semantics=("parallel",),
            vmem_limit_bytes=_vmem_limit(2 * 6 * sum(w.size for w in casts) // batch + 2 * w_t.size)),
        name="mem_kv",
    )(mem, g, w_t, kg_tab, *casts)


def _pool_select(lane, a2, a4, a8, a16):
    return jnp.where(lane < POOL_GROUP, a2,
                     jnp.where(lane < 2 * POOL_GROUP, a4, jnp.where(lane < 3 * POOL_GROUP, a8, a16)))


def _pool_mixer_tile(u, halo, first_pos, wbd_ref, scale_ref):
    halo_rows = halo.shape[0]
    ext = jnp.concatenate([halo, u], axis=0)
    a2 = ext + pltpu.roll(ext, 1, 0)
    a4 = a2 + pltpu.roll(a2, 2, 0)
    a8 = a4 + pltpu.roll(a4, 4, 0)
    a16 = a8 + pltpu.roll(a8, 8, 0)
    lane = lax.broadcasted_iota(jnp.int32, u.shape, 1)
    sums = _pool_select(lane, a2[halo_rows:], a4[halo_rows:], a8[halo_rows:], a16[halo_rows:])
    pos = first_pos + lax.broadcasted_iota(jnp.int32, u.shape, 0)
    window = _pool_select(lane, *POOL_WINDOWS)
    cnt = jnp.minimum(window, pos + 1).astype(F32)
    pooled = sums / cnt - u
    return _dot(pooled.astype(BF16), wbd_ref[...]) * scale_ref[...]


def _mem_attention_pair(qm, mk_ref, mv_ref, ym_ref, pair):
    tm = qm.shape[0]
    mem_len = mk_ref.shape[1]
    row_v = lax.broadcasted_iota(jnp.int32, (LANES, mem_len), 0)
    lane_q = lax.broadcasted_iota(jnp.int32, (tm, LANES), 1)
    cols = slice(pair * LANES, (pair + 1) * LANES)
    qp = qm[:, cols]
    mk_t = mk_ref[cols, :].astype(BF16)
    mv_t = mv_ref[cols, :]
    out = None
    for h in range(HEADS_PER_VREG):
        in_head = (lane_q < HEAD_DIM) if h == 0 else (lane_q >= HEAD_DIM)
        in_head_v = (row_v < HEAD_DIM) if h == 0 else (row_v >= HEAD_DIM)
        lg = _dot(jnp.where(in_head, qp, 0.0).astype(BF16), mk_t)
        p = jnp.exp(lg - jnp.max(lg, axis=-1, keepdims=True))
        o = _dot_nt(p.astype(BF16), jnp.where(in_head_v, mv_t, 0.0).astype(BF16))
        o = o / jnp.sum(p, axis=-1, keepdims=True)
        out = o if out is None else out + o
    ym_ref[:, cols] = out.astype(ym_ref.dtype)


def _mix_sample_kernel(u_ref, st_ref, qm_ref, mk_ref, mv_ref, wbd_ref, scale_ref, yp_ref, ym_ref, *, pos):
    rows = u_ref.shape[0]
    u = u_ref[...]
    tail = u
    sums = {}
    for back in range(1, POOL_BUF + 1):
        tail = tail + st_ref[POOL_BUF - back]
        if back + 1 in POOL_WINDOWS:
            sums[back + 1] = tail / float(min(back + 1, pos + 1))
    lane = lax.broadcasted_iota(jnp.int32, u.shape, 1)
    pooled = _pool_select(lane, *(sums[w] for w in POOL_WINDOWS)) - u
    yp_ref[...] = (_dot(pooled.astype(BF16), wbd_ref[...]) * scale_ref[...]).astype(yp_ref.dtype)

    row_t = lax.broadcasted_iota(jnp.int32, (SCORE_ROWS, MEM_WIDTH), 0)
    own_head = lax.broadcasted_iota(jnp.int32, (SCORE_ROWS, MEM_WIDTH), 1) // HEAD_DIM == row_t
    for r in range(rows):
        q = jnp.broadcast_to(qm_ref[r:r + 1, :], (SCORE_ROWS, MEM_WIDTH))
        q_t = jnp.where(own_head, q, 0.0).astype(BF16)
        lg = _dot(q_t, mk_ref[r].astype(BF16))
        p = jnp.exp(lg - jnp.max(lg, axis=-1, keepdims=True))
        p = p / jnp.sum(p, axis=-1, keepdims=True)
        o = _dot_nt(p.astype(BF16), mv_ref[r].astype(BF16))
        ym_ref[r:r + 1, :] = jnp.sum(jnp.where(own_head, o, 0.0), axis=0, keepdims=True).astype(ym_ref.dtype)


def _mix_sample(u, state_t, qm, mk_t, mv_t, wbd, scale, rows_per_step, pos):
    n = u.shape[0]
    mem_len = mk_t.shape[2]

    def rows(width):
        return pl.BlockSpec((rows_per_step, width), lambda i: (i, 0))

    mem = pl.BlockSpec((rows_per_step, MEM_WIDTH, mem_len), lambda i: (i, 0, 0))
    return pl.pallas_call(
        functools.partial(_mix_sample_kernel, pos=pos),
        grid=(n // rows_per_step,),
        in_specs=[rows(POOL_WIDTH), pl.BlockSpec((POOL_BUF, rows_per_step, POOL_WIDTH), lambda i: (0, i, 0)),
                  rows(MEM_WIDTH), mem, mem, _const_spec(wbd.shape), _const_spec((1, POOL_WIDTH))],
        out_specs=[rows(POOL_WIDTH), rows(MEM_WIDTH)],
        out_shape=[jax.ShapeDtypeStruct((n, POOL_WIDTH), BF16), jax.ShapeDtypeStruct((n, MEM_WIDTH), BF16)],
        compiler_params=pltpu.CompilerParams(dimension_semantics=("parallel",)),
        name="mix_sample",
    )(u, state_t, qm, mk_t, mv_t, wbd, scale)


def _rope_angles(pos):
    inv = jnp.power(jnp.float32(ROPE_THETA), -jnp.arange(ROT_HALF, dtype=F32) * (2.0 / ROT_DIM))
    return pos.astype(F32)[:, None] * inv[None, :]


def _rope_tables(pos):
    ang = _rope_angles(pos)
    cos, sin = jnp.cos(ang), jnp.sin(ang)
    rest = HEAD_DIM - ROT_DIM
    ones = jnp.ones((pos.shape[0], rest), F32)
    zeros = jnp.zeros((pos.shape[0], rest), F32)
    cos_h = jnp.concatenate([cos, cos, ones], axis=1)
    sin_h = jnp.concatenate([-sin, sin, zeros], axis=1)
    return jnp.tile(cos_h, (1, HEADS_PER_VREG)), jnp.tile(sin_h, (1, HEADS_PER_VREG))


def _block_diag_ones(width):
    head = jnp.arange(width) // HEAD_DIM
    return (head[:, None] == head[None, :]).astype(BF16)


def _tile_gain(g, width):
    return jnp.tile(g, width // HEAD_DIM).reshape(1, width)


def _gain_table(g, width):
    return jnp.broadcast_to(jnp.tile(g, width // HEAD_DIM)[:, None], (width, LANES))


def _feature_major_to_heads(a_t, n_heads):
    batch, _, tokens = a_t.shape
    return jnp.transpose(a_t.reshape(batch, n_heads, HEAD_DIM, tokens), (0, 3, 1, 2))[None]


def kernel(x_prompt, x_sample, mem_prompt, cache_k, cache_v, cache_mem_k, cache_mem_v, state_pool, page_table,
           ffn1_norm, ffn1_w_in, ffn1_w_out, mix_norm, w_in, pool_w, pool_scale, q_norm, k_norm,
           mem_norm, mem_w_kv, mem_q_norm, mem_k_norm, w_out, ffn2_norm, ffn2_w_in, ffn2_w_out):
    depth = ffn1_norm.shape[0]
    assert depth == 1
    n_p, t_p, _ = x_prompt.shape
    n_s, t_s, _ = x_sample.shape
    assert t_s == 1
    past_len = page_table.shape[1] * PAGE_SIZE
    mem_len = mem_prompt.shape[1]
    l = 0

    wkv_t = mem_w_kv[l].astype(BF16).T
    n1, nmix, n2, nmem = (a[l].reshape(1, D_MODEL) for a in (ffn1_norm, mix_norm, ffn2_norm, mem_norm))
    qg, kg = _tile_gain(q_norm[l], MOBA_WIDTH), _tile_gain(k_norm[l], MOBA_WIDTH)
    mqg = _tile_gain(mem_q_norm[l], MEM_WIDTH)
    qg_tab, kg_tab = _gain_table(q_norm[l], MOBA_WIDTH), _gain_table(k_norm[l], MOBA_WIDTH)
    mkg_tab = _gain_table(mem_k_norm[l], MEM_WIDTH)
    ones_bd = _block_diag_ones(MOBA_WIDTH)
    ones_mem = ones_bd[:MEM_WIDTH, :MEM_WIDTH]
    wbd = jax.scipy.linalg.block_diag(*[pool_w[l, g] for g in range(len(POOL_WINDOWS))]).astype(BF16)
    scale = pool_scale[l].reshape(1, POOL_WIDTH)

    xp = x_prompt.reshape(n_p * t_p, D_MODEL)
    ang_p = _rope_angles(jnp.arange(t_p, dtype=jnp.int32)).T
    mkt_p, mvt_p, wgu1, wo1, win, wnat, wqkv_t = _mem_kv(
        mem_prompt.reshape(n_p * mem_len, D_MODEL), n_p, mem_len, nmem, wkv_t, mkg_tab,
        ffn1_w_in[l], ffn1_w_out[l], w_in[l])
    x1p, up, yp_p, ym_p, qt_p, kt_p, vt_p, wgu2, wo2, wout = _ffn1_inproj_t(
        xp, n_p, t_p, ROW_TILE, n1, wgu1, wo1, nmix, wnat, wqkv_t, qg_tab, kg_tab, mqg, ones_mem,
        jnp.cos(ang_p), jnp.sin(ang_p), mkt_p, mvt_p, wbd, scale, ffn2_w_in[l], ffn2_w_out[l], w_out[l])
    ya_p = _moba_prompt(qt_p, kt_p, vt_p)

    xs = x_sample.reshape(n_s, D_MODEL)
    cos_s, sin_s = _rope_tables(jnp.full((n_s,), past_len, jnp.int32))
    x1s, us, qs, ks, vs, qms = _ffn1_inproj(xs, n1, wgu1, wo1, nmix, win, qg, kg, mqg, ones_bd, cos_s, sin_s)

    def feature_major(cache, n_heads):
        rows, tokens = cache.shape[1], cache.shape[2]
        return jnp.transpose(cache[l], (0, 2, 3, 1)).reshape(rows, n_heads * HEAD_DIM, tokens)

    y_prompt, p_sel, kept_ids, p_self = _outproj_ffn2_with_scores(
        x1p, yp_p, ya_p, ym_p, ROW_TILE, wout, n2, wgu2, wo2, page_table, qs, ks, feature_major(cache_k, MOBA_HEADS))
    ya_s = _moba_sample_values(page_table, p_sel, kept_ids, p_self, vs,
                               feature_major(cache_v, MOBA_HEADS)).reshape(n_s, MOBA_WIDTH)
    state = state_pool[l]
    yp_s, ym_s = _mix_sample(us, jnp.swapaxes(state, 0, 1), qms, feature_major(cache_mem_k, MEM_HEADS),
                             feature_major(cache_mem_v, MEM_HEADS), wbd, scale, 8, past_len)
    y_sample = _outproj_ffn2(x1s, yp_s, ya_s, ym_s, n_s, wout, n2, wgu2, wo2)

    pool_state_p = up.reshape(n_p, t_p, POOL_WIDTH)[:, t_p - POOL_BUF:][None]
    pool_state_s = jnp.concatenate([state[:, 1:], us[:, None, :]], axis=1)[None]
    return (y_prompt.reshape(n_p, t_p, D_MODEL), y_sample.reshape(n_s, t_s, D_MODEL),
            _feature_major_to_heads(kt_p, MOBA_HEADS), _feature_major_to_heads(vt_p, MOBA_HEADS),
            _feature_major_to_heads(mkt_p, MEM_HEADS), _feature_major_to_heads(mvt_p, MEM_HEADS),
            pool_state_p,
            ks.reshape(1, n_s, t_s, MOBA_HEADS, HEAD_DIM), vs.reshape(1, n_s, t_s, MOBA_HEADS, HEAD_DIM),
            pool_state_s)
```

```python
import functools

import jax
import jax.numpy as jnp
from jax import lax
from jax.experimental import pallas as pl
from jax.experimental.pallas import tpu as pltpu

F32 = jnp.float32
BF16 = jnp.bfloat16

D_MODEL = 1024
HEAD_DIM = 64
POOL_WIDTH = 256
POOL_WINDOWS = (2, 4, 8, 16)
POOL_GROUP = POOL_WIDTH // len(POOL_WINDOWS)
POOL_BUF = max(POOL_WINDOWS) - 1
MOBA_WIDTH = 512
MOBA_HEADS = MOBA_WIDTH // HEAD_DIM
MOBA_BLOCK = 256
MOBA_TOPK = 3
MEM_WIDTH = 256
MEM_HEADS = MEM_WIDTH // HEAD_DIM
ROT_DIM = HEAD_DIM // 4
ROT_HALF = ROT_DIM // 2
ROPE_THETA = 500000.0
D_FF = 2816
RMS_EPS = 1e-6
PAGE_SIZE = 128
ATTN_SCALE = HEAD_DIM ** -0.5
LOG2_E = 1.4426950408889634

LANES = 128
SUBLANES = 8
V7X_VMEM_BYTES = 64 * 1024 * 1024

HEADS_PER_VREG = LANES // HEAD_DIM
MASKED = -1e30
FF_CHUNK = 256
FF_CHUNKS = D_FF // FF_CHUNK
ROW_TILE = 512
SCORE_ROWS = 16
MOBA_PAIRS_PER_STEP = 2


def _dot(a, b):
    return jnp.dot(a, b, preferred_element_type=F32)


def _dot_nt(a, b):
    return lax.dot_general(a, b, (((1,), (1,)), ((), ())), preferred_element_type=F32)


def _split_bf16(x):
    hi = x.astype(BF16)
    lo = (x - hi.astype(F32)).astype(BF16)
    return hi, lo


def _vmem_limit(resident_bytes):
    return int(min(resident_bytes + 16 * 1024 * 1024, V7X_VMEM_BYTES - 4 * 1024 * 1024))


def _rmsnorm(x, g):
    ms = jnp.mean(x * x, axis=-1, keepdims=True)
    return x * lax.rsqrt(ms + RMS_EPS) * g


def _head_rmsnorm(x, g, ones_bd):
    hi, lo = _split_bf16(x * x)
    ms = (_dot(hi, ones_bd) + _dot(lo, ones_bd)) * (1.0 / HEAD_DIM)
    return x * lax.rsqrt(ms + RMS_EPS) * g


def _head_rmsnorm_t(x, g):
    heads = []
    for r in range(0, x.shape[0], HEAD_DIM):
        xh = x[r:r + HEAD_DIM]
        ms = jnp.mean(xh * xh, axis=0, keepdims=True)
        heads.append(xh * lax.rsqrt(ms + RMS_EPS))
    return jnp.concatenate(heads, axis=0) * g


def _rope(x, cos, sin):
    width = x.shape[1]
    lane = lax.broadcasted_iota(jnp.int32, x.shape, 1)
    first_half = (lane & (HEAD_DIM - 1)) < ROT_HALF
    partner = jnp.where(first_half, pltpu.roll(x, width - ROT_HALF, 1), pltpu.roll(x, ROT_HALF, 1))
    return x * cos + partner * sin


def _rope_t(x, cos, sin):
    pieces = []
    for r in range(0, x.shape[0], HEAD_DIM):
        x1, x2 = x[r:r + ROT_HALF], x[r + ROT_HALF:r + ROT_DIM]
        pieces += [x1 * cos - x2 * sin, x2 * cos + x1 * sin, x[r + ROT_DIM:r + HEAD_DIM]]
    return jnp.concatenate(pieces, axis=0)


def _lane_tile(table, n):
    return jnp.concatenate([table] * (n // LANES), axis=1)


def _swiglu(h_s, wgu_ref, wo_ref, act_s, side_work=None):
    for c in range(FF_CHUNKS):
        lo = c * FF_CHUNK
        gate = _dot(h_s[...], wgu_ref[:, lo:lo + FF_CHUNK])
        up = _dot(h_s[...], wgu_ref[:, D_FF + lo:D_FF + lo + FF_CHUNK])
        act = jax.nn.silu(gate) * up
        act_s[:, lo:lo + FF_CHUNK] = act.astype(BF16)
        if side_work is not None:
            side_work(c, act)
    return _dot(act_s[...], wo_ref[...])


def _const_spec(shape):
    return pl.BlockSpec(shape, lambda *_: (0,) * len(shape), pipeline_mode=pl.Buffered(1))


def _ffn1_trunk(x_ref, n1_ref, wgu_ref, wo_ref, nmix_ref, x1_ref, h_s, act_s):
    h_s[...] = _rmsnorm(x_ref[...], n1_ref[...]).astype(BF16)
    x1_ref[...] = x_ref[...] + 0.5 * _swiglu(h_s, wgu_ref, wo_ref, act_s)
    h_s[...] = _rmsnorm(x1_ref[...], nmix_ref[...]).astype(BF16)


def _ffn1_inproj_kernel(x_ref, n1_ref, wgu_ref, wo_ref, nmix_ref, win_ref, qg_ref, kg_ref, mg_ref,
                        ones_ref, cos_ref, sin_ref,
                        x1_ref, u_ref, q_ref, k_ref, v_ref, qm_ref, h_s, act_s):
    _ffn1_trunk(x_ref, n1_ref, wgu_ref, wo_ref, nmix_ref, x1_ref, h_s, act_s)
    c1 = POOL_WIDTH
    c2 = c1 + MOBA_WIDTH
    c3 = c2 + MOBA_WIDTH
    c4 = c3 + MOBA_WIDTH
    u_ref[...] = _dot(h_s[...], win_ref[:, :c1])
    q = _dot(h_s[...], win_ref[:, c1:c2])
    k = _dot(h_s[...], win_ref[:, c2:c3])
    v_ref[...] = _dot(h_s[...], win_ref[:, c3:c4])
    qm = _dot(h_s[...], win_ref[:, c4:])
    reps = MOBA_WIDTH // LANES
    cos = jnp.concatenate([cos_ref[...]] * reps, axis=1)
    sin = jnp.concatenate([sin_ref[...]] * reps, axis=1)
    ones = ones_ref[...]
    q_ref[...] = _rope(_head_rmsnorm(q, qg_ref[...], ones), cos, sin) * ATTN_SCALE
    k_ref[...] = _rope(_head_rmsnorm(k, kg_ref[...], ones), cos, sin)
    qm_ref[...] = _head_rmsnorm(qm, mg_ref[...], ones_ref[:MEM_WIDTH, :MEM_WIDTH]) * ATTN_SCALE


def _ffn1_inproj(x, n1, wgu, wo, nmix, win, qg, kg, mg, ones_bd, cos_tab, sin_tab):
    n = x.shape[0]

    def full(width):
        return pl.BlockSpec((n, width), lambda i: (0, 0))

    widths = (D_MODEL, POOL_WIDTH, MOBA_WIDTH, MOBA_WIDTH, MOBA_WIDTH, MEM_WIDTH)
    resident = 2 * (wgu.size + wo.size + win.size + ones_bd.size)
    return pl.pallas_call(
        _ffn1_inproj_kernel,
        grid=(1,),
        in_specs=[full(D_MODEL), _const_spec((1, D_MODEL)), _const_spec(wgu.shape), _const_spec(wo.shape),
                  _const_spec((1, D_MODEL)), _const_spec(win.shape),
                  _const_spec((1, MOBA_WIDTH)), _const_spec((1, MOBA_WIDTH)), _const_spec((1, MEM_WIDTH)),
                  _const_spec(ones_bd.shape), full(LANES), full(LANES)],
        out_specs=[full(w) for w in widths],
        out_shape=[jax.ShapeDtypeStruct((n, w), F32) for w in widths],
        scratch_shapes=[pltpu.VMEM((n, D_MODEL), BF16), pltpu.VMEM((n, D_FF), BF16)],
        compiler_params=pltpu.CompilerParams(dimension_semantics=("arbitrary",),
                                             vmem_limit_bytes=_vmem_limit(resident)),
        name="ffn1_inproj_sample",
    )(x, n1, wgu, wo, nmix, win, qg, kg, mg, ones_bd, cos_tab, sin_tab)


def _ffn1_inproj_t_kernel(x_ref, n1_ref, wgu_ref, wo_ref, nmix_ref, wnat_ref, wqkv_t_ref, qg_ref, kg_ref, mg_ref,
                          ones_ref, cos_ref, sin_ref, mk_ref, mv_ref, wbd_ref, scale_ref,
                          wgu2_f32_ref, wo2_f32_ref, wout_f32_ref,
                          x1_ref, u_ref, yp_ref, ym_ref, qt_ref, kt_ref, vt_ref, wgu2_ref, wo2_ref, wout_ref,
                          h_s, act_s, halo_s, *, tiles):
    t = lax.rem(pl.program_id(0), tiles)
    wgu2_ref[...] = wgu2_f32_ref[...].astype(BF16)
    wo2_ref[...] = wo2_f32_ref[...].astype(BF16)
    wout_ref[...] = wout_f32_ref[...].astype(BF16)

    @pl.when(pl.program_id(0) == 0)
    def _():
        halo_s[...] = jnp.zeros_like(halo_s)

    _ffn1_trunk(x_ref, n1_ref, wgu_ref, wo_ref, nmix_ref, x1_ref, h_s, act_s)
    tm = x_ref.shape[0]
    u = _dot(h_s[...], wnat_ref[:, :POOL_WIDTH])
    u_ref[...] = u
    qm = _head_rmsnorm(_dot(h_s[...], wnat_ref[:, POOL_WIDTH:]), mg_ref[...], ones_ref[...]) * ATTN_SCALE
    q_t = _dot_nt(wqkv_t_ref[:MOBA_WIDTH, :], h_s[...])
    halo = jnp.where(t > 0, halo_s[...], 0.0)
    halo_s[...] = u[tm - halo_s.shape[0]:]
    yp_ref[...] = _pool_mixer_tile(u, halo, t * tm, wbd_ref, scale_ref).astype(yp_ref.dtype)
    k_t = _dot_nt(wqkv_t_ref[MOBA_WIDTH:2 * MOBA_WIDTH, :], h_s[...])
    _mem_attention_pair(qm, mk_ref, mv_ref, ym_ref, 0)
    vt_ref[...] = _dot_nt(wqkv_t_ref[2 * MOBA_WIDTH:, :], h_s[...])
    _mem_attention_pair(qm, mk_ref, mv_ref, ym_ref, 1)
    cos, sin = cos_ref[...], sin_ref[...]
    qt_ref[...] = _rope_t(_head_rmsnorm_t(q_t, _lane_tile(qg_ref[...], tm)), cos, sin) * (ATTN_SCALE * LOG2_E)
    kt_ref[...] = _rope_t(_head_rmsnorm_t(k_t, _lane_tile(kg_ref[...], tm)), cos, sin)


def _ffn1_inproj_t(x, batch, seq, tm, n1, wgu, wo, nmix, wnat, wqkv_t, qg_tab, kg_tab, mg, ones_bd, cos_t, sin_t,
                   mk_t, mv_t, wbd, scale, wgu2_f32, wo2_f32, wout_f32):
    tiles = seq // tm
    steps = batch * tiles
    halo_rows = 16
    assert halo_rows > POOL_BUF and tm % halo_rows == 0 and seq % tm == 0
    mem_len = mk_t.shape[2]
    mem = pl.BlockSpec((None, MEM_WIDTH, mem_len), lambda i: (i // tiles, 0, 0))

    def rows(width):
        return pl.BlockSpec((tm, width), lambda i: (i, 0))

    def slab(w, bf16_rows=16):
        n_slabs = steps
        while w.shape[0] % n_slabs or (w.shape[0] // n_slabs) % bf16_rows:
            n_slabs //= 2
        return pl.BlockSpec((w.shape[0] // n_slabs, w.shape[1]), lambda i, n=n_slabs: (jnp.minimum(i, n - 1), 0))

    feat = pl.BlockSpec((None, MOBA_WIDTH, tm), lambda i: (i // tiles, 0, i % tiles))
    tab = pl.BlockSpec((ROT_HALF, tm), lambda i: (0, i % tiles))
    casts = (wgu2_f32, wo2_f32, wout_f32)
    resident = 2 * (wgu.size + wo.size + wnat.size + wqkv_t.size + ones_bd.size) + tm * (D_MODEL + D_FF) * 2
    streamed = (2 * 4 * tm * (2 * D_MODEL + POOL_WIDTH + MEM_WIDTH + 3 * MOBA_WIDTH + 2 * ROT_HALF)
                + 2 * 6 * sum(w.size for w in casts) // steps)
    feat_shape = jax.ShapeDtypeStruct((batch, MOBA_WIDTH, seq), F32)
    return pl.pallas_call(
        functools.partial(_ffn1_inproj_t_kernel, tiles=tiles),
        grid=(steps,),
        in_specs=[rows(D_MODEL), _const_spec((1, D_MODEL)), _const_spec(wgu.shape), _const_spec(wo.shape),
                  _const_spec((1, D_MODEL)), _const_spec(wnat.shape), _const_spec(wqkv_t.shape),
                  _const_spec(qg_tab.shape), _const_spec(kg_tab.shape), _const_spec((1, MEM_WIDTH)),
                  _const_spec(ones_bd.shape), tab, tab, mem, mem, _const_spec(wbd.shape),
                  _const_spec((1, POOL_WIDTH))] + [slab(w) for w in casts],
        out_specs=[rows(D_MODEL), rows(POOL_WIDTH), rows(POOL_WIDTH), rows(MEM_WIDTH), feat, feat, feat]
        + [slab(w) for w in casts],
        out_shape=[jax.ShapeDtypeStruct((batch * seq, D_MODEL), F32),
                   jax.ShapeDtypeStruct((batch * seq, POOL_WIDTH), F32),
                   jax.ShapeDtypeStruct((batch * seq, POOL_WIDTH), BF16),
                   jax.ShapeDtypeStruct((batch * seq, MEM_WIDTH), BF16)]
        + [feat_shape] * 3 + [jax.ShapeDtypeStruct(w.shape, BF16) for w in casts],
        scratch_shapes=[pltpu.VMEM((tm, D_MODEL), BF16), pltpu.VMEM((tm, D_FF), BF16),
                        pltpu.VMEM((halo_rows, POOL_WIDTH), F32)],
        compiler_params=pltpu.CompilerParams(dimension_semantics=("arbitrary",),
                                             vmem_limit_bytes=_vmem_limit(resident + streamed)),
        name="ffn1_inproj",
    )(x, n1, wgu, wo, nmix, wnat, wqkv_t, qg_tab, kg_tab, mg, ones_bd, cos_t, sin_t, mk_t, mv_t, wbd, scale, *casts)


def _outproj_ffn2_kernel(x1_ref, yp_ref, ya_ref, ym_ref, wout_ref, n2_ref, wgu_ref, wo_ref, y_ref, h_s, act_s):
    c1 = POOL_WIDTH
    c2 = c1 + MOBA_WIDTH
    mix = (_dot(yp_ref[...], wout_ref[:c1, :]) + _dot(ya_ref[...], wout_ref[c1:c2, :])
           + _dot(ym_ref[...], wout_ref[c2:, :]))
    x2 = x1_ref[...] + mix
    h_s[...] = _rmsnorm(x2, n2_ref[...]).astype(BF16)
    y_ref[...] = x2 + 0.5 * _swiglu(h_s, wgu_ref, wo_ref, act_s)


def _outproj_ffn2(x1, y_pool, y_moba, y_mem, tm, wout, n2, wgu, wo):
    n = x1.shape[0]

    def rows(width):
        return pl.BlockSpec((tm, width), lambda i: (i, 0))

    resident = 2 * (wgu.size + wo.size + wout.size) + tm * (D_MODEL + D_FF) * 2
    streamed = 2 * tm * (4 * 2 * D_MODEL + 2 * D_MODEL)
    return pl.pallas_call(
        _outproj_ffn2_kernel,
        grid=(n // tm,),
        in_specs=[rows(D_MODEL), rows(POOL_WIDTH), rows(MOBA_WIDTH), rows(MEM_WIDTH),
                  _const_spec(wout.shape), _const_spec((1, D_MODEL)), _const_spec(wgu.shape), _const_spec(wo.shape)],
        out_specs=rows(D_MODEL),
        out_shape=jax.ShapeDtypeStruct((n, D_MODEL), F32),
        scratch_shapes=[pltpu.VMEM((tm, D_MODEL), BF16), pltpu.VMEM((tm, D_FF), BF16)],
        compiler_params=pltpu.CompilerParams(dimension_semantics=("parallel",),
                                             vmem_limit_bytes=_vmem_limit(resident + streamed)),
        name="outproj_ffn2",
    )(x1, y_pool, y_moba, y_mem, wout, n2, wgu, wo)


def _moba_prompt_kernel(qt_ref, kt_ref, vt_ref, o_ref, kb_s, vb_s, km_s, s_s, p_s):
    seq = kt_ref.shape[1]
    n_pairs = kt_ref.shape[0] // LANES
    n_heads = n_pairs * HEADS_PER_VREG
    n_blocks = seq // MOBA_BLOCK
    blocks = [slice(n * MOBA_BLOCK, (n + 1) * MOBA_BLOCK) for n in range(n_blocks)]
    pair_rows = [slice(p * LANES, (p + 1) * LANES) for p in range(n_pairs)]
    lane_k = lax.broadcasted_iota(jnp.int32, (MOBA_BLOCK, LANES), 1)
    km_s[...] = jnp.zeros_like(km_s)
    for p in range(n_pairs):
        for n in range(n_blocks):
            k_n = kt_ref[pair_rows[p], blocks[n]].T
            km_s[p, n:n + 1, :] = jnp.mean(k_n, axis=0, keepdims=True)
            for h in range(HEADS_PER_VREG):
                own = (lane_k < HEAD_DIM) if h == 0 else (lane_k >= HEAD_DIM)
                tag_lane = (1 - h) * HEAD_DIM + n
                kb_s[p * HEADS_PER_VREG + h, blocks[n], :] = jnp.where(
                    own, k_n, jnp.where(lane_k == tag_lane, 1.0, 0.0)).astype(BF16)
    vb_s[...] = vt_ref[...].astype(BF16)

    lane_m = lax.broadcasted_iota(jnp.int32, km_s.shape[1:], 1)
    km_split = [_split_bf16(jnp.where((lane_m < HEAD_DIM) if g % HEADS_PER_VREG == 0 else (lane_m >= HEAD_DIM),
                                      km_s[g // HEADS_PER_VREG], 0.0)) for g in range(n_heads)]
    blk = lax.broadcasted_iota(jnp.int32, (SUBLANES, MOBA_BLOCK), 0)
    key_id = lax.broadcasted_iota(jnp.int32, (MOBA_BLOCK, MOBA_BLOCK), 0)
    qry_id = lax.broadcasted_iota(jnp.int32, (MOBA_BLOCK, MOBA_BLOCK), 1)
    spare_rows = HEAD_DIM - SUBLANES

    def tile_rows(i, j):
        start = MOBA_BLOCK * ((i % 2) * n_blocks + j)
        return slice(start, start + MOBA_BLOCK)

    def query_operand(i, g):
        h = g % HEADS_PER_VREG
        q_t = qt_ref[pair_rows[g // HEADS_PER_VREG], blocks[i]]
        if i == 0:
            bias = jnp.zeros((SUBLANES, MOBA_BLOCK), F32)
        else:
            q_hi, q_lo = _split_bf16(q_t)
            km_hi, km_lo = km_split[g]
            s = (_dot(km_hi, q_hi) + _dot(km_lo, q_hi) + _dot(km_hi, q_lo))[:SUBLANES]
            s = jnp.where(blk < i, s, -jnp.inf)
            rank = jnp.zeros(s.shape, jnp.int32)
            for m in range(i):
                sm = s[m:m + 1, :]
                beats = (sm > s) | ((sm == s) & (blk > m))
                rank = rank + beats.astype(jnp.int32)
            allowed = ((blk < i) & (rank < MOBA_TOPK)) | (blk == i)
            bias = jnp.where(allowed, 0.0, MASKED)
        zeros = jnp.zeros((spare_rows, MOBA_BLOCK), F32)
        if h == 0:
            return jnp.concatenate([q_t[:HEAD_DIM], bias, zeros], axis=0).astype(BF16)
        return jnp.concatenate([bias, zeros, q_t[HEAD_DIM:]], axis=0).astype(BF16)

    def logits_tile(i, h, j, q_aug):
        st = _dot(kb_s[h, blocks[j], :], q_aug)
        if j == i:
            st = jnp.where(key_id <= qry_id, st, MASKED)
        s_s[h, tile_rows(i, j), :] = st
        return jnp.max(st, axis=0, keepdims=True)

    def prob_tile(i, h, j, m_row):
        p = jnp.exp2(s_s[h, tile_rows(i, j), :] - m_row)
        p_s[h, tile_rows(i, j), :] = p.astype(BF16)
        return jnp.sum(p, axis=0, keepdims=True)

    def weighted_values(i, h, l_row):
        keys = slice(tile_rows(i, 0).start, tile_rows(i, i).stop)
        acc = _dot(vb_s[h * HEAD_DIM:(h + 1) * HEAD_DIM, :(i + 1) * MOBA_BLOCK], p_s[h, keys, :])
        return acc / l_row

    heads = range(n_heads)

    def combine(acc, new, op):
        return new if acc is None else op(acc, new)

    q_next = [query_operand(0, h) for h in heads]
    m_next = [logits_tile(0, h, 0, q_next[h]) for h in heads]
    for i in range(n_blocks):
        m_row, m_next, l_row = m_next, [None for _ in heads], [None for _ in heads]
        nxt = i + 1 if i + 1 < n_blocks else None
        if nxt is not None:
            q_next = [query_operand(nxt, h) for h in heads]
        for j in range(i + 1 if nxt is None else nxt + 1):
            for h in heads:
                if nxt is not None:
                    m_next[h] = combine(m_next[h], logits_tile(nxt, h, j, q_next[h]), jnp.maximum)
                if j <= i:
                    l_row[h] = combine(l_row[h], prob_tile(i, h, j, m_row[h]), jnp.add)
        outs = [weighted_values(i, h, l_row[h]) for h in heads]
        o_ref[blocks[i], :] = jnp.concatenate(outs, axis=0).T.astype(o_ref.dtype)


def _moba_prompt(q_t, k_t, v_t):
    batch, _, seq = k_t.shape
    n_blocks = seq // MOBA_BLOCK
    width = MOBA_PAIRS_PER_STEP * LANES
    heads = MOBA_PAIRS_PER_STEP * HEADS_PER_VREG
    assert n_blocks <= SUBLANES and n_blocks <= HEAD_DIM and MOBA_WIDTH % width == 0
    tile_rows = MOBA_BLOCK * 2 * n_blocks
    feat = pl.BlockSpec((None, width, seq), lambda b, p: (b, p, 0))
    return pl.pallas_call(
        _moba_prompt_kernel,
        grid=(batch, MOBA_WIDTH // width),
        in_specs=[feat, feat, feat],
        out_specs=pl.BlockSpec((seq, width), lambda b, p: (b, p)),
        out_shape=jax.ShapeDtypeStruct((batch * seq, MOBA_WIDTH), BF16),
        scratch_shapes=[pltpu.VMEM((heads, seq, LANES), BF16),
                        pltpu.VMEM((width, seq), BF16),
                        pltpu.VMEM((MOBA_PAIRS_PER_STEP, SCORE_ROWS, LANES), F32),
                        pltpu.VMEM((heads, tile_rows, MOBA_BLOCK), F32),
                        pltpu.VMEM((heads, tile_rows, MOBA_BLOCK), BF16)],
        compiler_params=pltpu.CompilerParams(
            dimension_semantics=("parallel", "parallel"),
            vmem_limit_bytes=_vmem_limit(heads * tile_rows * MOBA_BLOCK * 6 + 8 * width * seq * 4)),
        name="moba_prompt",
    )(q_t, k_t, v_t)


def _zero_after(value):
    return lax.shift_right_logical(pltpu.bitcast(value, jnp.uint32), jnp.uint32(32))


def _score_pages(pages, qb_s, zero):
    head_row = lax.broadcasted_iota(jnp.int32, (MOBA_HEADS, PAGE_SIZE), 0)
    tiles = [jnp.zeros((MOBA_HEADS, PAGE_SIZE), F32) for _ in pages]
    for h in range(MOBA_HEADS):
        rows = slice(h * HEAD_DIM, (h + 1) * HEAD_DIM)
        q_h = (qb_s[rows, :].reshape(HEAD_DIM // SUBLANES, SUBLANES, PAGE_SIZE) + zero).reshape(HEAD_DIM, PAGE_SIZE)
        for n, page in enumerate(pages):
            logit = jnp.sum(page[rows, :] * q_h, axis=0, keepdims=True)
            tiles[n] = jnp.where(head_row == h, logit, tiles[n])
    return tiles


def _select_blocks(lg, q_ref, kn_ref, psel_ref, ids_ref, pself_ref):
    n_pages = lg.shape[0]
    pages_per_block = MOBA_BLOCK // PAGE_SIZE
    n_blocks = n_pages // pages_per_block

    blk = lg.reshape(n_blocks, pages_per_block, MOBA_HEADS, PAGE_SIZE)
    blk_sum = blk[:, 0]
    for r in range(1, pages_per_block):
        blk_sum = blk_sum + blk[:, r]
    score_col = jnp.sum(blk_sum, axis=-1, keepdims=True) * (1.0 / MOBA_BLOCK)
    lane = lax.broadcasted_iota(jnp.int32, (MOBA_HEADS, LANES), 1)
    blk_of_lane = lane & (n_blocks - 1)
    score = jnp.zeros((MOBA_HEADS, LANES), F32)
    for n in range(n_blocks):
        score = jnp.where(blk_of_lane == n, score_col[n], score)
    rank = jnp.zeros(score.shape, jnp.int32)
    for k in range(1, n_blocks):
        other = pltpu.roll(score, k, 1)
        beats = (other > score) | ((other == score) & (blk_of_lane >= k))
        rank = rank + beats.astype(jnp.int32)
    bias = jnp.where(rank < MOBA_TOPK, 0.0, MASKED)

    q = q_ref[0]
    own_head = (lax.broadcasted_iota(jnp.int32, (MOBA_HEADS, MOBA_WIDTH), 1) // HEAD_DIM
                == lax.broadcasted_iota(jnp.int32, (MOBA_HEADS, MOBA_WIDTH), 0))
    lg_self = jnp.sum(jnp.where(own_head, q, 0.0) * kn_ref[0], axis=-1, keepdims=True)

    page_bias = [jnp.broadcast_to(bias[:, n:n + 1], (MOBA_HEADS, PAGE_SIZE)) for n in range(n_blocks)]
    lg = blk + jnp.stack(page_bias)[:, None]
    m = jnp.maximum(jnp.max(jnp.max(jnp.max(lg, axis=0), axis=0), axis=-1, keepdims=True), lg_self)
    p = jnp.exp(lg - m)
    p_self = jnp.exp(lg_self - m)
    denom = jnp.sum(jnp.sum(jnp.sum(p, axis=0), axis=0), axis=-1, keepdims=True) + p_self
    p = p / denom
    pself_ref[0] = jnp.broadcast_to(p_self / denom, (MOBA_HEADS, LANES))

    first_period = lane < n_blocks
    ids = jnp.zeros((MOBA_HEADS, LANES), F32)
    for t in range(MOBA_TOPK):
        is_t = jnp.where(rank == t, 1.0, 0.0)
        for r in range(pages_per_block):
            kept = jnp.zeros((MOBA_HEADS, PAGE_SIZE), F32)
            for n in range(n_blocks):
                kept = kept + is_t[:, n:n + 1] * p[n, r]
            psel_ref[0, t * pages_per_block + r] = kept
        blk_id = jnp.sum(jnp.where(first_period, is_t * blk_of_lane.astype(F32), 0.0), axis=-1, keepdims=True)
        ids = jnp.where(lane == t, blk_id, ids)
    ids_ref[0] = ids.astype(jnp.int32)


def _moba_sample_values_kernel(pt_ref, ids_ref, psel_ref, pself_ref, vn_ref, cv_ref, o_ref, vbuf, sem):
    b = pl.program_id(0)
    n_rows = pl.num_programs(0)
    pages_per_block = MOBA_BLOCK // PAGE_SIZE
    per_head = MOBA_TOPK * pages_per_block
    slot = lax.rem(b, 2)

    def chunk_copy(row, c, s):
        h, t, r = c // per_head, (c % per_head) // pages_per_block, c % pages_per_block
        page = pt_ref[row, ids_ref[row, h * MOBA_TOPK + t] * pages_per_block + r]
        return pltpu.make_async_copy(cv_ref.at[page, pl.ds(h * HEAD_DIM, HEAD_DIM), :], vbuf.at[s, c], sem.at[s])

    n_chunks = MOBA_HEADS * per_head

    @pl.when(b == 0)
    def _():
        for c in range(n_chunks):
            chunk_copy(0, c, 0).start()

    @pl.when(b + 1 < n_rows)
    def _():
        for c in range(n_chunks):
            chunk_copy(b + 1, c, 1 - slot).start()

    for c in range(n_chunks):
        chunk_copy(b, c, slot).wait()

    heads = []
    for h in range(MOBA_HEADS):
        acc = jnp.zeros((HEAD_DIM, PAGE_SIZE), F32)
        for tr in range(per_head):
            acc = acc + vbuf[slot, h * per_head + tr] * psel_ref[0, tr, h:h + 1, :]
        heads.append(acc)
    acc_hi, acc_lo = _split_bf16(jnp.concatenate(heads, axis=0))
    ones = jnp.ones((SUBLANES, PAGE_SIZE), BF16)
    past = (_dot_nt(ones, acc_hi) + _dot_nt(ones, acc_lo))[0:1]
    own_head = (lax.broadcasted_iota(jnp.int32, (MOBA_HEADS, MOBA_WIDTH), 1) // HEAD_DIM
                == lax.broadcasted_iota(jnp.int32, (MOBA_HEADS, MOBA_WIDTH), 0))
    p_self = jnp.sum(jnp.where(own_head, _lane_tile(pself_ref[0], MOBA_WIDTH), 0.0), axis=0, keepdims=True)
    o_ref[0] = (past + p_self * vn_ref[0]).astype(o_ref.dtype)


def _sample_specs(n_pages):
    pages_per_block = MOBA_BLOCK // PAGE_SIZE
    kept_pages = MOBA_TOPK * pages_per_block
    row_spec = pl.BlockSpec((1, 1, MOBA_WIDTH), lambda b, *_: (b, 0, 0))
    psel_spec = pl.BlockSpec((1, kept_pages, MOBA_HEADS, PAGE_SIZE), lambda b, *_: (b, 0, 0, 0))
    tile_spec = pl.BlockSpec((1, MOBA_HEADS, LANES), lambda b, *_: (b, 0, 0))
    return kept_pages, row_spec, psel_spec, tile_spec


def _outproj_ffn2_scores_kernel(pt_ref, x1_ref, yp_ref, ya_ref, ym_ref, wout_ref, n2_ref, wgu_ref, wo_ref,
                                qc_ref, q_ref, kn_ref, ck_ref,
                                y_ref, psel_ref, ids_ref, pself_ref,
                                h_s, act_s, kbuf_a, kbuf_b, qb_s, lg_s, sem):
    b = pl.program_id(0)
    n_rows = pl.num_programs(0)
    n_a, n_b = kbuf_a.shape[0], kbuf_b.shape[0]
    per_chunk = -(-(n_a + n_b) // FF_CHUNKS)
    switch_chunk = n_a // per_chunk
    assert n_a == switch_chunk * per_chunk
    issue_b = -(-n_b // switch_chunk)
    issue_a = -(-n_a // (FF_CHUNKS - switch_chunk))
    next_row = jnp.minimum(b + 1, n_rows - 1)

    def copy_a(row, j, zero=0):
        return pltpu.make_async_copy(ck_ref.at[pt_ref[row, j] + zero], kbuf_a.at[j], sem.at[0])

    def copy_b(row, j, zero=0):
        return pltpu.make_async_copy(ck_ref.at[pt_ref[row, n_a + j] + zero], kbuf_b.at[j], sem.at[1])

    @pl.when(b == 0)
    def _():
        for j in range(n_a):
            copy_a(0, j).start()

    qb_s[...] = jnp.broadcast_to(qc_ref[0], qb_s.shape)
    for j in range(n_a):
        copy_a(b, j).wait()

    def side_work(c, act):
        zero_bits = _zero_after(act[:SUBLANES, :PAGE_SIZE])
        zero_idx = pltpu.bitcast(zero_bits, jnp.int32)[0, 0]
        if c < switch_chunk:
            for j in range(c * issue_b, min((c + 1) * issue_b, n_b)):
                copy_b(b, j, zero_idx).start()
        else:
            if c == switch_chunk:
                for j in range(n_b):
                    copy_b(b, j).wait()
            first = (c - switch_chunk) * issue_a
            for j in range(first, min(first + issue_a, n_a)):
                copy_a(next_row, j, zero_idx).start()
        page_ids = range(c * per_chunk, min((c + 1) * per_chunk, n_a + n_b))
        pages = [kbuf_a.at[j] if j < n_a else kbuf_b.at[j - n_a] for j in page_ids]
        for j, tile in zip(page_ids, _score_pages(pages, qb_s, pltpu.bitcast(zero_bits, F32))):
            lg_s[j] = tile

    c1 = POOL_WIDTH
    c2 = c1 + MOBA_WIDTH
    mix = (_dot(yp_ref[...], wout_ref[:c1, :]) + _dot(ya_ref[...], wout_ref[c1:c2, :])
           + _dot(ym_ref[...], wout_ref[c2:, :]))
    x2 = x1_ref[...] + mix
    h_s[...] = _rmsnorm(x2, n2_ref[...]).astype(BF16)
    y_ref[...] = x2 + 0.5 * _swiglu(h_s, wgu_ref, wo_ref, act_s, side_work)
    _select_blocks(lg_s[...], q_ref, kn_ref, psel_ref, ids_ref, pself_ref)

    @pl.when(b == n_rows - 1)
    def _():
        for j in range(n_a):
            copy_a(next_row, j).wait()


def _outproj_ffn2_with_scores(x1, y_pool, y_moba, y_mem, tm, wout, n2, wgu, wo, page_table, q, k_new, cache_k_t):
    n = x1.shape[0]
    n_rows, n_pages = page_table.shape
    assert n // tm == n_rows
    pages_per_block = MOBA_BLOCK // PAGE_SIZE
    assert n_pages % pages_per_block == 0
    n_blocks = n_pages // pages_per_block
    assert LANES % n_blocks == 0 and n_blocks & (n_blocks - 1) == 0 and n_blocks >= MOBA_TOPK
    per_chunk = -(-n_pages // FF_CHUNKS)
    n_a = per_chunk * (FF_CHUNKS // 2)
    kept_pages, row_spec, psel_spec, tile_spec = _sample_specs(n_pages)
    col_spec = pl.BlockSpec((1, MOBA_WIDTH, 1), lambda b, *_: (b, 0, 0))

    def rows(width):
        return pl.BlockSpec((tm, width), lambda i, *_: (i, 0))

    resident = (2 * (wgu.size + wo.size + wout.size) + tm * (D_MODEL + D_FF) * 2
                + (n_pages + 1) * MOBA_WIDTH * PAGE_SIZE * 4)
    streamed = 2 * tm * (4 * 2 * D_MODEL + 2 * D_MODEL)
    return pl.pallas_call(
        _outproj_ffn2_scores_kernel,
        grid_spec=pltpu.PrefetchScalarGridSpec(
            num_scalar_prefetch=1,
            grid=(n_rows,),
            in_specs=[rows(D_MODEL), rows(POOL_WIDTH), rows(MOBA_WIDTH), rows(MEM_WIDTH),
                      _const_spec(wout.shape), _const_spec((1, D_MODEL)), _const_spec(wgu.shape),
                      _const_spec(wo.shape), col_spec, row_spec, row_spec, pl.BlockSpec(memory_space=pl.ANY)],
            out_specs=[rows(D_MODEL), psel_spec, tile_spec, tile_spec],
            scratch_shapes=[pltpu.VMEM((tm, D_MODEL), BF16), pltpu.VMEM((tm, D_FF), BF16),
                            pltpu.VMEM((n_a, MOBA_WIDTH, PAGE_SIZE), F32),
                            pltpu.VMEM((n_pages - n_a, MOBA_WIDTH, PAGE_SIZE), F32),
                            pltpu.VMEM((MOBA_WIDTH, PAGE_SIZE), F32),
                            pltpu.VMEM((n_pages, MOBA_HEADS, PAGE_SIZE), F32),
                            pltpu.SemaphoreType.DMA((2,))]),
        out_shape=[jax.ShapeDtypeStruct((n, D_MODEL), F32),
                   jax.ShapeDtypeStruct((n_rows, kept_pages, MOBA_HEADS, PAGE_SIZE), F32),
                   jax.ShapeDtypeStruct((n_rows, MOBA_HEADS, LANES), jnp.int32),
                   jax.ShapeDtypeStruct((n_rows, MOBA_HEADS, LANES), F32)],
        compiler_params=pltpu.CompilerParams(dimension_semantics=("arbitrary",),
                                             vmem_limit_bytes=_vmem_limit(resident + streamed)),
        name="outproj_ffn2_scores",
    )(page_table, x1, y_pool, y_moba, y_mem, wout, n2, wgu, wo, q.reshape(n_rows, MOBA_WIDTH, 1),
      q.reshape(n_rows, 1, MOBA_WIDTH), k_new.reshape(n_rows, 1, MOBA_WIDTH), cache_k_t)


def _moba_sample_values(page_table, p_sel, ids, p_self, v_new, cache_v_t):
    n_rows, n_pages = page_table.shape
    kept_pages, row_spec, psel_spec, tile_spec = _sample_specs(n_pages)
    any_spec = pl.BlockSpec(memory_space=pl.ANY)
    kept_ids = ids[:, :, :MOBA_TOPK].reshape(n_rows, MOBA_HEADS * MOBA_TOPK)
    return pl.pallas_call(
        _moba_sample_values_kernel,
        grid_spec=pltpu.PrefetchScalarGridSpec(
            num_scalar_prefetch=2,
            grid=(n_rows,),
            in_specs=[psel_spec, tile_spec, row_spec, any_spec],
            out_specs=row_spec,
            scratch_shapes=[pltpu.VMEM((2, MOBA_HEADS * kept_pages, HEAD_DIM, PAGE_SIZE), F32),
                            pltpu.SemaphoreType.DMA((2,))]),
        out_shape=jax.ShapeDtypeStruct((n_rows, 1, MOBA_WIDTH), BF16),
        compiler_params=pltpu.CompilerParams(dimension_semantics=("arbitrary",)),
        name="moba_sample_values",
    )(page_table, kept_ids, p_sel, p_self, v_new.reshape(n_rows, 1, MOBA_WIDTH), cache_v_t)


def _mem_kv_kernel(mem_ref, g_ref, wt_ref, kg_ref, wgu_f32_ref, wo_f32_ref, win_f32_ref,
                   mk_ref, mv_ref, wgu_ref, wo_ref, win_ref, wnat_ref, wqkv_t_ref):
    h = _rmsnorm(mem_ref[...], g_ref[...]).astype(BF16)
    kv_t = _dot_nt(wt_ref[...], h)
    mk_ref[...] = _head_rmsnorm_t(kv_t[:MEM_WIDTH], _lane_tile(kg_ref[...], h.shape[0]))
    mv_ref[...] = kv_t[MEM_WIDTH:]
    wgu_ref[...] = wgu_f32_ref[...].astype(BF16)
    wo_ref[...] = wo_f32_ref[...].astype(BF16)
    win = win_f32_ref[...]
    c1 = POOL_WIDTH
    c4 = c1 + 3 * MOBA_WIDTH
    win_ref[...] = win.astype(BF16)
    wnat_ref[...] = jnp.concatenate([win[:, :c1], win[:, c4:]], axis=1).astype(BF16)
    wqkv_t_ref[...] = win[:, c1:c4].T.astype(BF16)


def _mem_kv(mem, batch, mem_len, g, w_t, kg_tab, wgu_f32, wo_f32, win_f32):
    feat = pl.BlockSpec((None, MEM_WIDTH, mem_len), lambda b: (b, 0, 0))

    def slab(w):
        assert w.shape[0] % (batch * 16) == 0
        return pl.BlockSpec((w.shape[0] // batch, w.shape[1]), lambda b: (b, 0))

    rows_in = win_f32.shape[0] // batch
    assert rows_in % LANES == 0
    nat_cols = POOL_WIDTH + MEM_WIDTH
    qkv_cols = 3 * MOBA_WIDTH
    casts = (wgu_f32, wo_f32, win_f32)
    return pl.pallas_call(
        _mem_kv_kernel,
        grid=(batch,),
        in_specs=[pl.BlockSpec((mem_len, D_MODEL), lambda b: (b, 0)), _const_spec((1, D_MODEL)),
                  _const_spec(w_t.shape), _const_spec(kg_tab.shape)] + [slab(w) for w in casts],
        out_specs=[feat, feat] + [slab(w) for w in casts]
        + [pl.BlockSpec((rows_in, nat_cols), lambda b: (b, 0)), pl.BlockSpec((qkv_cols, rows_in), lambda b: (0, b))],
        out_shape=[jax.ShapeDtypeStruct((batch, MEM_WIDTH, mem_len), F32)] * 2
        + [jax.ShapeDtypeStruct(w.shape, BF16) for w in casts]
        + [jax.ShapeDtypeStruct((win_f32.shape[0], nat_cols), BF16),
           jax.ShapeDtypeStruct((qkv_cols, win_f32.shape[0]), BF16)],
        compiler_params=pltpu.CompilerParams(
            dimension_semantics=("parallel",),
            vmem_limit_bytes=_vmem_limit(2 * 6 * sum(w.size for w in casts) // batch + 2 * w_t.size)),
        name="mem_kv",
    )(mem, g, w_t, kg_tab, *casts)


def _pool_select(lane, a2, a4, a8, a16):
    return jnp.where(lane < POOL_GROUP, a2,
                     jnp.where(lane < 2 * POOL_GROUP, a4, jnp.where(lane < 3 * POOL_GROUP, a8, a16)))


def _pool_mixer_tile(u, halo, first_pos, wbd_ref, scale_ref):
    halo_rows = halo.shape[0]
    ext = jnp.concatenate([halo, u], axis=0)
    a2 = ext + pltpu.roll(ext, 1, 0)
    a4 = a2 + pltpu.roll(a2, 2, 0)
    a8 = a4 + pltpu.roll(a4, 4, 0)
    a16 = a8 + pltpu.roll(a8, 8, 0)
    lane = lax.broadcasted_iota(jnp.int32, u.shape, 1)
    sums = _pool_select(lane, a2[halo_rows:], a4[halo_rows:], a8[halo_rows:], a16[halo_rows:])
    pos = first_pos + lax.broadcasted_iota(jnp.int32, u.shape, 0)
    window = _pool_select(lane, *POOL_WINDOWS)
    cnt = jnp.minimum(window, pos + 1).astype(F32)
    pooled = sums / cnt - u
    return _dot(pooled.astype(BF16), wbd_ref[...]) * scale_ref[...]


def _mem_attention_pair(qm, mk_ref, mv_ref, ym_ref, pair):
    tm = qm.shape[0]
    mem_len = mk_ref.shape[1]
    row_v = lax.broadcasted_iota(jnp.int32, (LANES, mem_len), 0)
    lane_q = lax.broadcasted_iota(jnp.int32, (tm, LANES), 1)
    cols = slice(pair * LANES, (pair + 1) * LANES)
    qp = qm[:, cols]
    mk_t = mk_ref[cols, :].astype(BF16)
    mv_t = mv_ref[cols, :]
    out = None
    for h in range(HEADS_PER_VREG):
        in_head = (lane_q < HEAD_DIM) if h == 0 else (lane_q >= HEAD_DIM)
        in_head_v = (row_v < HEAD_DIM) if h == 0 else (row_v >= HEAD_DIM)
        lg = _dot(jnp.where(in_head, qp, 0.0).astype(BF16), mk_t)
        p = jnp.exp(lg - jnp.max(lg, axis=-1, keepdims=True))
        o = _dot_nt(p.astype(BF16), jnp.where(in_head_v, mv_t, 0.0).astype(BF16))
        o = o / jnp.sum(p, axis=-1, keepdims=True)
        out = o if out is None else out + o
    ym_ref[:, cols] = out.astype(ym_ref.dtype)


def _mix_sample_kernel(u_ref, st_ref, qm_ref, mk_ref, mv_ref, wbd_ref, scale_ref, yp_ref, ym_ref, *, pos):
    rows = u_ref.shape[0]
    u = u_ref[...]
    tail = u
    sums = {}
    for back in range(1, POOL_BUF + 1):
        tail = tail + st_ref[POOL_BUF - back]
        if back + 1 in POOL_WINDOWS:
            sums[back + 1] = tail / float(min(back + 1, pos + 1))
    lane = lax.broadcasted_iota(jnp.int32, u.shape, 1)
    pooled = _pool_select(lane, *(sums[w] for w in POOL_WINDOWS)) - u
    yp_ref[...] = (_dot(pooled.astype(BF16), wbd_ref[...]) * scale_ref[...]).astype(yp_ref.dtype)

    row_t = lax.broadcasted_iota(jnp.int32, (SCORE_ROWS, MEM_WIDTH), 0)
    own_head = lax.broadcasted_iota(jnp.int32, (SCORE_ROWS, MEM_WIDTH), 1) // HEAD_DIM == row_t
    for r in range(rows):
        q = jnp.broadcast_to(qm_ref[r:r + 1, :], (SCORE_ROWS, MEM_WIDTH))
        q_t = jnp.where(own_head, q, 0.0).astype(BF16)
        lg = _dot(q_t, mk_ref[r].astype(BF16))
        p = jnp.exp(lg - jnp.max(lg, axis=-1, keepdims=True))
        p = p / jnp.sum(p, axis=-1, keepdims=True)
        o = _dot_nt(p.astype(BF16), mv_ref[r].astype(BF16))
        ym_ref[r:r + 1, :] = jnp.sum(jnp.where(own_head, o, 0.0), axis=0, keepdims=True).astype(ym_ref.dtype)


def _mix_sample(u, state_t, qm, mk_t, mv_t, wbd, scale, rows_per_step, pos):
    n = u.shape[0]
    mem_len = mk_t.shape[2]

    def rows(width):
        return pl.BlockSpec((rows_per_step, width), lambda i: (i, 0))

    mem = pl.BlockSpec((rows_per_step, MEM_WIDTH, mem_len), lambda i: (i, 0, 0))
    return pl.pallas_call(
        functools.partial(_mix_sample_kernel, pos=pos),
        grid=(n // rows_per_step,),
        in_specs=[rows(POOL_WIDTH), pl.BlockSpec((POOL_BUF, rows_per_step, POOL_WIDTH), lambda i: (0, i, 0)),
                  rows(MEM_WIDTH), mem, mem, _const_spec(wbd.shape), _const_spec((1, POOL_WIDTH))],
        out_specs=[rows(POOL_WIDTH), rows(MEM_WIDTH)],
        out_shape=[jax.ShapeDtypeStruct((n, POOL_WIDTH), BF16), jax.ShapeDtypeStruct((n, MEM_WIDTH), BF16)],
        compiler_params=pltpu.CompilerParams(dimension_semantics=("parallel",)),
        name="mix_sample",
    )(u, state_t, qm, mk_t, mv_t, wbd, scale)


def _rope_angles(pos):
    inv = jnp.power(jnp.float32(ROPE_THETA), -jnp.arange(ROT_HALF, dtype=F32) * (2.0 / ROT_DIM))
    return pos.astype(F32)[:, None] * inv[None, :]


def _rope_tables(pos):
    ang = _rope_angles(pos)
    cos, sin = jnp.cos(ang), jnp.sin(ang)
    rest = HEAD_DIM - ROT_DIM
    ones = jnp.ones((pos.shape[0], rest), F32)
    zeros = jnp.zeros((pos.shape[0], rest), F32)
    cos_h = jnp.concatenate([cos, cos, ones], axis=1)
    sin_h = jnp.concatenate([-sin, sin, zeros], axis=1)
    return jnp.tile(cos_h, (1, HEADS_PER_VREG)), jnp.tile(sin_h, (1, HEADS_PER_VREG))


def _block_diag_ones(width):
    head = jnp.arange(width) // HEAD_DIM
    return (head[:, None] == head[None, :]).astype(BF16)


def _tile_gain(g, width):
    return jnp.tile(g, width // HEAD_DIM).reshape(1, width)


def _gain_table(g, width):
    return jnp.broadcast_to(jnp.tile(g, width // HEAD_DIM)[:, None], (width, LANES))


def _feature_major_to_heads(a_t, n_heads):
    batch, _, tokens = a_t.shape
    return jnp.transpose(a_t.reshape(batch, n_heads, HEAD_DIM, tokens), (0, 3, 1, 2))[None]


def kernel(x_prompt, x_sample, mem_prompt, cache_k, cache_v, cache_mem_k, cache_mem_v, state_pool, page_table,
           ffn1_norm, ffn1_w_in, ffn1_w_out, mix_norm, w_in, pool_w, pool_scale, q_norm, k_norm,
           mem_norm, mem_w_kv, mem_q_norm, mem_k_norm, w_out, ffn2_norm, ffn2_w_in, ffn2_w_out):
    depth = ffn1_norm.shape[0]
    assert depth == 1
    n_p, t_p, _ = x_prompt.shape
    n_s, t_s, _ = x_sample.shape
    assert t_s == 1
    past_len = page_table.shape[1] * PAGE_SIZE
    mem_len = mem_prompt.shape[1]
    l = 0

    wkv_t = mem_w_kv[l].astype(BF16).T
    n1, nmix, n2, nmem = (a[l].reshape(1, D_MODEL) for a in (ffn1_norm, mix_norm, ffn2_norm, mem_norm))
    qg, kg = _tile_gain(q_norm[l], MOBA_WIDTH), _tile_gain(k_norm[l], MOBA_WIDTH)
    mqg = _tile_gain(mem_q_norm[l], MEM_WIDTH)
    qg_tab, kg_tab = _gain_table(q_norm[l], MOBA_WIDTH), _gain_table(k_norm[l], MOBA_WIDTH)
    mkg_tab = _gain_table(mem_k_norm[l], MEM_WIDTH)
    ones_bd = _block_diag_ones(MOBA_WIDTH)
    ones_mem = ones_bd[:MEM_WIDTH, :MEM_WIDTH]
    wbd = jax.scipy.linalg.block_diag(*[pool_w[l, g] for g in range(len(POOL_WINDOWS))]).astype(BF16)
    scale = pool_scale[l].reshape(1, POOL_WIDTH)

    xp = x_prompt.reshape(n_p * t_p, D_MODEL)
    ang_p = _rope_angles(jnp.arange(t_p, dtype=jnp.int32)).T
    mkt_p, mvt_p, wgu1, wo1, win, wnat, wqkv_t = _mem_kv(
        mem_prompt.reshape(n_p * mem_len, D_MODEL), n_p, mem_len, nmem, wkv_t, mkg_tab,
        ffn1_w_in[l], ffn1_w_out[l], w_in[l])
    x1p, up, yp_p, ym_p, qt_p, kt_p, vt_p, wgu2, wo2, wout = _ffn1_inproj_t(
        xp, n_p, t_p, ROW_TILE, n1, wgu1, wo1, nmix, wnat, wqkv_t, qg_tab, kg_tab, mqg, ones_mem,
        jnp.cos(ang_p), jnp.sin(ang_p), mkt_p, mvt_p, wbd, scale, ffn2_w_in[l], ffn2_w_out[l], w_out[l])
    ya_p = _moba_prompt(qt_p, kt_p, vt_p)

    xs = x_sample.reshape(n_s, D_MODEL)
    cos_s, sin_s = _rope_tables(jnp.full((n_s,), past_len, jnp.int32))
    x1s, us, qs, ks, vs, qms = _ffn1_inproj(xs, n1, wgu1, wo1, nmix, win, qg, kg, mqg, ones_bd, cos_s, sin_s)

    def feature_major(cache, n_heads):
        rows, tokens = cache.shape[1], cache.shape[2]
        return jnp.transpose(cache[l], (0, 2, 3, 1)).reshape(rows, n_heads * HEAD_DIM, tokens)

    y_prompt, p_sel, kept_ids, p_self = _outproj_ffn2_with_scores(
        x1p, yp_p, ya_p, ym_p, ROW_TILE, wout, n2, wgu2, wo2, page_table, qs, ks, feature_major(cache_k, MOBA_HEADS))
    ya_s = _moba_sample_values(page_table, p_sel, kept_ids, p_self, vs,
                               feature_major(cache_v, MOBA_HEADS)).reshape(n_s, MOBA_WIDTH)
    state = state_pool[l]
    yp_s, ym_s = _mix_sample(us, jnp.swapaxes(state, 0, 1), qms, feature_major(cache_mem_k, MEM_HEADS),
                             feature_major(cache_mem_v, MEM_HEADS), wbd, scale, 8, past_len)
    y_sample = _outproj_ffn2(x1s, yp_s, ya_s, ym_s, n_s, wout, n2, wgu2, wo2)

    pool_state_p = up.reshape(n_p, t_p, POOL_WIDTH)[:, t_p - POOL_BUF:][None]
    pool_state_s = jnp.concatenate([state[:, 1:], us[:, None, :]], axis=1)[None]
    return (y_prompt.reshape(n_p, t_p, D_MODEL), y_sample.reshape(n_s, t_s, D_MODEL),
            _feature_major_to_heads(kt_p, MOBA_HEADS), _feature_major_to_heads(vt_p, MOBA_HEADS),
            _feature_major_to_heads(mkt_p, MEM_HEADS), _feature_major_to_heads(mvt_p, MEM_HEADS),
            pool_state_p,
            ks.reshape(1, n_s, t_s, MOBA_HEADS, HEAD_DIM), vs.reshape(1, n_s, t_s, MOBA_HEADS, HEAD_DIM),
            pool_state_s)
```
